```python
import math
import jax, jax.numpy as jnp
from jax import lax
import numpy as np

D_MODEL = 1024
BATCH = 8
SEQ = 4096
DEPTH = 4
DEC_BATCH = 16
DEC_SEQ = 16
PAST_LEN = 2048

CHUNK = 64
CONV_W = 4
NORM_EPS = 1e-6
N_BRANCH = 3

D_M = D_MODEL
H_M = 4
DH_M = D_M // H_M
QKV_BLOCK = 4
NB_M = D_M // QKV_BLOCK

D_S = D_MODEL
P_S = 64
H_S = D_S // P_S
G_S = 4
N_S = 128
D_CONV_S = D_S + 2 * G_S * N_S

D_R = D_MODEL
HD_R = 64
H_R = D_R // HD_R
LORA_W = 64
LORA_A = 64
N_SHIFT = 3 * D_R + LORA_W + LORA_A
RWKV_GN_EPS = 64e-5

OFF_GATE = 0
OFF_M_X = OFF_GATE + N_BRANCH * D_MODEL
OFF_M_Z = OFF_M_X + D_M
OFF_M_O = OFF_M_Z + D_M
OFF_M_IF = OFF_M_O + D_M
OFF_S_Z = OFF_M_IF + 2 * H_M
OFF_S_XBC = OFF_S_Z + D_S
OFF_S_DT = OFF_S_XBC + D_CONV_S
OFF_R_SH = OFF_S_DT + H_S
OFF_R_Z = OFF_R_SH + N_SHIFT
N_IN = OFF_R_Z + D_R

kernel_name = 'hybrid_mlstm_ssd_rwkv7_stream_step'


def _rmsnorm(x, w, eps):
    xf = x.astype(jnp.float32)
    y = xf * lax.rsqrt(jnp.mean(xf * xf, axis=-1, keepdims=True) + eps)
    return (y * w.astype(jnp.float32)).astype(x.dtype)


def _head_layernorm(u, eps):
    mu = jnp.mean(u, axis=-1, keepdims=True)
    var = jnp.mean(jnp.square(u - mu), axis=-1, keepdims=True)
    return (u - mu) * lax.rsqrt(var + eps)


def _group_rmsnorm(u, groups, eps):
    s = u.shape
    g = u.reshape(s[:-1] + (groups, s[-1] // groups))
    g = g * lax.rsqrt(jnp.mean(g * g, axis=-1, keepdims=True) + eps)
    return g.reshape(s)


def _causal_conv(u, prev, w, b):
    T = u.shape[1]
    up = jnp.concatenate([prev.astype(u.dtype), u], axis=1)
    y = b + up[:, 0:T] * w[0]
    for j in range(1, w.shape[0]):
        y = y + up[:, j:j + T] * w[j]
    return y, up[:, T:]


def _blockdiag(a, w):
    B, T, _ = a.shape
    out = jnp.einsum('btnc,ncd->btnd', a.reshape(B, T, w.shape[0], w.shape[1]), w)
    return out.reshape(B, T, -1)


def _chunk_len(T):
    return CHUNK if T % CHUNK == 0 else T


def _to_chunks(a, L):
    B, T = a.shape[:2]
    return jnp.moveaxis(a.reshape((B, T // L, L) + a.shape[2:]), 1, 0)


def _from_chunks(a):
    nc, B, L = a.shape[:3]
    return jnp.moveaxis(a, 0, 1).reshape((B, nc * L) + a.shape[3:])


def _mlstm_chunked(q, k, v, i_pre, f_pre, C0, n0, m0):
    L = _chunk_len(q.shape[1])
    causal = jnp.tril(jnp.ones((L, L), dtype=bool))
    logf = jax.nn.log_sigmoid(f_pre)

    def step(carry, inp):
        C, n, m = carry
        qc, kc, vc, ic, fc = inp
        b = jnp.swapaxes(jnp.cumsum(fc, axis=1), 1, 2)
        ih = jnp.swapaxes(ic, 1, 2)
        logD = jnp.where(causal, b[..., :, None] - b[..., None, :] + ih[..., None, :], -jnp.inf)
        inter = b + m[..., None]
        m_t = jnp.maximum(inter, jnp.max(logD, axis=-1))
        Dw = jnp.exp(logD - m_t[..., None])
        sc = jnp.exp(inter - m_t)
        s = jnp.einsum('blhd,bjhd->bhlj', qc, kc) * Dw
        num = (jnp.einsum('bhlj,bjhv->blhv', s, vc)
               + jnp.einsum('blhk,bhkv->blhv', qc, C) * jnp.swapaxes(sc, 1, 2)[..., None])
        den = jnp.sum(s, axis=-1) + sc * jnp.einsum('blhk,bhk->bhl', qc, n)
        hden = jnp.maximum(jnp.abs(den), jnp.exp(-m_t))
        hc = num / jnp.swapaxes(hden, 1, 2)[..., None]
        bL = b[..., -1]
        wj = bL[..., None] - b + ih
        m_new = jnp.maximum(bL + m, jnp.max(wj, axis=-1))
        dec = jnp.exp(bL + m - m_new)
        wj = jnp.exp(wj - m_new[..., None])
        C = dec[..., None, None] * C + jnp.einsum('bhj,bjhk,bjhv->bhkv', wj, kc, vc)
        n = dec[..., None] * n + jnp.einsum('bhj,bjhk->bhk', wj, kc)
        return (C, n, m_new), hc

    xs = tuple(_to_chunks(a, L) for a in (q, k, v, i_pre, logf))
    (C, n, m), h = lax.scan(step, (C0, n0, m0), xs)
    return _from_chunks(h), C, n, m


def _ssd_chunked(x, dt, A, Bm, Cm, S0):
    Bsz, T, H, P = x.shape
    G, E = G_S, H // G_S
    L = _chunk_len(T)
    causal = jnp.tril(jnp.ones((L, L), dtype=bool))

    def step(S, inp):
        xc, dtc, Bc, Cc = inp
        cs = jnp.swapaxes(jnp.cumsum(dtc * A, axis=1), 1, 2)
        seg = jnp.exp(jnp.where(causal, cs[..., :, None] - cs[..., None, :], -jnp.inf))
        seg = seg.reshape(Bsz, G, E, L, L)
        xg = xc.reshape(Bsz, L, G, E, P)
        dg = dtc.reshape(Bsz, L, G, E)
        Sg = S.reshape(Bsz, G, E, P, N_S)
        CB = jnp.einsum('blgn,bjgn->bglj', Cc, Bc)
        y = jnp.einsum('bglj,bgelj,bjge,bjgep->blgep', CB, seg, dg, xg)
        y = y + jnp.einsum('blgn,bgepn,bgel->blgep', Cc, Sg, jnp.exp(cs).reshape(Bsz, G, E, L))
        csL = cs[..., -1:]
        wj = jnp.exp(csL - cs).reshape(Bsz, G, E, L)
        Sg = (jnp.exp(csL[..., 0]).reshape(Bsz, G, E)[..., None, None] * Sg
              + jnp.einsum('bgej,bjgn,bjge,bjgep->bgepn', wj, Bc, dg, xg))
        return Sg.reshape(Bsz, H, P, N_S), y.reshape(Bsz, L, H, P)

    xs = tuple(_to_chunks(a, L) for a in (x, dt, Bm, Cm))
    S, y = lax.scan(step, S0, xs)
    return _from_chunks(y), S


def _rwkv7_scan(r, w, k, v, a, b, S0):
    def step(S, inp):
        rt, wt, kt, vt, at, bt = inp
        Sa = jnp.einsum('bhvk,bhk->bhv', S, at)
        S = S * wt[:, :, None, :] + Sa[..., None] * bt[:, :, None, :] + vt[..., None] * kt[:, :, None, :]
        return S, jnp.einsum('bhvk,bhk->bhv', S, rt)

    xs = tuple(jnp.moveaxis(u, 1, 0) for u in (r, w, k, v, a, b))
    S, ys = lax.scan(step, S0, xs)
    return jnp.moveaxis(ys, 0, 1), S


def _layer(x, st, p):
    mC0, mn0, mm0, mconv0, sS0, sconv0, rS0, rsh0 = st
    Bsz, T, _ = x.shape
    f32 = jnp.float32
    h = _rmsnorm(x, p['norm_w'], NORM_EPS)
    w_in = p['w_in']

    def cols(off, size):
        return h @ w_in[:, off:off + size]

    g = jax.nn.sigmoid((cols(OFF_GATE, N_BRANCH * D_MODEL) + p['b_gate']).astype(f32))
    g = g.reshape(Bsz, T, N_BRANCH, D_MODEL)

    m_x = cols(OFF_M_X, D_M)
    m_c, mconv1 = _causal_conv(m_x, mconv0, p['m_conv_w'], p['m_conv_b'])
    m_c = jax.nn.silu(m_c)
    q = _blockdiag(m_c, p['m_wq']).reshape(Bsz, T, H_M, DH_M).astype(f32)
    k = (_blockdiag(m_c, p['m_wk']).reshape(Bsz, T, H_M, DH_M) * DH_M ** -0.5).astype(f32)
    v = _blockdiag(m_x, p['m_wv']).reshape(Bsz, T, H_M, DH_M).astype(f32)
    m_if = (cols(OFF_M_IF, 2 * H_M) + p['m_b_if']).astype(f32)
    hm, mC1, mn1, mm1 = _mlstm_chunked(q, k, v, m_if[..., :H_M], m_if[..., H_M:],
                                       mC0.astype(f32), mn0.astype(f32), mm0.astype(f32))
    hm = _head_layernorm(hm, 1e-5).reshape(Bsz, T, D_M) * p['m_norm_w']
    m_o = jax.nn.sigmoid(cols(OFF_M_O, D_M).astype(f32))
    hm = (m_o * hm + p['m_skip'] * m_c) * jax.nn.silu(cols(OFF_M_Z, D_M).astype(f32))
    y_m = hm.astype(x.dtype) @ p['m_w_out']

    xbc, sconv1 = _causal_conv(cols(OFF_S_XBC, D_CONV_S), sconv0, p['s_conv_w'], p['s_conv_b'])
    xbc = jax.nn.silu(xbc).astype(f32)
    xs_ = xbc[..., :D_S].reshape(Bsz, T, H_S, P_S)
    Bm = xbc[..., D_S:D_S + G_S * N_S].reshape(Bsz, T, G_S, N_S)
    Cm = xbc[..., D_S + G_S * N_S:].reshape(Bsz, T, G_S, N_S)
    dt = jax.nn.softplus((cols(OFF_S_DT, H_S) + p['s_dt_bias']).astype(f32))
    A = -jnp.exp(p['s_A_log'].astype(f32))
    ys, sS1 = _ssd_chunked(xs_, dt, A, Bm, Cm, sS0.astype(f32))
    ys = (ys + p['s_D'].astype(f32)[:, None] * xs_).reshape(Bsz, T, D_S)
    ys = ys * jax.nn.silu(cols(OFF_S_Z, D_S).astype(f32))
    ys = _group_rmsnorm(ys, G_S, 1e-5) * p['s_norm_w']
    y_s = ys.astype(x.dtype) @ p['s_w_out']

    r_sh = cols(OFF_R_SH, N_SHIFT)
    r_full = jnp.concatenate([rsh0.astype(r_sh.dtype), r_sh], axis=1)
    r_prev = r_full[:, :T]
    rsh1 = r_full[:, T:]
    rx = (r_sh + (r_prev - r_sh) * p['r_mu']).astype(f32)
    rr = rx[..., :D_R]
    kr = rx[..., D_R:2 * D_R]
    vr = rx[..., 2 * D_R:3 * D_R]
    w_lo = rx[..., 3 * D_R:3 * D_R + LORA_W]
    a_lo = rx[..., 3 * D_R + LORA_W:]
    w_log = -jax.nn.softplus(-(p['r_w0'] + jnp.tanh(w_lo) @ p['r_w2'])) - 0.5
    decay = jnp.exp(-jnp.exp(w_log))
    a = jax.nn.sigmoid(p['r_a0'] + a_lo @ p['r_a2'])

    def hs(u):
        return u.reshape(Bsz, T, H_R, HD_R)

    kk = hs(kr * p['r_k_k'])
    kk = kk / jnp.maximum(jnp.sqrt(jnp.sum(kk * kk, axis=-1, keepdims=True)), 1e-12)
    kmod = hs(kr * (1.0 + (a - 1.0) * p['r_k_a']))
    rh, vh = hs(rr), hs(vr)
    yr, rS1 = _rwkv7_scan(rh, hs(decay), kmod, vh, -kk, kk * hs(a), rS0.astype(f32))
    yr = _head_layernorm(yr, RWKV_GN_EPS).reshape(Bsz, T, D_R) * p['r_ln_w'] + p['r_ln_b']
    bonus = jnp.sum(rh * kmod * p['r_r_k'], axis=-1, keepdims=True) * vh
    yr = (yr + bonus.reshape(Bsz, T, D_R)) * jax.nn.silu(cols(OFF_R_Z, D_R).astype(f32))
    y_r = yr.astype(x.dtype) @ p['r_w_out']

    merged = g[:, :, 0] * y_m + g[:, :, 1] * y_s + g[:, :, 2] * y_r
    out = x + merged.astype(x.dtype) @ p['w_out']
    new = (mC1.astype(mC0.dtype), mn1.astype(mn0.dtype), mm1.astype(mm0.dtype), mconv1.astype(mconv0.dtype),
           sS1.astype(sS0.dtype), sconv1.astype(sconv0.dtype), rS1.astype(rS0.dtype), rsh1.astype(rsh0.dtype))
    return out, new


def _zero_states(batch, dtype):
    def z(*s):
        return jnp.zeros((DEPTH, batch) + s, dtype)
    return (z(H_M, DH_M, DH_M), z(H_M, DH_M), z(H_M), z(CONV_W - 1, D_M),
            z(H_S, P_S, N_S), z(CONV_W - 1, D_CONV_S), z(H_R, HD_R, HD_R), z(1, N_SHIFT))


def setup_inputs(seed: int = 0) -> dict:
    key = jax.random.key(seed)
    ks = iter(jax.random.split(key, 64))

    def nrm(shape, scale=1.0):
        return scale * jax.random.normal(next(ks), shape, jnp.float32)

    def uni(shape, lo, hi):
        return jax.random.uniform(next(ks), shape, jnp.float32, lo, hi)

    L = DEPTH
    dt0 = jnp.exp(uni((L, H_S), math.log(1e-3), math.log(1e-1)))
    f_bias = jnp.broadcast_to(jnp.linspace(3.0, 6.0, H_M), (L, H_M)) + nrm((L, H_M), 0.1)
    return {
        'x_prompt': nrm((BATCH, SEQ, D_MODEL)),
        'x_sample': nrm((DEC_BATCH, DEC_SEQ, D_MODEL)),
        'state_mlstm_C': nrm((L, DEC_BATCH, H_M, DH_M, DH_M), 0.05),
        'state_mlstm_n': nrm((L, DEC_BATCH, H_M, DH_M), 0.1),
        'state_mlstm_m': nrm((L, DEC_BATCH, H_M), 0.5),
        'state_mlstm_conv': nrm((L, DEC_BATCH, CONV_W - 1, D_M)),
        'state_ssd': nrm((L, DEC_BATCH, H_S, P_S, N_S), 0.1),
        'state_ssd_conv': nrm((L, DEC_BATCH, CONV_W - 1, D_CONV_S)),
        'state_rwkv': nrm((L, DEC_BATCH, H_R, HD_R, HD_R), 0.1),
        'state_rwkv_shift': nrm((L, DEC_BATCH, 1, N_SHIFT)),
        'norm_w': 1.0 + nrm((L, D_MODEL), 0.02),
        'w_in': nrm((L, D_MODEL, N_IN), D_MODEL ** -0.5),
        'b_gate': nrm((L, N_BRANCH * D_MODEL), 0.1),
        'm_conv_w': nrm((L, CONV_W, D_M), CONV_W ** -0.5),
        'm_conv_b': nrm((L, D_M), 0.02),
        'm_wq': nrm((L, NB_M, QKV_BLOCK, QKV_BLOCK), QKV_BLOCK ** -0.5),
        'm_wk': nrm((L, NB_M, QKV_BLOCK, QKV_BLOCK), QKV_BLOCK ** -0.5),
        'm_wv': nrm((L, NB_M, QKV_BLOCK, QKV_BLOCK), QKV_BLOCK ** -0.5),
        'm_b_if': jnp.concatenate([nrm((L, H_M), 0.1), f_bias], axis=-1),
        'm_norm_w': 1.0 + nrm((L, D_M), 0.02),
        'm_skip': 1.0 + nrm((L, D_M), 0.02),
        'm_w_out': nrm((L, D_M, D_MODEL), D_M ** -0.5),
        's_conv_w': nrm((L, CONV_W, D_CONV_S), CONV_W ** -0.5),
        's_conv_b': nrm((L, D_CONV_S), 0.02),
        's_dt_bias': dt0 + jnp.log(-jnp.expm1(-dt0)),
        's_A_log': jnp.log(uni((L, H_S), 1.0, 16.0)),
        's_D': 1.0 + nrm((L, H_S), 0.1),
        's_norm_w': 1.0 + nrm((L, D_S), 0.02),
        's_w_out': nrm((L, D_S, D_MODEL), D_S ** -0.5),
        'r_mu': uni((L, N_SHIFT), 0.0, 1.0),
        'r_w0': uni((L, D_R), -6.0, -1.0),
        'r_w2': nrm((L, LORA_W, D_R), 0.1),
        'r_a0': nrm((L, D_R), 0.1),
        'r_a2': nrm((L, LORA_A, D_R), 0.1),
        'r_k_k': 0.85 + nrm((L, D_R), 0.02),
        'r_k_a': 1.0 + nrm((L, D_R), 0.02),
        'r_r_k': nrm((L, H_R, HD_R), 0.1),
        'r_ln_w': 1.0 + nrm((L, D_R), 0.02),
        'r_ln_b': nrm((L, D_R), 0.02),
        'r_w_out': nrm((L, D_R, D_MODEL), D_R ** -0.5),
        'w_out': nrm((L, D_MODEL, D_MODEL), D_MODEL ** -0.5),
        'final_norm_w': 1.0 + nrm((D_MODEL,), 0.02),
    }


def reference(x_prompt, x_sample, state_mlstm_C, state_mlstm_n, state_mlstm_m, state_mlstm_conv,
              state_ssd, state_ssd_conv, state_rwkv, state_rwkv_shift,
              norm_w, w_in, b_gate, m_conv_w, m_conv_b, m_wq, m_wk, m_wv, m_b_if, m_norm_w, m_skip, m_w_out,
              s_conv_w, s_conv_b, s_dt_bias, s_A_log, s_D, s_norm_w, s_w_out,
              r_mu, r_w0, r_w2, r_a0, r_a2, r_k_k, r_k_a, r_r_k, r_ln_w, r_ln_b, r_w_out,
              w_out, final_norm_w):
    layer_params = [dict(norm_w=norm_w[l], w_in=w_in[l], b_gate=b_gate[l],
                         m_conv_w=m_conv_w[l], m_conv_b=m_conv_b[l], m_wq=m_wq[l], m_wk=m_wk[l], m_wv=m_wv[l],
                         m_b_if=m_b_if[l], m_norm_w=m_norm_w[l], m_skip=m_skip[l], m_w_out=m_w_out[l],
                         s_conv_w=s_conv_w[l], s_conv_b=s_conv_b[l], s_dt_bias=s_dt_bias[l], s_A_log=s_A_log[l],
                         s_D=s_D[l], s_norm_w=s_norm_w[l], s_w_out=s_w_out[l],
                         r_mu=r_mu[l], r_w0=r_w0[l], r_w2=r_w2[l], r_a0=r_a0[l], r_a2=r_a2[l],
                         r_k_k=r_k_k[l], r_k_a=r_k_a[l], r_r_k=r_r_k[l], r_ln_w=r_ln_w[l], r_ln_b=r_ln_b[l],
                         r_w_out=r_w_out[l], w_out=w_out[l])
                    for l in range(DEPTH)]

    def run(x, states):
        collected = [[] for _ in states]
        for l in range(DEPTH):
            x, new = _layer(x, tuple(s[l] for s in states), layer_params[l])
            for c, s in zip(collected, new):
                c.append(s)
        return _rmsnorm(x, final_norm_w, NORM_EPS), [jnp.stack(c) for c in collected]

    y_prompt, (pC, pn, pm, pmc, ps, psc, pr, prs) = run(
        x_prompt, _zero_states(x_prompt.shape[0], x_prompt.dtype))
    y_sample, (sC, sn, sm, smc, ss, ssc, sr, srs) = run(
        x_sample, (state_mlstm_C, state_mlstm_n, state_mlstm_m, state_mlstm_conv,
                   state_ssd, state_ssd_conv, state_rwkv, state_rwkv_shift))
    return (y_prompt, y_sample, pC, pn, pm, pmc, ps, psc, pr, prs, sC, sn, sm, smc, ss, ssc, sr, srs)
```

```python
import functools

import jax
import jax.numpy as jnp
from jax import lax
from jax.experimental import pallas as pl
from jax.experimental.pallas import tpu as pltpu

F32 = jnp.float32
BF16 = jnp.bfloat16
HI = lax.Precision.HIGHEST

CHUNK = 64
NORM_EPS = 1e-6
H_M = 4
MLSTM_LN_EPS = 1e-5
P_S = 64
G_S = 4
N_S = 128
SSD_GN_EPS = 1e-5
HD_R = 64
LORA_W = 64
LORA_A = 64
RWKV_GN_EPS = 64e-5
RWKV_PACK = 4
SOLVE_BASE = 16

LANE = 128
SUBLANE = 8
VMEM_LIMIT_BYTES = 48 * 1024 * 1024

E_LO = 0 * LANE
E_IF = 1 * LANE
E_DT = 2 * LANE
E_W = 3 * LANE


def _nn(a, b, prec=None):
    return jnp.dot(a, b, precision=prec, preferred_element_type=F32)


def _nt(a, b, prec=None):
    return lax.dot_general(a, b, (((1,), (1,)), ((), ())), precision=prec, preferred_element_type=F32)


def _tn(a, b, prec=None):
    return lax.dot_general(a, b, (((0,), (0,)), ((), ())), precision=prec, preferred_element_type=F32)


def _b(x):
    return x.astype(BF16)


def _softplus(x):
    return jnp.maximum(x, 0.0) + jnp.log1p(jnp.exp(-jnp.abs(x)))


def _log_sigmoid(x):
    return -_softplus(-x)


def _silu(x):
    return x * jax.nn.sigmoid(x)


def _iota2(shape, dim):
    return lax.broadcasted_iota(jnp.int32, shape, dim)


def _tri_consts(L):
    r, c = _iota2((L, L), 0), _iota2((L, L), 1)
    causal = c <= r
    return causal, causal.astype(F32), (r <= c).astype(F32)


def _eye(n):
    return (_iota2((n, n), 0) == _iota2((n, n), 1)).astype(F32)


def _causal_conv(xp_ref, x, cw_ref, cb_ref, L, KW):
    base = SUBLANE - (KW - 1)
    xp_ref[SUBLANE:SUBLANE + L, :] = x
    y = cb_ref[...] + xp_ref[base:base + L, :] * cw_ref[0:1, :]
    for j in range(1, KW):
        y = y + xp_ref[base + j:base + j + L, :] * cw_ref[j:j + 1, :]
    xp_ref[base:SUBLANE, :] = x[L - (KW - 1):L, :]
    return y


def _norm_kernel(x_ref, w_ref, o_ref):
    x = x_ref[...]
    y = x * lax.rsqrt(jnp.mean(x * x, axis=-1, keepdims=True) + NORM_EPS)
    o_ref[...] = (y * w_ref[...]).astype(o_ref.dtype)


def _rmsnorm(x, w, out_dtype, tm):
    n, d = x.shape
    return pl.pallas_call(
        _norm_kernel,
        grid=(n // tm,),
        in_specs=[pl.BlockSpec((tm, d), lambda i: (i, 0)), pl.BlockSpec((1, d), lambda i: (0, 0))],
        out_specs=pl.BlockSpec((tm, d), lambda i: (i, 0)),
        out_shape=jax.ShapeDtypeStruct((n, d), out_dtype),
        compiler_params=pltpu.CompilerParams(dimension_semantics=("parallel",), vmem_limit_bytes=VMEM_LIMIT_BYTES),
        name="rmsnorm",
    )(x, w)


def _matmul_kernel(h_ref, w_ref, o_ref):
    o_ref[...] = _nn(h_ref[...], w_ref[...])


def _matmul(h, w, tm, tn):
    n, d = h.shape
    nw = w.shape[1]
    return pl.pallas_call(
        _matmul_kernel,
        grid=(nw // tn, n // tm),
        in_specs=[pl.BlockSpec((tm, d), lambda j, i: (i, 0)), pl.BlockSpec((d, tn), lambda j, i: (0, j))],
        out_specs=pl.BlockSpec((tm, tn), lambda j, i: (i, j)),
        out_shape=jax.ShapeDtypeStruct((n, nw), F32),
        compiler_params=pltpu.CompilerParams(dimension_semantics=("parallel", "parallel"),
                                             vmem_limit_bytes=VMEM_LIMIT_BYTES),
        name="proj_in",
    )(h, w)


def _out_kernel(hm_ref, ys_ref, yr_ref, g_ref, x_ref, wm_ref, ws_ref, wr_ref, wo_ref, bg_ref, nw_ref,
                xo_ref, ho_ref, *, D):
    g = jax.nn.sigmoid(g_ref[...] + bg_ref[...])
    merged = (g[:, 0:D] * _nn(hm_ref[...], wm_ref[...])
              + g[:, D:2 * D] * _nn(ys_ref[...], ws_ref[...])
              + g[:, 2 * D:3 * D] * _nn(yr_ref[...], wr_ref[...]))
    out = x_ref[...] + _nn(_b(merged), wo_ref[...])
    xo_ref[...] = out
    y = out * lax.rsqrt(jnp.mean(out * out, axis=-1, keepdims=True) + NORM_EPS)
    ho_ref[...] = (y * nw_ref[...]).astype(ho_ref.dtype)


def _out_proj(hm, ys, yr, gate, x, wm, ws, wr, wo, bg, nw, h_dtype, tm):
    n, d = x.shape
    row = lambda w: pl.BlockSpec((tm, w), lambda i: (i, 0))
    full = lambda a: pl.BlockSpec(a.shape, lambda i: (0, 0))
    return pl.pallas_call(
        functools.partial(_out_kernel, D=d),
        grid=(n // tm,),
        in_specs=[row(d), row(d), row(d), row(3 * d), row(d),
                  full(wm), full(ws), full(wr), full(wo), full(bg), full(nw)],
        out_specs=[row(d), row(d)],
        out_shape=[jax.ShapeDtypeStruct((n, d), F32), jax.ShapeDtypeStruct((n, d), h_dtype)],
        compiler_params=pltpu.CompilerParams(dimension_semantics=("parallel",), vmem_limit_bytes=VMEM_LIMIT_BYTES),
        name="out_proj",
    )(hm, ys, yr, gate, x, wm, ws, wr, wo, bg, nw)


def _blockdiag_apply(a, coef_ref, QB):
    parts = []
    for g in range(a.shape[1] // LANE):
        sl = slice(g * LANE, (g + 1) * LANE)
        ag = a[:, sl]
        acc = ag * coef_ref[QB - 1:QB, sl]
        for s in range(-(QB - 1), QB):
            if s != 0:
                acc = acc + pltpu.roll(ag, s % LANE, 1) * coef_ref[s + QB - 1:s + QB, sl]
        parts.append(acc)
    return jnp.concatenate(parts, axis=1)


def _mlstm_kernel(u_ref, e_ref, conv0_ref, C0_ref, n0_ref, m0_ref,
                  cw_ref, cb_ref, wq_ref, wk_ref, wv_ref, bif_ref, nw_ref, skip_ref,
                  h_ref, C1_ref, n1_ref, m1_ref,
                  xp_ref, C_ref, n_ref, m_ref, *, L, H, DH, KW, QB):
    c = pl.program_id(1)
    D = H * DH

    @pl.when(c == 0)
    def _():
        xp_ref[SUBLANE - (KW - 1):SUBLANE, :] = conv0_ref[0]
        C_ref[...] = C0_ref[0]
        n_ref[...] = n0_ref[0]
        m_ref[...] = m0_ref[0]

    u = u_ref[0]
    mx, mz, mo = u[:, 0:D], u[:, D:2 * D], u[:, 2 * D:3 * D]
    mc = _silu(_causal_conv(xp_ref, mx, cw_ref, cb_ref, L, KW))
    q = _blockdiag_apply(mc, wq_ref, QB)
    k = _blockdiag_apply(mc, wk_ref, QB)
    v = _blockdiag_apply(mx, wv_ref, QB)

    causal, tril, triu = _tri_consts(L)
    ecol = e_ref[0][:, E_IF:E_IF + LANE] + bif_ref[...]
    eT = _nt(_eye(LANE), ecol, HI)[0:SUBLANE, :]
    b_col = _nn(tril, _log_sigmoid(ecol), HI)
    b_row = _nn(_log_sigmoid(eT), triu, HI)

    outs = []
    for h in range(H):
        sl = slice(h * DH, (h + 1) * DH)
        qf, kf = q[:, sl], k[:, sl]
        qh, vh = _b(qf), _b(v[:, sl])
        bc, br = b_col[:, H + h:H + h + 1], b_row[H + h:H + h + 1, :]
        ic, ir = ecol[:, h:h + 1], eT[h:h + 1, :]
        mprev = m_ref[h:h + 1, 0:1]
        Ch = C_ref[h]
        nh = n_ref[h:h + 1, :]

        logD = jnp.where(causal, bc - br + ir, -jnp.inf)
        inter = bc + mprev
        m_t = jnp.maximum(inter, jnp.max(logD, axis=1, keepdims=True))
        Dw = jnp.exp(logD - m_t)
        sc = jnp.exp(inter - m_t)
        s = _nt(qh, _b(kf)) * Dw
        num = _nn(_b(s), vh) + _nn(qh, _b(Ch)) * sc
        den = jnp.sum(s, axis=1, keepdims=True) + sc * jnp.sum(qf * nh, axis=1, keepdims=True)
        hden = jnp.maximum(jnp.abs(den), jnp.exp(-m_t))
        hc = num / hden

        bL = bc[L - 1:L, :]
        wr = bL - br + ir
        m_new = jnp.maximum(bL + mprev, jnp.max(wr, axis=1, keepdims=True))
        dec = jnp.exp(bL + mprev - m_new)
        kw = kf * jnp.exp(bL - bc + ic - m_new)
        C_ref[h] = dec * Ch + _tn(_b(kw), vh)
        n_ref[h:h + 1, :] = dec * nh + jnp.sum(kw, axis=0, keepdims=True)
        m_ref[h:h + 1, :] = jnp.broadcast_to(m_new, (1, LANE))

        mu = jnp.mean(hc, axis=1, keepdims=True)
        hz = hc - mu
        var = jnp.mean(hz * hz, axis=1, keepdims=True)
        outs.append(hz * lax.rsqrt(var + MLSTM_LN_EPS))

    hm = jnp.concatenate(outs, axis=1) * nw_ref[...]
    out = (jax.nn.sigmoid(mo) * hm + skip_ref[...] * mc) * _silu(mz)
    h_ref[0] = out.astype(h_ref.dtype)

    @pl.when(c == pl.num_programs(1) - 1)
    def _():
        C1_ref[0] = C_ref[...]
        n1_ref[0] = n_ref[...]
        m1_ref[0] = m_ref[...]


def _mlstm(u, e, conv0, C0, n0, m0, p, L):
    B, T, W = u.shape
    H, DH = C0.shape[1], C0.shape[2]
    D = H * DH
    KW = p["m_cw"].shape[0]
    QB = (p["m_wq"].shape[0] + 1) // 2
    m0b = jnp.broadcast_to(m0[..., None], (B, H, LANE))
    params = [p["m_cw"], p["m_cb"], p["m_wq"], p["m_wk"], p["m_wv"], p["m_bif"], p["m_nw"], p["m_skip"]]
    per_b = lambda a: pl.BlockSpec((1,) + a.shape[1:], lambda b, c: (b,) + (0,) * (a.ndim - 1))
    full = lambda a: pl.BlockSpec(a.shape, lambda b, c: (0,) * a.ndim)
    h, C1, n1, m1 = pl.pallas_call(
        functools.partial(_mlstm_kernel, L=L, H=H, DH=DH, KW=KW, QB=QB),
        grid=(B, T // L),
        in_specs=[pl.BlockSpec((1, L, W), lambda b, c: (b, c, 0)),
                  pl.BlockSpec((1, L, E_W), lambda b, c: (b, c, 0)),
                  per_b(conv0), per_b(C0), per_b(n0), per_b(m0b)] + [full(a) for a in params],
        out_specs=[pl.BlockSpec((1, L, D), lambda b, c: (b, c, 0)), per_b(C0), per_b(n0), per_b(m0b)],
        out_shape=[jax.ShapeDtypeStruct((B, T, D), BF16), jax.ShapeDtypeStruct(C0.shape, F32),
                   jax.ShapeDtypeStruct(n0.shape, F32), jax.ShapeDtypeStruct(m0b.shape, F32)],
        scratch_shapes=[pltpu.VMEM((L + SUBLANE, D), F32), pltpu.VMEM((H, DH, DH), F32),
                        pltpu.VMEM((H, DH), F32), pltpu.VMEM((H, LANE), F32)],
        compiler_params=pltpu.CompilerParams(dimension_semantics=("parallel", "arbitrary"),
                                             vmem_limit_bytes=VMEM_LIMIT_BYTES),
        name="mlstm",
    )(u, e, conv0, C0, n0, m0b, *params)
    return h, C1, n1, m1[..., 0]


def _ssd_kernel(u_ref, e_ref, conv0_ref, S0_ref, cw_ref, cb_ref, dtb_ref, alog_ref, drow_ref, nw_ref,
                y_ref, S1_ref, xp_ref, S_ref, *, L, H, P, G, N, KW):
    c = pl.program_id(1)
    D = H * P
    E = H // G

    @pl.when(c == 0)
    def _():
        xp_ref[SUBLANE - (KW - 1):SUBLANE, :] = conv0_ref[0]
        S_ref[...] = S0_ref[0]

    u = u_ref[0]
    sz = u[:, 0:D]
    xbc = _silu(_causal_conv(xp_ref, u[:, D:], cw_ref, cb_ref, L, KW))
    xs, Bm, Cm = xbc[:, 0:D], xbc[:, D:D + G * N], xbc[:, D + G * N:]

    causal, tril, triu = _tri_consts(L)
    dt = _softplus(e_ref[0][:, E_DT:E_DT + LANE] + dtb_ref[...])
    dtA = dt * (-jnp.exp(alog_ref[...]))
    eye = _eye(LANE)
    dtT = _nt(eye, dt, HI)[0:H, :]
    cs_col = _nn(tril, dtA, HI)
    cs_row = _nn(_nt(eye, dtA, HI)[0:H, :], triu, HI)

    ys = []
    for g in range(G):
        Bg, Cg = _b(Bm[:, g * N:(g + 1) * N]), _b(Cm[:, g * N:(g + 1) * N])
        CB = _nt(Cg, Bg)
        for e in range(E):
            h = g * E + e
            cc, cr = cs_col[:, h:h + 1], cs_row[h:h + 1, :]
            dc, dr = dt[:, h:h + 1], dtT[h:h + 1, :]
            xh = xs[:, h * P:(h + 1) * P]
            Sh = S_ref[h]
            seg = jnp.exp(jnp.where(causal, cc - cr, -jnp.inf))
            ys.append(_nn(_b(CB * seg * dr), _b(xh)) + _nt(Cg, _b(Sh)) * jnp.exp(cc))
            csL = cc[L - 1:L, :]
            xw = xh * (jnp.exp(csL - cc) * dc)
            S_ref[h] = jnp.exp(csL) * Sh + _tn(_b(xw), Bg)

    yv = (jnp.concatenate(ys, axis=1) + drow_ref[...] * xs) * _silu(sz)
    DG = D // G
    parts = []
    for g in range(G):
        seg = yv[:, g * DG:(g + 1) * DG]
        parts.append(seg * lax.rsqrt(jnp.mean(seg * seg, axis=1, keepdims=True) + SSD_GN_EPS))
    y_ref[0] = (jnp.concatenate(parts, axis=1) * nw_ref[...]).astype(y_ref.dtype)

    @pl.when(c == pl.num_programs(1) - 1)
    def _():
        S1_ref[0] = S_ref[...]


def _ssd(u, e, conv0, S0, p, L):
    B, T, W = u.shape
    H, P, N = S0.shape[1:]
    D = H * P
    DC = conv0.shape[-1]
    G = (DC - D) // (2 * N)
    KW = p["s_cw"].shape[0]
    params = [p["s_cw"], p["s_cb"], p["s_dtb"], p["s_alog"], p["s_drow"], p["s_nw"]]
    per_b = lambda a: pl.BlockSpec((1,) + a.shape[1:], lambda b, c: (b,) + (0,) * (a.ndim - 1))
    full = lambda a: pl.BlockSpec(a.shape, lambda b, c: (0,) * a.ndim)
    return pl.pallas_call(
        functools.partial(_ssd_kernel, L=L, H=H, P=P, G=G, N=N, KW=KW),
        grid=(B, T // L),
        in_specs=[pl.BlockSpec((1, L, W), lambda b, c: (b, c, 0)),
                  pl.BlockSpec((1, L, E_W), lambda b, c: (b, c, 0)),
                  per_b(conv0), per_b(S0)] + [full(a) for a in params],
        out_specs=[pl.BlockSpec((1, L, D), lambda b, c: (b, c, 0)), per_b(S0)],
        out_shape=[jax.ShapeDtypeStruct((B, T, D), BF16), jax.ShapeDtypeStruct(S0.shape, F32)],
        scratch_shapes=[pltpu.VMEM((L + SUBLANE, DC), F32), pltpu.VMEM((H, P, N), F32)],
        compiler_params=pltpu.CompilerParams(dimension_semantics=("parallel", "arbitrary"),
                                             vmem_limit_bytes=VMEM_LIMIT_BYTES),
        name="ssd",
    )(u, e, conv0, S0, *params)


def _stack_masked(x, blk):
    n = x.shape[1] // blk
    cb = _iota2(x.shape, 1) // blk
    return jnp.concatenate([jnp.where(cb == h, x, 0.0) for h in range(n)], axis=0)


def _unit_lower_solve(A, rhs, L, HD):
    base = min(SOLVE_BASE, L)
    t, i = _iota2(A.shape, 0), _iota2(A.shape, 1) % L
    same = (t // base) == (i // base)
    mmc = lambda x, y: _nn(x, _stack_masked(y, L), HI)
    mmv = lambda x, y: _nn(x, _stack_masked(y, HD), HI)
    Ad = jnp.where(same, A, 0.0)
    Dinv = (t == i).astype(F32) + Ad
    Pw = Ad
    n = 2
    while n < base:
        Pw = mmc(Pw, Pw)
        Dinv = Dinv + mmc(Pw, Dinv)
        n *= 2
    X = mmv(Dinv, rhs)
    nb = L // base
    if nb > 1:
        M = mmc(Dinv, jnp.where(same, 0.0, A))
        n = 1
        while n < nb:
            X = X + mmv(M, X)
            n *= 2
            if n < nb:
                M = mmc(M, M)
    return X


def _rwkv_kernel(u_ref, e_ref, sha_ref, shb_ref, S0_ref,
                 mua_ref, mub_ref, w0_ref, w2_ref, a0_ref, a2_ref, kk_ref, ka_ref, rk_ref, lnw_ref, lnb_ref,
                 y_ref, S1_ref, xpa_ref, xpb_ref, S_ref, *, L, H, HD):
    c = pl.program_id(1)
    D = H * HD
    GW = RWKV_PACK * HD
    NG = H // RWKV_PACK

    @pl.when(c == 0)
    def _():
        xpa_ref[SUBLANE - 1:SUBLANE, :] = sha_ref[0]
        xpb_ref[SUBLANE - 1:SUBLANE, :] = shb_ref[0]
        S_ref[...] = jnp.zeros(S_ref.shape, F32)
        for h in range(H):
            g, j = divmod(h, RWKV_PACK)
            S_ref[g, j * HD:(j + 1) * HD, j * HD:(j + 1) * HD] = S0_ref[0, h]

    def shift_mix(xp_ref, cur, mu_ref):
        xp_ref[SUBLANE:SUBLANE + L, :] = cur
        prev = xp_ref[SUBLANE - 1:SUBLANE - 1 + L, :]
        xp_ref[SUBLANE - 1:SUBLANE, :] = cur[L - 1:L, :]
        return cur + (prev - cur) * mu_ref[...]

    u = u_ref[0]
    rx = shift_mix(xpa_ref, u[:, 0:3 * D], mua_ref)
    lo = shift_mix(xpb_ref, e_ref[0][:, E_LO:E_LO + LANE], mub_ref)
    rz = u[:, 3 * D:4 * D]
    rr, kr, vr = rx[:, 0:D], rx[:, D:2 * D], rx[:, 2 * D:3 * D]

    seg = ((_iota2((GW, GW), 0) // HD) == (_iota2((GW, GW), 1) // HD))
    segf = seg.astype(F32)

    def head_sum(x):
        return jnp.concatenate([_nn(x[:, g * GW:(g + 1) * GW], segf, HI) for g in range(NG)], axis=1)

    w_log = -_softplus(-(w0_ref[...] + _nn(jnp.tanh(lo), w2_ref[...], HI))) - 0.5
    lw = -jnp.exp(w_log)
    a_sig = jax.nn.sigmoid(a0_ref[...] + _nn(lo, a2_ref[...], HI))
    kk = kr * kk_ref[...]
    kk = kk / jnp.maximum(jnp.sqrt(head_sum(kk * kk)), 1e-12)
    kmod = kr * (1.0 + (a_sig - 1.0) * ka_ref[...])
    kb = kk * a_sig

    _, tril, _ = _tri_consts(L)
    cum = _nn(tril, lw, HI)
    cumL = cum[L - 1:L, :]
    p_in, p_inv, p_end = jnp.exp(cum), jnp.exp(-cum), jnp.exp(cumL - cum)
    At = -kk * jnp.exp(cum - lw)
    Bt, Kt, Rt = kb * p_inv, kmod * p_inv, rr * p_in
    Bd, Kd = kb * p_end, kmod * p_end
    PL = jnp.exp(cumL)

    t, i = _iota2((L, RWKV_PACK * L), 0), _iota2((L, RWKV_PACK * L), 1) % L
    strict, incl = i < t, i <= t

    ys = []
    for g in range(NG):
        gs = slice(g * GW, (g + 1) * GW)
        Ag, Rg, Vg = At[:, gs], Rt[:, gs], vr[:, gs]
        Sbd = S_ref[g]
        Bst, Kst, Vst = _stack_masked(Bt[:, gs], HD), _stack_masked(Kt[:, gs], HD), _stack_masked(Vg, HD)
        Aab = jnp.where(strict, _nt(Ag, Bst, HI), 0.0)
        Aak = jnp.where(strict, _nt(Ag, Kst, HI), 0.0)
        U = _unit_lower_solve(Aab, _nt(Ag, Sbd, HI) + _nn(Aak, Vst, HI), L, HD)
        Rb = jnp.where(incl, _nt(Rg, Bst, HI), 0.0)
        Rk = jnp.where(incl, _nt(Rg, Kst, HI), 0.0)
        ys.append(_nt(Rg, Sbd, HI) + _nn(Rb, _stack_masked(U, HD), HI) + _nn(Rk, Vst, HI))
        S_ref[g] = Sbd * PL[:, gs] + jnp.where(seg, _tn(U, Bd[:, gs], HI) + _tn(Vg, Kd[:, gs], HI), 0.0)

    y = jnp.concatenate(ys, axis=1)
    mu = head_sum(y) * (1.0 / HD)
    yz = y - mu
    var = head_sum(yz * yz) * (1.0 / HD)
    yn = yz * lax.rsqrt(var + RWKV_GN_EPS) * lnw_ref[...] + lnb_ref[...]
    bonus = head_sum(rr * kmod * rk_ref[...]) * vr
    y_ref[0] = ((yn + bonus) * _silu(rz)).astype(y_ref.dtype)

    @pl.when(c == pl.num_programs(1) - 1)
    def _():
        for h in range(H):
            g, j = divmod(h, RWKV_PACK)
            S1_ref[0, h] = S_ref[g, j * HD:(j + 1) * HD, j * HD:(j + 1) * HD]


def _rwkv(u, e, sha, shb, S0, p, L):
    B, T, W = u.shape
    H, HD = S0.shape[1], S0.shape[2]
    D = H * HD
    GW = RWKV_PACK * HD
    params = [p["r_mua"], p["r_mub"], p["r_w0"], p["r_w2"], p["r_a0"], p["r_a2"], p["r_kk"], p["r_ka"],
              p["r_rk"], p["r_lnw"], p["r_lnb"]]
    per_b = lambda a: pl.BlockSpec((1,) + a.shape[1:], lambda b, c: (b,) + (0,) * (a.ndim - 1))
    full = lambda a: pl.BlockSpec(a.shape, lambda b, c: (0,) * a.ndim)
    return pl.pallas_call(
        functools.partial(_rwkv_kernel, L=L, H=H, HD=HD),
        grid=(B, T // L),
        in_specs=[pl.BlockSpec((1, L, W), lambda b, c: (b, c, 0)),
                  pl.BlockSpec((1, L, E_W), lambda b, c: (b, c, 0)),
                  per_b(sha), per_b(shb), per_b(S0)] + [full(a) for a in params],
        out_specs=[pl.BlockSpec((1, L, D), lambda b, c: (b, c, 0)), per_b(S0)],
        out_shape=[jax.ShapeDtypeStruct((B, T, D), BF16), jax.ShapeDtypeStruct(S0.shape, F32)],
        scratch_shapes=[pltpu.VMEM((L + SUBLANE, 3 * D), F32), pltpu.VMEM((L + SUBLANE, LANE), F32),
                        pltpu.VMEM((H // RWKV_PACK, GW, GW), F32)],
        compiler_params=pltpu.CompilerParams(dimension_semantics=("parallel", "arbitrary"),
                                             vmem_limit_bytes=VMEM_LIMIT_BYTES),
        name="rwkv7",
    )(u, e, sha, shb, S0, *params)


def _bd_coef(w):
    dep, nb, qb, _ = w.shape
    rows = []
    for s in range(-(qb - 1), qb):
        cols = [w[:, :, d - s, d] if 0 <= d - s < qb else jnp.zeros((dep, nb), w.dtype) for d in range(qb)]
        rows.append(jnp.stack(cols, axis=-1).reshape(dep, nb * qb))
    return jnp.stack(rows, axis=1)


def _pad_lanes(a, width=LANE):
    return jnp.pad(a, [(0, 0)] * (a.ndim - 1) + [(0, width - a.shape[-1])])


def _prep_params(D, w_in, b_gate, m_conv_w, m_conv_b, m_wq, m_wk, m_wv, m_b_if, m_norm_w, m_skip, m_w_out,
                 s_conv_w, s_conv_b, s_dt_bias, s_A_log, s_D, s_norm_w, s_w_out,
                 r_mu, r_w0, r_w2, r_a0, r_a2, r_k_k, r_k_a, r_r_k, r_ln_w, r_ln_b, r_w_out, w_out):
    H_S = D // P_S
    DC = D + 2 * G_S * N_S
    n_lo = LORA_W + LORA_A
    assert n_lo == LANE and 2 * H_M <= SUBLANE and H_S <= LANE
    off_gate = 0
    off_m = off_gate + 3 * D
    off_if = off_m + 3 * D
    off_sz = off_if + 2 * H_M
    off_dt = off_sz + D + DC
    off_rsh = off_dt + H_S
    off_rz = off_rsh + 3 * D + n_lo
    assert off_rz + D == w_in.shape[-1]
    col = lambda o, n: w_in[:, :, o:o + n]
    row = lambda a: a[:, None, :]
    DH = D // H_M
    zl = jnp.zeros_like(r_w2)
    return dict(
        wA=_b(col(off_gate, 3 * D)),
        wB=_b(col(off_m, 3 * D)),
        wC=_b(col(off_sz, D + DC)),
        wD=_b(jnp.concatenate([col(off_rsh, 3 * D), col(off_rz, D)], axis=-1)),
        wE=_b(jnp.concatenate([col(off_rsh + 3 * D, n_lo), _pad_lanes(col(off_if, 2 * H_M)),
                               _pad_lanes(col(off_dt, H_S))], axis=-1)),
        bg=row(b_gate),
        m_cw=m_conv_w, m_cb=row(m_conv_b),
        m_wq=_bd_coef(m_wq), m_wk=_bd_coef(m_wk) * DH ** -0.5, m_wv=_bd_coef(m_wv),
        m_bif=row(_pad_lanes(m_b_if)), m_nw=row(m_norm_w), m_skip=row(m_skip), m_wo=_b(m_w_out),
        s_cw=s_conv_w, s_cb=row(s_conv_b), s_dtb=row(_pad_lanes(s_dt_bias)), s_alog=row(_pad_lanes(s_A_log)),
        s_drow=row(jnp.repeat(s_D, P_S, axis=-1)), s_nw=row(s_norm_w), s_wo=_b(s_w_out),
        r_mua=row(r_mu[:, :3 * D]), r_mub=row(r_mu[:, 3 * D:]),
        r_w0=row(r_w0), r_w2=jnp.concatenate([r_w2, zl], axis=1),
        r_a0=row(r_a0), r_a2=jnp.concatenate([zl, r_a2], axis=1),
        r_kk=row(r_k_k), r_ka=row(r_k_a), r_rk=row(r_r_k.reshape(r_r_k.shape[0], -1)),
        r_lnw=row(r_ln_w), r_lnb=row(r_ln_b), r_wo=_b(r_w_out),
        wo=_b(w_out),
    )


def _row_tile(n, cap):
    t = min(n, cap)
    while n % t:
        t //= 2
    return t


def _run(x, states, prep, norm_w, final_norm_w):
    B, T, D = x.shape
    depth = norm_w.shape[0]
    mC, mn, mm, mconv, sS, sconv, rS, rsh = states
    L = CHUNK if T % CHUNK == 0 else T
    KW = mconv.shape[2] + 1
    assert T >= KW - 1 and L % SUBLANE == 0
    N = B * T
    tm = _row_tile(N, 1024)
    tmo = _row_tile(N, 256)
    x2 = x.reshape(N, D)
    h = _rmsnorm(x2, norm_w[0][None, :], BF16, tm)
    new = [[] for _ in range(8)]
    for l in range(depth):
        p = {k: v[l] for k, v in prep.items()}
        proj = lambda w: _matmul(h, w, tm, _row_tile(w.shape[1], 1024) if w.shape[1] % 1024 == 0 else w.shape[1])
        uA = proj(p["wA"])
        uB = proj(p["wB"]).reshape(B, T, -1)
        uC = proj(p["wC"]).reshape(B, T, -1)
        uD = proj(p["wD"]).reshape(B, T, -1)
        uE = proj(p["wE"]).reshape(B, T, -1)
        hm, mC1, mn1, mm1 = _mlstm(uB, uE, mconv[l], mC[l], mn[l], mm[l], p, L)
        ys, sS1 = _ssd(uC, uE, sconv[l], sS[l], p, L)
        yr, rS1 = _rwkv(uD, uE, rsh[l][..., :3 * D], rsh[l][..., 3 * D:], rS[l], p, L)
        last = l == depth - 1
        nw = (final_norm_w if last else norm_w[l + 1])[None, :]
        x2, h = _out_proj(hm.reshape(N, D), ys.reshape(N, D), yr.reshape(N, D), uA, x2,
                          p["m_wo"], p["s_wo"], p["r_wo"], p["wo"], p["bg"], nw, F32 if last else BF16, tmo)
        outs = (mC1, mn1, mm1, uB[:, T - (KW - 1):, 0:D], sS1, uC[:, T - (KW - 1):, D:],
                rS1, jnp.concatenate([uD[:, T - 1:, 0:3 * D], uE[:, T - 1:, E_LO:E_LO + LANE]], axis=-1))
        for acc, s in zip(new, outs):
            acc.append(s)
    return h.reshape(B, T, D), [jnp.stack(s) for s in new]


def kernel(x_prompt, x_sample, state_mlstm_C, state_mlstm_n, state_mlstm_m, state_mlstm_conv, state_ssd,
           state_ssd_conv, state_rwkv, state_rwkv_shift, norm_w, w_in, b_gate, m_conv_w, m_conv_b, m_wq, m_wk,
           m_wv, m_b_if, m_norm_w, m_skip, m_w_out, s_conv_w, s_conv_b, s_dt_bias, s_A_log, s_D, s_norm_w,
           s_w_out, r_mu, r_w0, r_w2, r_a0, r_a2, r_k_k, r_k_a, r_r_k, r_ln_w, r_ln_b, r_w_out, w_out,
           final_norm_w):
    D = x_prompt.shape[-1]
    prep = _prep_params(D, w_in, b_gate, m_conv_w, m_conv_b, m_wq, m_wk, m_wv, m_b_if, m_norm_w, m_skip, m_w_out,
                        s_conv_w, s_conv_b, s_dt_bias, s_A_log, s_D, s_norm_w, s_w_out,
                        r_mu, r_w0, r_w2, r_a0, r_a2, r_k_k, r_k_a, r_r_k, r_ln_w, r_ln_b, r_w_out, w_out)
    sample_states = (state_mlstm_C, state_mlstm_n, state_mlstm_m, state_mlstm_conv, state_ssd,
                     state_ssd_conv, state_rwkv, state_rwkv_shift)
    Bp = x_prompt.shape[0]
    zero_states = tuple(jnp.zeros(s.shape[:1] + (Bp,) + s.shape[2:], s.dtype) for s in sample_states)
    y_p, st_p = _run(x_prompt, zero_states, prep, norm_w, final_norm_w)
    y_s, st_s = _run(x_sample, sample_states, prep, norm_w, final_norm_w)
    return (y_p, y_s, *st_p, *st_s)
```

```python
import functools

import jax
import jax.numpy as jnp
from jax import lax
from jax.experimental import pallas as pl
from jax.experimental.pallas import tpu as pltpu

F32 = jnp.float32
BF16 = jnp.bfloat16
HI = lax.Precision.HIGHEST

CHUNK = 64
NORM_EPS = 1e-6
H_M = 4
MLSTM_LN_EPS = 1e-5
P_S = 64
G_S = 4
N_S = 128
SSD_GN_EPS = 1e-5
HD_R = 64
LORA_W = 64
LORA_A = 64
RWKV_GN_EPS = 64e-5
RWKV_PACK = 4
SOLVE_BASE = 16

LANE = 128
SUBLANE = 8
MXU_DIM = 256
VMEM_LIMIT_BYTES = 48 * 1024 * 1024

E_LO = 0 * LANE
E_IF = 1 * LANE
E_DT = 2 * LANE
E_W = 3 * LANE


def _nn(a, b, prec=None):
    return jnp.dot(a, b, precision=prec, preferred_element_type=F32)


def _nt(a, b, prec=None):
    return lax.dot_general(a, b, (((1,), (1,)), ((), ())), precision=prec, preferred_element_type=F32)


def _tn(a, b, prec=None):
    return lax.dot_general(a, b, (((0,), (0,)), ((), ())), precision=prec, preferred_element_type=F32)


def _b(x):
    return x.astype(BF16)


def _softplus(x):
    return jnp.maximum(x, 0.0) + jnp.log1p(jnp.exp(-jnp.abs(x)))


def _log_sigmoid(x):
    return -_softplus(-x)


def _silu(x):
    return x * jax.nn.sigmoid(x)


def _iota2(shape, dim):
    return lax.broadcasted_iota(jnp.int32, shape, dim)


def _tri_consts(L):
    r, c = _iota2((L, L), 0), _iota2((L, L), 1)
    causal = c <= r
    return causal, causal.astype(F32), (r <= c).astype(F32)


def _eye(n):
    return (_iota2((n, n), 0) == _iota2((n, n), 1)).astype(F32)


def _causal_conv(xp_ref, x, cw_ref, cb_ref, L, KW):
    base = SUBLANE - (KW - 1)
    xp_ref[SUBLANE:SUBLANE + L, :] = x
    y = cb_ref[...] + xp_ref[base:base + L, :] * cw_ref[0:1, :]
    for j in range(1, KW):
        y = y + xp_ref[base + j:base + j + L, :] * cw_ref[j:j + 1, :]
    xp_ref[base:SUBLANE, :] = x[L - (KW - 1):L, :]
    return y


def _norm_kernel(x_ref, w_ref, o_ref):
    x = x_ref[...]
    y = x * lax.rsqrt(jnp.mean(x * x, axis=-1, keepdims=True) + NORM_EPS)
    o_ref[...] = (y * w_ref[...]).astype(o_ref.dtype)


def _rmsnorm(x, w, out_dtype, tm):
    n, d = x.shape
    return pl.pallas_call(
        _norm_kernel,
        grid=(n // tm,),
        in_specs=[pl.BlockSpec((tm, d), lambda i: (i, 0)), pl.BlockSpec((1, d), lambda i: (0, 0))],
        out_specs=pl.BlockSpec((tm, d), lambda i: (i, 0)),
        out_shape=jax.ShapeDtypeStruct((n, d), out_dtype),
        compiler_params=pltpu.CompilerParams(dimension_semantics=("parallel",), vmem_limit_bytes=VMEM_LIMIT_BYTES),
        name="rmsnorm",
    )(x, w)


def _matmul_kernel(h_ref, w_ref, o_ref):
    o_ref[...] = _nn(h_ref[...], w_ref[...])


def _matmul(h, w, tm, tn):
    n, d = h.shape
    nw = w.shape[1]
    return pl.pallas_call(
        _matmul_kernel,
        grid=(nw // tn, n // tm),
        in_specs=[pl.BlockSpec((tm, d), lambda j, i: (i, 0)), pl.BlockSpec((d, tn), lambda j, i: (0, j))],
        out_specs=pl.BlockSpec((tm, tn), lambda j, i: (i, j)),
        out_shape=jax.ShapeDtypeStruct((n, nw), F32),
        compiler_params=pltpu.CompilerParams(dimension_semantics=("parallel", "parallel"),
                                             vmem_limit_bytes=VMEM_LIMIT_BYTES),
        name="proj_in",
    )(h, w)


def _out_kernel(hm_ref, ys_ref, yr_ref, g_ref, x_ref, wm_ref, ws_ref, wr_ref, wo_ref, bg_ref, nw_ref,
                xo_ref, ho_ref, *, D):
    g = jax.nn.sigmoid(g_ref[...] + bg_ref[...])
    merged = (g[:, 0:D] * _nn(hm_ref[...], wm_ref[...])
              + g[:, D:2 * D] * _nn(ys_ref[...], ws_ref[...])
              + g[:, 2 * D:3 * D] * _nn(yr_ref[...], wr_ref[...]))
    out = x_ref[...] + _nn(_b(merged), wo_ref[...])
    xo_ref[...] = out
    y = out * lax.rsqrt(jnp.mean(out * out, axis=-1, keepdims=True) + NORM_EPS)
    ho_ref[...] = (y * nw_ref[...]).astype(ho_ref.dtype)


def _out_proj(hm, ys, yr, gate, x, wm, ws, wr, wo, bg, nw, h_dtype, tm):
    n, d = x.shape
    row = lambda w: pl.BlockSpec((tm, w), lambda i: (i, 0))
    full = lambda a: pl.BlockSpec(a.shape, lambda i: (0, 0))
    return pl.pallas_call(
        functools.partial(_out_kernel, D=d),
        grid=(n // tm,),
        in_specs=[row(d), row(d), row(d), row(3 * d), row(d),
                  full(wm), full(ws), full(wr), full(wo), full(bg), full(nw)],
        out_specs=[row(d), row(d)],
        out_shape=[jax.ShapeDtypeStruct((n, d), F32), jax.ShapeDtypeStruct((n, d), h_dtype)],
        compiler_params=pltpu.CompilerParams(dimension_semantics=("parallel",), vmem_limit_bytes=VMEM_LIMIT_BYTES),
        name="out_proj",
    )(hm, ys, yr, gate, x, wm, ws, wr, wo, bg, nw)


def _blockdiag_apply(a, w_ref):
    ab = _b(a)
    return [_nn(ab[:, t * MXU_DIM:(t + 1) * MXU_DIM], w_ref[t]) for t in range(w_ref.shape[0])]


def _mlstm_kernel(u_ref, e_ref, conv0_ref, C0_ref, n0_ref, m0_ref,
                  cw_ref, cb_ref, wq_ref, wk_ref, wv_ref, bif_ref, nw_ref, skip_ref,
                  h_ref, C1_ref, n1_ref, m1_ref,
                  xp_ref, C_ref, n_ref, m_ref, *, L, H, DH, KW):
    c = pl.program_id(1)
    D = H * DH

    @pl.when(c == 0)
    def _():
        xp_ref[SUBLANE - (KW - 1):SUBLANE, :] = conv0_ref[0]
        C_ref[...] = C0_ref[0]
        n_ref[...] = n0_ref[0]
        m_ref[...] = m0_ref[0]

    u = u_ref[0]
    mx, mz, mo = u[:, 0:D], u[:, D:2 * D], u[:, 2 * D:3 * D]
    mc = _silu(_causal_conv(xp_ref, mx, cw_ref, cb_ref, L, KW))
    q = _cat1(*_blockdiag_apply(mc, wq_ref))
    k = _cat1(*_blockdiag_apply(mc, wk_ref))
    v = _cat1(*_blockdiag_apply(mx, wv_ref))

    causal, tril, triu = _tri_consts(L)
    ecol = e_ref[0][:, E_IF:E_IF + LANE] + bif_ref[...]
    eT = _nt(_eye(LANE), ecol, HI)[0:SUBLANE, :]
    b_col = _nn(tril, _log_sigmoid(ecol), HI)
    b_row = _nn(_log_sigmoid(eT), triu, HI)

    Hs = range(H)
    sl = [slice(h * DH, (h + 1) * DH) for h in Hs]
    qf, kf = [q[:, s] for s in sl], [k[:, s] for s in sl]
    qh, kh, vh = [_b(x) for x in qf], [_b(x) for x in kf], [_b(v[:, s]) for s in sl]
    bc, br = [b_col[:, H + h:H + h + 1] for h in Hs], [b_row[H + h:H + h + 1, :] for h in Hs]
    ic, ir = [ecol[:, h:h + 1] for h in Hs], [eT[h:h + 1, :] for h in Hs]
    mprev = [m_ref[h:h + 1, 0:1] for h in Hs]
    Ch, nh = [C_ref[h] for h in Hs], [n_ref[h:h + 1, :] for h in Hs]

    qk = [_nt(qh[h], kh[h]) for h in Hs]
    qC = [_nn(qh[h], _b(Ch[h])) for h in Hs]
    logD = [jnp.where(causal, bc[h] - br[h] + ir[h], -jnp.inf) for h in Hs]
    inter = [bc[h] + mprev[h] for h in Hs]
    m_t = [jnp.maximum(inter[h], jnp.max(logD[h], axis=1, keepdims=True)) for h in Hs]
    sc = [jnp.exp(inter[h] - m_t[h]) for h in Hs]
    s = [qk[h] * jnp.exp(logD[h] - m_t[h]) for h in Hs]
    sv = [_nn(_b(s[h]), vh[h]) for h in Hs]

    bL = [bc[h][L - 1:L, :] for h in Hs]
    m_new = [jnp.maximum(bL[h] + mprev[h], jnp.max(bL[h] - br[h] + ir[h], axis=1, keepdims=True)) for h in Hs]
    dec = [jnp.exp(bL[h] + mprev[h] - m_new[h]) for h in Hs]
    kw = [kf[h] * jnp.exp(bL[h] - bc[h] + ic[h] - m_new[h]) for h in Hs]
    upd = [_tn(_b(kw[h]), vh[h]) for h in Hs]

    outs = []
    for h in Hs:
        den = jnp.sum(s[h], axis=1, keepdims=True) + sc[h] * jnp.sum(qf[h] * nh[h], axis=1, keepdims=True)
        hc = (sv[h] + qC[h] * sc[h]) / jnp.maximum(jnp.abs(den), jnp.exp(-m_t[h]))
        mu = jnp.mean(hc, axis=1, keepdims=True)
        hz = hc - mu
        var = jnp.mean(hz * hz, axis=1, keepdims=True)
        outs.append(hz * lax.rsqrt(var + MLSTM_LN_EPS))
        C_ref[h] = dec[h] * Ch[h] + upd[h]
        n_ref[h:h + 1, :] = dec[h] * nh[h] + jnp.sum(kw[h], axis=0, keepdims=True)
        m_ref[h:h + 1, :] = jnp.broadcast_to(m_new[h], (1, LANE))

    hm = jnp.concatenate(outs, axis=1) * nw_ref[...]
    out = (jax.nn.sigmoid(mo) * hm + skip_ref[...] * mc) * _silu(mz)
    h_ref[0] = out.astype(h_ref.dtype)

    @pl.when(c == pl.num_programs(1) - 1)
    def _():
        C1_ref[0] = C_ref[...]
        n1_ref[0] = n_ref[...]
        m1_ref[0] = m_ref[...]


def _mlstm(u, e, conv0, C0, n0, m0, p, L):
    B, T, W = u.shape
    H, DH = C0.shape[1], C0.shape[2]
    D = H * DH
    KW = p["m_cw"].shape[0]
    m0b = jnp.broadcast_to(m0[..., None], (B, H, LANE))
    params = [p["m_cw"], p["m_cb"], p["m_wq"], p["m_wk"], p["m_wv"], p["m_bif"], p["m_nw"], p["m_skip"]]
    per_b = lambda a: pl.BlockSpec((1,) + a.shape[1:], lambda b, c: (b,) + (0,) * (a.ndim - 1))
    full = lambda a: pl.BlockSpec(a.shape, lambda b, c: (0,) * a.ndim)
    h, C1, n1, m1 = pl.pallas_call(
        functools.partial(_mlstm_kernel, L=L, H=H, DH=DH, KW=KW),
        grid=(B, T // L),
        in_specs=[pl.BlockSpec((1, L, W), lambda b, c: (b, c, 0)),
                  pl.BlockSpec((1, L, E_W), lambda b, c: (b, c, 0)),
                  per_b(conv0), per_b(C0), per_b(n0), per_b(m0b)] + [full(a) for a in params],
        out_specs=[pl.BlockSpec((1, L, D), lambda b, c: (b, c, 0)), per_b(C0), per_b(n0), per_b(m0b)],
        out_shape=[jax.ShapeDtypeStruct((B, T, D), BF16), jax.ShapeDtypeStruct(C0.shape, F32),
                   jax.ShapeDtypeStruct(n0.shape, F32), jax.ShapeDtypeStruct(m0b.shape, F32)],
        scratch_shapes=[pltpu.VMEM((L + SUBLANE, D), F32), pltpu.VMEM((H, DH, DH), F32),
                        pltpu.VMEM((H, DH), F32), pltpu.VMEM((H, LANE), F32)],
        compiler_params=pltpu.CompilerParams(dimension_semantics=("parallel", "arbitrary"),
                                             vmem_limit_bytes=VMEM_LIMIT_BYTES),
        name="mlstm",
    )(u, e, conv0, C0, n0, m0b, *params)
    return h, C1, n1, m1[..., 0]


def _ssd_kernel(u_ref, e_ref, conv0_ref, S0_ref, cw_ref, cb_ref, dtb_ref, alog_ref, drow_ref, nw_ref,
                y_ref, S1_ref, xp_ref, S_ref, *, L, H, P, G, N, KW):
    c = pl.program_id(1)
    D = H * P
    E = H // G

    @pl.when(c == 0)
    def _():
        xp_ref[SUBLANE - (KW - 1):SUBLANE, :] = conv0_ref[0]
        S_ref[...] = S0_ref[0]

    u = u_ref[0]
    sz = u[:, 0:D]
    xbc = _silu(_causal_conv(xp_ref, u[:, D:], cw_ref, cb_ref, L, KW))
    xs, Bm, Cm = xbc[:, 0:D], xbc[:, D:D + G * N], xbc[:, D + G * N:]

    causal, tril, triu = _tri_consts(L)
    dt = _softplus(e_ref[0][:, E_DT:E_DT + LANE] + dtb_ref[...])
    dtA = dt * (-jnp.exp(alog_ref[...]))
    eye = _eye(LANE)
    dtT = _nt(eye, dt, HI)[0:H, :]
    cs_col = _nn(tril, dtA, HI)
    cs_row = _nn(_nt(eye, dtA, HI)[0:H, :], triu, HI)

    Hs = range(H)
    Bg = [_b(Bm[:, g * N:(g + 1) * N]) for g in range(G)]
    Cg = [_b(Cm[:, g * N:(g + 1) * N]) for g in range(G)]
    CB = [_nt(Cg[g], Bg[g]) for g in range(G)]
    cc, cr = [cs_col[:, h:h + 1] for h in Hs], [cs_row[h:h + 1, :] for h in Hs]
    xh = [xs[:, h * P:(h + 1) * P] for h in Hs]
    Sh = [S_ref[h] for h in Hs]
    CS = [_nt(Cg[h // E], _b(Sh[h])) for h in Hs]
    Mh = [CB[h // E] * jnp.exp(jnp.where(causal, cc[h] - cr[h], -jnp.inf)) * dtT[h:h + 1, :] for h in Hs]
    ys = [_nn(_b(Mh[h]), _b(xh[h])) + CS[h] * jnp.exp(cc[h]) for h in Hs]
    csL = [cc[h][L - 1:L, :] for h in Hs]
    xw = [xh[h] * (jnp.exp(csL[h] - cc[h]) * dt[:, h:h + 1]) for h in Hs]
    upd = [_tn(_b(xw[h]), Bg[h // E]) for h in Hs]
    for h in Hs:
        S_ref[h] = jnp.exp(csL[h]) * Sh[h] + upd[h]

    yv = (jnp.concatenate(ys, axis=1) + drow_ref[...] * xs) * _silu(sz)
    DG = D // G
    parts = []
    for g in range(G):
        seg = yv[:, g * DG:(g + 1) * DG]
        parts.append(seg * lax.rsqrt(jnp.mean(seg * seg, axis=1, keepdims=True) + SSD_GN_EPS))
    y_ref[0] = (jnp.concatenate(parts, axis=1) * nw_ref[...]).astype(y_ref.dtype)

    @pl.when(c == pl.num_programs(1) - 1)
    def _():
        S1_ref[0] = S_ref[...]


def _ssd(u, e, conv0, S0, p, L):
    B, T, W = u.shape
    H, P, N = S0.shape[1:]
    D = H * P
    DC = conv0.shape[-1]
    G = (DC - D) // (2 * N)
    KW = p["s_cw"].shape[0]
    params = [p["s_cw"], p["s_cb"], p["s_dtb"], p["s_alog"], p["s_drow"], p["s_nw"]]
    per_b = lambda a: pl.BlockSpec((1,) + a.shape[1:], lambda b, c: (b,) + (0,) * (a.ndim - 1))
    full = lambda a: pl.BlockSpec(a.shape, lambda b, c: (0,) * a.ndim)
    return pl.pallas_call(
        functools.partial(_ssd_kernel, L=L, H=H, P=P, G=G, N=N, KW=KW),
        grid=(B, T // L),
        in_specs=[pl.BlockSpec((1, L, W), lambda b, c: (b, c, 0)),
                  pl.BlockSpec((1, L, E_W), lambda b, c: (b, c, 0)),
                  per_b(conv0), per_b(S0)] + [full(a) for a in params],
        out_specs=[pl.BlockSpec((1, L, D), lambda b, c: (b, c, 0)), per_b(S0)],
        out_shape=[jax.ShapeDtypeStruct((B, T, D), BF16), jax.ShapeDtypeStruct(S0.shape, F32)],
        scratch_shapes=[pltpu.VMEM((L + SUBLANE, DC), F32), pltpu.VMEM((H, P, N), F32)],
        compiler_params=pltpu.CompilerParams(dimension_semantics=("parallel", "arbitrary"),
                                             vmem_limit_bytes=VMEM_LIMIT_BYTES),
        name="ssd",
    )(u, e, conv0, S0, *params)


def _split(x):
    hi = x.astype(BF16)
    return hi, (x - hi.astype(F32)).astype(BF16)


def _cat0(*xs):
    return jnp.concatenate(xs, axis=0)


def _cat1(*xs):
    return jnp.concatenate(xs, axis=1)


def _dot_split(dot, x, rhs_hi, rhs_lo):
    xh, xl = _split(x)
    m = x.shape[0]
    r = dot(_cat0(xh, xl), rhs_hi)
    return r[:m] + r[m:] + dot(xh, rhs_lo)


class _BlockDiag:
    def __init__(self, L, HD):
        n = RWKV_PACK
        self.hm = [((_iota2((L, n * HD), 1) // HD) == h).astype(BF16) for h in range(n)]
        self.cm = [((_iota2((L, n * L), 1) // L) == h).astype(BF16) for h in range(n)]

    def heads(self, xb):
        return _cat0(*[xb * m for m in self.hm])

    def cols(self, xb):
        return _cat0(*[xb * m for m in self.cm])

    def heads2(self, x):
        xh, xl = _split(x)
        return self.heads(xh), self.heads(xl)

    def cols2(self, x):
        xh, xl = _split(x)
        return self.cols(xh), self.cols(xl)


def _unit_lower_solve(As, rhss, L, bd):
    base = min(SOLVE_BASE, L)
    W, HW = As[0].shape[1], rhss[0].shape[1]
    t, i = _iota2(As[0].shape, 0), _iota2(As[0].shape, 1) % L
    same = (t // base) == (i // base)
    eye = (t == i).astype(F32)
    Ads = [jnp.where(same, A, 0.0) for A in As]
    Tks = [eye + Ad for Ad in Ads]
    Pks = [_dot_split(_nn, Ad, *bd.cols2(Ad)) for Ad in Ads]
    n = 2
    while n < base:
        if 2 * n < base:
            rs = []
            for Tk, Pk in zip(Tks, Pks):
                (th, tl), (ph, pl_) = bd.cols2(Tk), bd.cols2(Pk)
                rs.append(_dot_split(_nn, Pk, _cat1(th, ph), _cat1(tl, pl_)))
            Tks = [Tk + r[:, :W] for Tk, r in zip(Tks, rs)]
            Pks = [r[:, W:] for r in rs]
        else:
            Tks = [Tk + _dot_split(_nn, Pk, *bd.cols2(Tk)) for Tk, Pk in zip(Tks, Pks)]
        n *= 2
    nb = L // base
    if nb == 1:
        return [_dot_split(_nn, Tk, *bd.heads2(rhs)) for Tk, rhs in zip(Tks, rhss)]
    rs = []
    for A, Tk, rhs in zip(As, Tks, rhss):
        (rh, rl), (nh, nl) = bd.heads2(rhs), bd.cols2(jnp.where(same, 0.0, A))
        rs.append(_dot_split(_nn, Tk, _cat1(rh, nh), _cat1(rl, nl)))
    Xs, Ms = [r[:, :HW] for r in rs], [r[:, HW:] for r in rs]
    n = 1
    while n < nb:
        if 2 * n < nb:
            rs = []
            for X, M in zip(Xs, Ms):
                (xh, xl), (mh, ml) = bd.heads2(X), bd.cols2(M)
                rs.append(_dot_split(_nn, M, _cat1(xh, mh), _cat1(xl, ml)))
            Xs = [X + r[:, :HW] for X, r in zip(Xs, rs)]
            Ms = [r[:, HW:] for r in rs]
        else:
            Xs = [X + _dot_split(_nn, M, *bd.heads2(X)) for X, M in zip(Xs, Ms)]
        n *= 2
    return Xs


def _rwkv_kernel(u_ref, e_ref, sha_ref, shb_ref, S0_ref,
                 mua_ref, mub_ref, w0_ref, w2_ref, a0_ref, a2_ref, kk_ref, ka_ref, rk_ref, lnw_ref, lnb_ref,
                 y_ref, S1_ref, xpa_ref, xpb_ref, S_ref, *, L, H, HD):
    c = pl.program_id(1)
    D = H * HD
    GW = RWKV_PACK * HD
    NG = H // RWKV_PACK
    CW = RWKV_PACK * L

    @pl.when(c == 0)
    def _():
        xpa_ref[SUBLANE - 1:SUBLANE, :] = sha_ref[0]
        xpb_ref[SUBLANE - 1:SUBLANE, :] = shb_ref[0]
        S_ref[...] = jnp.zeros(S_ref.shape, F32)
        for h in range(H):
            g, j = divmod(h, RWKV_PACK)
            S_ref[g, j * HD:(j + 1) * HD, j * HD:(j + 1) * HD] = S0_ref[0, h]

    def shift_mix(xp_ref, cur, mu_ref):
        xp_ref[SUBLANE:SUBLANE + L, :] = cur
        prev = xp_ref[SUBLANE - 1:SUBLANE - 1 + L, :]
        xp_ref[SUBLANE - 1:SUBLANE, :] = cur[L - 1:L, :]
        return cur + (prev - cur) * mu_ref[...]

    u = u_ref[0]
    rx = shift_mix(xpa_ref, u[:, 0:3 * D], mua_ref)
    lo = shift_mix(xpb_ref, e_ref[0][:, E_LO:E_LO + LANE], mub_ref)
    rz = u[:, 3 * D:4 * D]
    rr, kr, vr = rx[:, 0:D], rx[:, D:2 * D], rx[:, 2 * D:3 * D]

    seg = ((_iota2((GW, GW), 0) // HD) == (_iota2((GW, GW), 1) // HD))
    segb = seg.astype(BF16)

    def head_sum(x):
        parts = []
        for g in range(NG):
            xh, xl = _split(x[:, g * GW:(g + 1) * GW])
            r = _nn(_cat0(xh, xl), segb)
            parts.append(r[:L] + r[L:])
        return _cat1(*parts)

    w_log = -_softplus(-(w0_ref[...] + _dot_split(_nn, jnp.tanh(lo), *_split(w2_ref[...])))) - 0.5
    lw = -jnp.exp(w_log)
    a_sig = jax.nn.sigmoid(a0_ref[...] + _dot_split(_nn, lo, *_split(a2_ref[...])))
    kk = kr * kk_ref[...]
    kk = kk / jnp.maximum(jnp.sqrt(head_sum(kk * kk)), 1e-12)
    kmod = kr * (1.0 + (a_sig - 1.0) * ka_ref[...])
    kb = kk * a_sig

    trib = _tri_consts(L)[1].astype(BF16)
    l1, l2 = _split(lw)
    l3 = _b(lw - l1.astype(F32) - l2.astype(F32))
    cum = _nn(trib, l1) + _nn(trib, l2) + _nn(trib, l3)
    cumL = cum[L - 1:L, :]
    p_in, p_inv, p_end = jnp.exp(cum), jnp.exp(-cum), jnp.exp(cumL - cum)
    At = -kk * jnp.exp(cum - lw)
    Bt, Kt, Rt = kb * p_inv, kmod * p_inv, rr * p_in
    Bd, Kd = kb * p_end, kmod * p_end
    PL = jnp.exp(cumL)

    t, i = _iota2((L, 2 * CW), 0), _iota2((L, 2 * CW), 1) % L
    strict, incl = i < t, i <= t
    bd = _BlockDiag(L, HD)

    G = range(NG)
    gsl = [slice(g * GW, (g + 1) * GW) for g in G]
    Sbd = [S_ref[g] for g in G]
    Ahl = [_split(At[:, s]) for s in gsl]
    ARh = [_cat0(Ahl[g][0], Ahl[g][1], _b(Rt[:, gsl[g]])) for g in G]
    Bst = [bd.heads2(Bt[:, s]) for s in gsl]
    Kst = [bd.heads2(Kt[:, s]) for s in gsl]
    Vst = [bd.heads2(vr[:, s]) for s in gsl]
    Shl = [_split(S) for S in Sbd]
    g1 = [_nt(ARh[g], _cat0(Bst[g][0], Kst[g][0])) for g in G]
    g2 = [_nt(Ahl[g][0], _cat0(Bst[g][1], Kst[g][1])) for g in G]
    s1 = [_nt(ARh[g], Shl[g][0]) for g in G]
    s2 = [_nt(Ahl[g][0], Shl[g][1]) for g in G]
    GA = [jnp.where(strict, g1[g][:L] + g1[g][L:2 * L] + g2[g], 0.0) for g in G]
    GR = [jnp.where(incl, g1[g][2 * L:], 0.0) for g in G]
    rhs = [s1[g][:L] + s1[g][L:2 * L] + s2[g] + _dot_split(_nn, GA[g][:, CW:], *Vst[g]) for g in G]
    U = _unit_lower_solve([GA[g][:, :CW] for g in G], rhs, L, bd)
    Ub = [_b(x) for x in U]
    ys = [s1[g][2 * L:] + _nn(_b(GR[g]), _cat0(bd.heads(Ub[g]), Vst[g][0])) for g in G]
    for g in G:
        upd = _tn(_cat0(Ub[g], _b(vr[:, gsl[g]])), _cat0(_b(Bd[:, gsl[g]]), _b(Kd[:, gsl[g]])))
        S_ref[g] = Sbd[g] * PL[:, gsl[g]] + jnp.where(seg, upd, 0.0)

    y = jnp.concatenate(ys, axis=1)
    mu = head_sum(y) * (1.0 / HD)
    yz = y - mu
    var = head_sum(yz * yz) * (1.0 / HD)
    yn = yz * lax.rsqrt(var + RWKV_GN_EPS) * lnw_ref[...] + lnb_ref[...]
    bonus = head_sum(rr * kmod * rk_ref[...]) * vr
    y_ref[0] = ((yn + bonus) * _silu(rz)).astype(y_ref.dtype)

    @pl.when(c == pl.num_programs(1) - 1)
    def _():
        for h in range(H):
            g, j = divmod(h, RWKV_PACK)
            S1_ref[0, h] = S_ref[g, j * HD:(j + 1) * HD, j * HD:(j + 1) * HD]


def _rwkv(u, e, sha, shb, S0, p, L):
    B, T, W = u.shape
    H, HD = S0.shape[1], S0.shape[2]
    D = H * HD
    GW = RWKV_PACK * HD
    params = [p["r_mua"], p["r_mub"], p["r_w0"], p["r_w2"], p["r_a0"], p["r_a2"], p["r_kk"], p["r_ka"],
              p["r_rk"], p["r_lnw"], p["r_lnb"]]
    per_b = lambda a: pl.BlockSpec((1,) + a.shape[1:], lambda b, c: (b,) + (0,) * (a.ndim - 1))
    full = lambda a: pl.BlockSpec(a.shape, lambda b, c: (0,) * a.ndim)
    return pl.pallas_call(
        functools.partial(_rwkv_kernel, L=L, H=H, HD=HD),
        grid=(B, T // L),
        in_specs=[pl.BlockSpec((1, L, W), lambda b, c: (b, c, 0)),
                  pl.BlockSpec((1, L, E_W), lambda b, c: (b, c, 0)),
                  per_b(sha), per_b(shb), per_b(S0)] + [full(a) for a in params],
        out_specs=[pl.BlockSpec((1, L, D), lambda b, c: (b, c, 0)), per_b(S0)],
        out_shape=[jax.ShapeDtypeStruct((B, T, D), BF16), jax.ShapeDtypeStruct(S0.shape, F32)],
        scratch_shapes=[pltpu.VMEM((L + SUBLANE, 3 * D), F32), pltpu.VMEM((L + SUBLANE, LANE), F32),
                        pltpu.VMEM((H // RWKV_PACK, GW, GW), F32)],
        compiler_params=pltpu.CompilerParams(dimension_semantics=("parallel", "arbitrary"),
                                             vmem_limit_bytes=VMEM_LIMIT_BYTES),
        name="rwkv7",
    )(u, e, sha, shb, S0, *params)


def _bd_tiles(w):
    dep, nb, qb, _ = w.shape
    per = MXU_DIM // qb
    wt = w.reshape(dep, nb // per, per, qb, qb)
    tiles = jnp.einsum("ltncd,nm->ltncmd", wt, jnp.eye(per, dtype=w.dtype))
    return tiles.reshape(dep, nb // per, MXU_DIM, MXU_DIM)


def _pad_lanes(a, width=LANE):
    return jnp.pad(a, [(0, 0)] * (a.ndim - 1) + [(0, width - a.shape[-1])])


def _prep_params(D, w_in, b_gate, m_conv_w, m_conv_b, m_wq, m_wk, m_wv, m_b_if, m_norm_w, m_skip, m_w_out,
                 s_conv_w, s_conv_b, s_dt_bias, s_A_log, s_D, s_norm_w, s_w_out,
                 r_mu, r_w0, r_w2, r_a0, r_a2, r_k_k, r_k_a, r_r_k, r_ln_w, r_ln_b, r_w_out, w_out):
    H_S = D // P_S
    DC = D + 2 * G_S * N_S
    n_lo = LORA_W + LORA_A
    assert n_lo == LANE and 2 * H_M <= SUBLANE and H_S <= LANE
    off_gate = 0
    off_m = off_gate + 3 * D
    off_if = off_m + 3 * D
    off_sz = off_if + 2 * H_M
    off_dt = off_sz + D + DC
    off_rsh = off_dt + H_S
    off_rz = off_rsh + 3 * D + n_lo
    assert off_rz + D == w_in.shape[-1]
    col = lambda o, n: w_in[:, :, o:o + n]
    row = lambda a: a[:, None, :]
    DH = D // H_M
    zl = jnp.zeros_like(r_w2)
    return dict(
        wA=_b(col(off_gate, 3 * D)),
        wB=_b(col(off_m, 3 * D)),
        wC=_b(col(off_sz, D + DC)),
        wD=_b(jnp.concatenate([col(off_rsh, 3 * D), col(off_rz, D)], axis=-1)),
        wE=_b(jnp.concatenate([col(off_rsh + 3 * D, n_lo), _pad_lanes(col(off_if, 2 * H_M)),
                               _pad_lanes(col(off_dt, H_S))], axis=-1)),
        bg=row(b_gate),
        m_cw=m_conv_w, m_cb=row(m_conv_b),
        m_wq=_b(_bd_tiles(m_wq)), m_wk=_b(_bd_tiles(m_wk) * DH ** -0.5), m_wv=_b(_bd_tiles(m_wv)),
        m_bif=row(_pad_lanes(m_b_if)), m_nw=row(m_norm_w), m_skip=row(m_skip), m_wo=_b(m_w_out),
        s_cw=s_conv_w, s_cb=row(s_conv_b), s_dtb=row(_pad_lanes(s_dt_bias)), s_alog=row(_pad_lanes(s_A_log)),
        s_drow=row(jnp.repeat(s_D, P_S, axis=-1)), s_nw=row(s_norm_w), s_wo=_b(s_w_out),
        r_mua=row(r_mu[:, :3 * D]), r_mub=row(r_mu[:, 3 * D:]),
        r_w0=row(r_w0), r_w2=jnp.concatenate([r_w2, zl], axis=1),
        r_a0=row(r_a0), r_a2=jnp.concatenate([zl, r_a2], axis=1),
        r_kk=row(r_k_k), r_ka=row(r_k_a), r_rk=row(r_r_k.reshape(r_r_k.shape[0], -1)),
        r_lnw=row(r_ln_w), r_lnb=row(r_ln_b), r_wo=_b(r_w_out),
        wo=_b(w_out),
    )


def _row_tile(n, cap):
    t = min(n, cap)
    while n % t:
        t //= 2
    return t


def _run(x, states, prep, norm_w, final_norm_w):
    B, T, D = x.shape
    depth = norm_w.shape[0]
    mC, mn, mm, mconv, sS, sconv, rS, rsh = states
    L = CHUNK if T % CHUNK == 0 else T
    KW = mconv.shape[2] + 1
    assert T >= KW - 1 and L % SUBLANE == 0
    N = B * T
    tm = _row_tile(N, 1024)
    tmo = _row_tile(N, 256)
    x2 = x.reshape(N, D)
    h = _rmsnorm(x2, norm_w[0][None, :], BF16, tm)
    new = [[] for _ in range(8)]
    for l in range(depth):
        p = {k: v[l] for k, v in prep.items()}
        proj = lambda w: _matmul(h, w, tm, _row_tile(w.shape[1], 1024) if w.shape[1] % 1024 == 0 else w.shape[1])
        uA = proj(p["wA"])
        uB = proj(p["wB"]).reshape(B, T, -1)
        uC = proj(p["wC"]).reshape(B, T, -1)
        uD = proj(p["wD"]).reshape(B, T, -1)
        uE = proj(p["wE"]).reshape(B, T, -1)
        hm, mC1, mn1, mm1 = _mlstm(uB, uE, mconv[l], mC[l], mn[l], mm[l], p, L)
        ys, sS1 = _ssd(uC, uE, sconv[l], sS[l], p, L)
        yr, rS1 = _rwkv(uD, uE, rsh[l][..., :3 * D], rsh[l][..., 3 * D:], rS[l], p, L)
        last = l == depth - 1
        nw = (final_norm_w if last else norm_w[l + 1])[None, :]
        x2, h = _out_proj(hm.reshape(N, D), ys.reshape(N, D), yr.reshape(N, D), uA, x2,
                          p["m_wo"], p["s_wo"], p["r_wo"], p["wo"], p["bg"], nw, F32 if last else BF16, tmo)
        outs = (mC1, mn1, mm1, uB[:, T - (KW - 1):, 0:D], sS1, uC[:, T - (KW - 1):, D:],
                rS1, jnp.concatenate([uD[:, T - 1:, 0:3 * D], uE[:, T - 1:, E_LO:E_LO + LANE]], axis=-1))
        for acc, s in zip(new, outs):
            acc.append(s)
    return h.reshape(B, T, D), [jnp.stack(s) for s in new]


def kernel(x_prompt, x_sample, state_mlstm_C, state_mlstm_n, state_mlstm_m, state_mlstm_conv, state_ssd,
           state_ssd_conv, state_rwkv, state_rwkv_shift, norm_w, w_in, b_gate, m_conv_w, m_conv_b, m_wq, m_wk,
           m_wv, m_b_if, m_norm_w, m_skip, m_w_out, s_conv_w, s_conv_b, s_dt_bias, s_A_log, s_D, s_norm_w,
           s_w_out, r_mu, r_w0, r_w2, r_a0, r_a2, r_k_k, r_k_a, r_r_k, r_ln_w, r_ln_b, r_w_out, w_out,
           final_norm_w):
    D = x_prompt.shape[-1]
    prep = _prep_params(D, w_in, b_gate, m_conv_w, m_conv_b, m_wq, m_wk, m_wv, m_b_if, m_norm_w, m_skip, m_w_out,
                        s_conv_w, s_conv_b, s_dt_bias, s_A_log, s_D, s_norm_w, s_w_out,
                        r_mu, r_w0, r_w2, r_a0, r_a2, r_k_k, r_k_a, r_r_k, r_ln_w, r_ln_b, r_w_out, w_out)
    sample_states = (state_mlstm_C, state_mlstm_n, state_mlstm_m, state_mlstm_conv, state_ssd,
                     state_ssd_conv, state_rwkv, state_rwkv_shift)
    Bp = x_prompt.shape[0]
    zero_states = tuple(jnp.zeros(s.shape[:1] + (Bp,) + s.shape[2:], s.dtype) for s in sample_states)
    y_p, st_p = _run(x_prompt, zero_states, prep, norm_w, final_norm_w)
    y_s, st_s = _run(x_sample, sample_states, prep, norm_w, final_norm_w)
    return (y_p, y_s, *st_p, *st_s)
```

```python
import functools

import jax
import jax.numpy as jnp
from jax import lax
from jax.experimental import pallas as pl
from jax.experimental.pallas import tpu as pltpu

F32 = jnp.float32
BF16 = jnp.bfloat16
HI = lax.Precision.HIGHEST

CHUNK = 64
NORM_EPS = 1e-6
H_M = 4
MLSTM_LN_EPS = 1e-5
P_S = 64
G_S = 4
N_S = 128
SSD_GN_EPS = 1e-5
HD_R = 64
LORA_W = 64
LORA_A = 64
RWKV_GN_EPS = 64e-5
RWKV_PACK = 4
SOLVE_BASE = 16

LANE = 128
SUBLANE = 8
MXU_DIM = 256
VMEM_LIMIT_BYTES = 48 * 1024 * 1024

E_LO = 0 * LANE
E_IF = 1 * LANE
E_DT = 2 * LANE
E_W = 3 * LANE


def _nn(a, b, prec=None):
    return jnp.dot(a, b, precision=prec, preferred_element_type=F32)


def _nt(a, b, prec=None):
    return lax.dot_general(a, b, (((1,), (1,)), ((), ())), precision=prec, preferred_element_type=F32)


def _tn(a, b, prec=None):
    return lax.dot_general(a, b, (((0,), (0,)), ((), ())), precision=prec, preferred_element_type=F32)


def _b(x):
    return x.astype(BF16)


def _bf16_terms(x, n):
    terms, r = [], x
    for _ in range(n):
        t = _b(r)
        terms.append(t)
        r = r - t.astype(F32)
    return terms


def _split(x):
    return tuple(_bf16_terms(x, 2))


def _cat0(*xs):
    return jnp.concatenate(xs, axis=0)


def _cat1(*xs):
    return jnp.concatenate(xs, axis=1)


def _softplus(x):
    return jnp.maximum(x, 0.0) + jnp.log1p(jnp.exp(-jnp.abs(x)))


def _log_sigmoid(x):
    return -_softplus(-x)


def _silu(x):
    return x * jax.nn.sigmoid(x)


def _iota2(shape, dim):
    return lax.broadcasted_iota(jnp.int32, shape, dim)


def _tri_consts(L):
    r, c = _iota2((L, L), 0), _iota2((L, L), 1)
    causal = c <= r
    return causal, causal.astype(F32), (r <= c).astype(F32)


def _eye(n):
    return (_iota2((n, n), 0) == _iota2((n, n), 1)).astype(F32)


def _causal_conv(xp_ref, x, cw_ref, cb_ref, L, KW):
    xfull = _cat0(xp_ref[...], x)
    y = cb_ref[...]
    for j in range(KW):
        s = KW - 1 - j
        xs = x if s == 0 else pltpu.roll(xfull, s, 0)[SUBLANE:SUBLANE + L, :]
        y = y + xs * cw_ref[j:j + 1, :]
    xp_ref[...] = x[L - SUBLANE:L, :]
    return y


def _norm_kernel(x_ref, w_ref, o_ref):
    x = x_ref[...]
    y = x * lax.rsqrt(jnp.mean(x * x, axis=-1, keepdims=True) + NORM_EPS)
    o_ref[...] = (y * w_ref[...]).astype(o_ref.dtype)


def _rmsnorm(x, w, out_dtype, tm):
    n, d = x.shape
    return pl.pallas_call(
        _norm_kernel,
        grid=(n // tm,),
        in_specs=[pl.BlockSpec((tm, d), lambda i: (i, 0)), pl.BlockSpec((1, d), lambda i: (0, 0))],
        out_specs=pl.BlockSpec((tm, d), lambda i: (i, 0)),
        out_shape=jax.ShapeDtypeStruct((n, d), out_dtype),
        compiler_params=pltpu.CompilerParams(dimension_semantics=("parallel",), vmem_limit_bytes=VMEM_LIMIT_BYTES),
        name="rmsnorm",
    )(x, w)


def _matmul_kernel(h_ref, w_ref, o_ref):
    o_ref[...] = _nn(h_ref[...], w_ref[...]).astype(o_ref.dtype)


def _matmul(h, w, tm, tn, out_dtype):
    n, d = h.shape
    nw = w.shape[1]
    return pl.pallas_call(
        _matmul_kernel,
        grid=(nw // tn, n // tm),
        in_specs=[pl.BlockSpec((tm, d), lambda j, i: (i, 0)), pl.BlockSpec((d, tn), lambda j, i: (0, j))],
        out_specs=pl.BlockSpec((tm, tn), lambda j, i: (i, j)),
        out_shape=jax.ShapeDtypeStruct((n, nw), out_dtype),
        compiler_params=pltpu.CompilerParams(dimension_semantics=("parallel", "parallel"),
                                             vmem_limit_bytes=VMEM_LIMIT_BYTES),
        name="proj_in",
    )(h, w)


def _out_kernel(hm_ref, ys_ref, yr_ref, g_ref, x_ref, wm_ref, ws_ref, wr_ref, wo_ref, bg_ref, nw_ref,
                xo_ref, ho_ref, *, D):
    g = jax.nn.sigmoid(g_ref[...].astype(F32) + bg_ref[...])
    merged = (g[:, 0:D] * _nn(hm_ref[...], wm_ref[...])
              + g[:, D:2 * D] * _nn(ys_ref[...], ws_ref[...])
              + g[:, 2 * D:3 * D] * _nn(yr_ref[...], wr_ref[...]))
    out = x_ref[...] + _nn(_b(merged), wo_ref[...])
    xo_ref[...] = out
    y = out * lax.rsqrt(jnp.mean(out * out, axis=-1, keepdims=True) + NORM_EPS)
    ho_ref[...] = (y * nw_ref[...]).astype(ho_ref.dtype)


def _out_proj(hm, ys, yr, gate, x, wm, ws, wr, wo, bg, nw, h_dtype, tm):
    n, d = x.shape
    row = lambda w: pl.BlockSpec((tm, w), lambda i: (i, 0))
    full = lambda a: pl.BlockSpec(a.shape, lambda i: (0, 0))
    return pl.pallas_call(
        functools.partial(_out_kernel, D=d),
        grid=(n // tm,),
        in_specs=[row(d), row(d), row(d), row(3 * d), row(d),
                  full(wm), full(ws), full(wr), full(wo), full(bg), full(nw)],
        out_specs=[row(d), row(d)],
        out_shape=[jax.ShapeDtypeStruct((n, d), F32), jax.ShapeDtypeStruct((n, d), h_dtype)],
        compiler_params=pltpu.CompilerParams(dimension_semantics=("parallel",), vmem_limit_bytes=VMEM_LIMIT_BYTES),
        name="out_proj",
    )(hm, ys, yr, gate, x, wm, ws, wr, wo, bg, nw)


def _blockdiag_apply(a, w_ref):
    ab = _b(a)
    return [_nn(ab[:, t * MXU_DIM:(t + 1) * MXU_DIM], w_ref[t]) for t in range(w_ref.shape[0])]


def _mlstm_kernel(u_ref, e_ref, conv0_ref, C0_ref, n0_ref, m0_ref,
                  cw_ref, cb_ref, wq_ref, wk_ref, wv_ref, bif_ref, nw_ref, skip_ref,
                  h_ref, C1_ref, n1_ref, m1_ref,
                  xp_ref, C_ref, n_ref, m_ref, *, L, H, DH, KW):
    c = pl.program_id(1)
    D = H * DH

    @pl.when(c == 0)
    def _():
        xp_ref[...] = jnp.zeros(xp_ref.shape, F32)
        xp_ref[SUBLANE - (KW - 1):SUBLANE, :] = conv0_ref[0]
        C_ref[...] = C0_ref[0]
        n_ref[...] = n0_ref[0]
        m_ref[...] = m0_ref[0]

    u = u_ref[0].astype(F32)
    mx, mz, mo = u[:, 0:D], u[:, D:2 * D], u[:, 2 * D:3 * D]
    mc = _silu(_causal_conv(xp_ref, mx, cw_ref, cb_ref, L, KW))
    q = _cat1(*_blockdiag_apply(mc, wq_ref))
    k = _cat1(*_blockdiag_apply(mc, wk_ref))
    v = _cat1(*_blockdiag_apply(mx, wv_ref))

    causal, tril, triu = _tri_consts(L)
    ecol = e_ref[0][:, E_IF:E_IF + LANE] + bif_ref[...]
    eT = _nt(_eye(LANE), ecol, HI)[0:SUBLANE, :]
    b_col = _nn(tril, _log_sigmoid(ecol), HI)
    b_row = _nn(_log_sigmoid(eT), triu, HI)

    Hs = range(H)
    sl = [slice(h * DH, (h + 1) * DH) for h in Hs]
    qf, kf = [q[:, s] for s in sl], [k[:, s] for s in sl]
    qh, kh, vh = [_b(x) for x in qf], [_b(x) for x in kf], [_b(v[:, s]) for s in sl]
    bc, br = [b_col[:, H + h:H + h + 1] for h in Hs], [b_row[H + h:H + h + 1, :] for h in Hs]
    ic, ir = [ecol[:, h:h + 1] for h in Hs], [eT[h:h + 1, :] for h in Hs]
    mprev = [m_ref[h:h + 1, 0:1] for h in Hs]
    Ch, nh = [C_ref[h] for h in Hs], [n_ref[h:h + 1, :] for h in Hs]

    qk = [_nt(qh[h], kh[h]) for h in Hs]
    qC = [_nn(qh[h], _b(Ch[h])) for h in Hs]
    logD = [jnp.where(causal, bc[h] - br[h] + ir[h], -jnp.inf) for h in Hs]
    inter = [bc[h] + mprev[h] for h in Hs]
    m_t = [jnp.maximum(inter[h], jnp.max(logD[h], axis=1, keepdims=True)) for h in Hs]
    sc = [jnp.exp(inter[h] - m_t[h]) for h in Hs]
    s = [qk[h] * jnp.exp(logD[h] - m_t[h]) for h in Hs]
    sv = [_nn(_b(s[h]), vh[h]) for h in Hs]

    bL = [bc[h][L - 1:L, :] for h in Hs]
    m_new = [jnp.maximum(bL[h] + mprev[h], jnp.max(bL[h] - br[h] + ir[h], axis=1, keepdims=True)) for h in Hs]
    dec = [jnp.exp(bL[h] + mprev[h] - m_new[h]) for h in Hs]
    kw = [kf[h] * jnp.exp(bL[h] - bc[h] + ic[h] - m_new[h]) for h in Hs]
    upd = [_tn(_b(kw[h]), vh[h]) for h in Hs]

    outs = []
    for h in Hs:
        den = jnp.sum(s[h], axis=1, keepdims=True) + sc[h] * jnp.sum(qf[h] * nh[h], axis=1, keepdims=True)
        hc = (sv[h] + qC[h] * sc[h]) / jnp.maximum(jnp.abs(den), jnp.exp(-m_t[h]))
        mu = jnp.mean(hc, axis=1, keepdims=True)
        hz = hc - mu
        var = jnp.mean(hz * hz, axis=1, keepdims=True)
        outs.append(hz * lax.rsqrt(var + MLSTM_LN_EPS))
        C_ref[h] = dec[h] * Ch[h] + upd[h]
        n_ref[h:h + 1, :] = dec[h] * nh[h] + jnp.sum(kw[h], axis=0, keepdims=True)
        m_ref[h:h + 1, :] = jnp.broadcast_to(m_new[h], (1, LANE))

    hm = jnp.concatenate(outs, axis=1) * nw_ref[...]
    out = (jax.nn.sigmoid(mo) * hm + skip_ref[...] * mc) * _silu(mz)
    h_ref[0] = out.astype(h_ref.dtype)

    @pl.when(c == pl.num_programs(1) - 1)
    def _():
        C1_ref[0] = C_ref[...]
        n1_ref[0] = n_ref[...]
        m1_ref[0] = m_ref[...]


def _mlstm(u, e, conv0, C0, n0, m0, p, L):
    B, T, W = u.shape
    H, DH = C0.shape[1], C0.shape[2]
    D = H * DH
    KW = p["m_cw"].shape[0]
    m0b = jnp.broadcast_to(m0[..., None], (B, H, LANE))
    params = [p["m_cw"], p["m_cb"], p["m_wq"], p["m_wk"], p["m_wv"], p["m_bif"], p["m_nw"], p["m_skip"]]
    per_b = lambda a: pl.BlockSpec((1,) + a.shape[1:], lambda b, c: (b,) + (0,) * (a.ndim - 1))
    full = lambda a: pl.BlockSpec(a.shape, lambda b, c: (0,) * a.ndim)
    h, C1, n1, m1 = pl.pallas_call(
        functools.partial(_mlstm_kernel, L=L, H=H, DH=DH, KW=KW),
        grid=(B, T // L),
        in_specs=[pl.BlockSpec((1, L, W), lambda b, c: (b, c, 0)),
                  pl.BlockSpec((1, L, E_W), lambda b, c: (b, c, 0)),
                  per_b(conv0), per_b(C0), per_b(n0), per_b(m0b)] + [full(a) for a in params],
        out_specs=[pl.BlockSpec((1, L, D), lambda b, c: (b, c, 0)), per_b(C0), per_b(n0), per_b(m0b)],
        out_shape=[jax.ShapeDtypeStruct((B, T, D), BF16), jax.ShapeDtypeStruct(C0.shape, F32),
                   jax.ShapeDtypeStruct(n0.shape, F32), jax.ShapeDtypeStruct(m0b.shape, F32)],
        scratch_shapes=[pltpu.VMEM((SUBLANE, D), F32), pltpu.VMEM((H, DH, DH), F32),
                        pltpu.VMEM((H, DH), F32), pltpu.VMEM((H, LANE), F32)],
        compiler_params=pltpu.CompilerParams(dimension_semantics=("parallel", "arbitrary"),
                                             vmem_limit_bytes=VMEM_LIMIT_BYTES),
        name="mlstm",
    )(u, e, conv0, C0, n0, m0b, *params)
    return h, C1, n1, m1[..., 0]


def _ssd_kernel(u_ref, e_ref, conv0_ref, S0_ref, cw_ref, cb_ref, dtb_ref, alog_ref, drow_ref, nw_ref,
                xl_ref, xp_exp_ref, y_ref, S1_ref, xp_ref, S_ref, *, L, H, P, G, N, KW):
    c = pl.program_id(1)
    D = H * P
    E = H // G
    EL, EP = E * L, E * P

    @pl.when(c == 0)
    def _():
        xp_ref[...] = jnp.zeros(xp_ref.shape, F32)
        xp_ref[SUBLANE - (KW - 1):SUBLANE, :] = conv0_ref[0]
        S_ref[...] = S0_ref[0].reshape(G, EP, N)

    u = u_ref[0].astype(F32)
    sz = u[:, 0:D]
    xbc = _silu(_causal_conv(xp_ref, u[:, D:], cw_ref, cb_ref, L, KW))
    xs, Bm, Cm = xbc[:, 0:D], xbc[:, D:D + G * N], xbc[:, D + G * N:]

    dt = _softplus(e_ref[0][:, E_DT:E_DT + LANE] + dtb_ref[...])
    tril = _tri_consts(L)[1]
    cs = _nn(tril, dt * (-jnp.exp(alog_ref[...])), HI)
    csL = cs[L - 1:L, :]

    def expand(items, e_ref_):
        rows = [t for x, n in items for t in _bf16_terms(x, n)]
        r = _nn(_cat0(*rows), e_ref_[...])
        outs, k = [], 0
        for _, n in items:
            outs.append(sum(r[j * L:(j + 1) * L] for j in range(k, k + n)))
            k += n
        return outs

    csx, dtx = expand([(cs, 3), (dt, 2)], xl_ref)
    f_out, f_in = expand([(jnp.exp(cs), 2), (jnp.exp(csL - cs) * dt, 2)], xp_exp_ref)
    row, col = _iota2((L, H * L), 0), _iota2((L, H * L), 1) % L
    diag = row == col
    cs_src = jnp.sum(jnp.where(diag, csx, 0.0), axis=0, keepdims=True)
    dt_src = jnp.sum(jnp.where(diag, dtx, 0.0), axis=0, keepdims=True)
    Mall = jnp.exp(jnp.where(col <= row, csx - cs_src, -jnp.inf)) * dt_src
    decay_end = jnp.exp(csL)
    xw = xs * f_in
    bd = _BlockDiag(L, P, E)

    Gs = range(G)
    Bg = [_b(Bm[:, g * N:(g + 1) * N]) for g in Gs]
    Cg = [_b(Cm[:, g * N:(g + 1) * N]) for g in Gs]
    Sg = [S_ref[g] for g in Gs]
    CB = [_nt(Cg[g], _cat0(*[Bg[g]] * E)) for g in Gs]
    CS = [_nt(Cg[g], _b(Sg[g])) for g in Gs]
    Mg = [_b(CB[g] * Mall[:, g * EL:(g + 1) * EL]) for g in Gs]
    yi = [_nn(Mg[g], bd.heads(_b(xs[:, g * EP:(g + 1) * EP]))) for g in Gs]
    upd = [_tn(_b(xw[:, g * EP:(g + 1) * EP]), Bg[g]) for g in Gs]
    for g in Gs:
        scale = _cat0(*[jnp.broadcast_to(decay_end[:, h:h + 1], (P, 1)) for h in range(g * E, (g + 1) * E)])
        S_ref[g] = scale * Sg[g] + upd[g]

    yv = (_cat1(*yi) + _cat1(*CS) * f_out + drow_ref[...] * xs) * _silu(sz)
    DG = D // G
    parts = []
    for g in range(G):
        seg = yv[:, g * DG:(g + 1) * DG]
        parts.append(seg * lax.rsqrt(jnp.mean(seg * seg, axis=1, keepdims=True) + SSD_GN_EPS))
    y_ref[0] = (jnp.concatenate(parts, axis=1) * nw_ref[...]).astype(y_ref.dtype)

    @pl.when(c == pl.num_programs(1) - 1)
    def _():
        S1_ref[0] = S_ref[...].reshape(H, P, N)


def _lane_expander(n_heads, width):
    return (jnp.arange(LANE)[:, None] == jnp.arange(n_heads * width)[None, :] // width).astype(BF16)


def _ssd(u, e, conv0, S0, p, L):
    B, T, W = u.shape
    H, P, N = S0.shape[1:]
    D = H * P
    DC = conv0.shape[-1]
    G = (DC - D) // (2 * N)
    KW = p["s_cw"].shape[0]
    params = [p["s_cw"], p["s_cb"], p["s_dtb"], p["s_alog"], p["s_drow"], p["s_nw"],
              _lane_expander(H, L), _lane_expander(H, P)]
    per_b = lambda a: pl.BlockSpec((1,) + a.shape[1:], lambda b, c: (b,) + (0,) * (a.ndim - 1))
    full = lambda a: pl.BlockSpec(a.shape, lambda b, c: (0,) * a.ndim)
    return pl.pallas_call(
        functools.partial(_ssd_kernel, L=L, H=H, P=P, G=G, N=N, KW=KW),
        grid=(B, T // L),
        in_specs=[pl.BlockSpec((1, L, W), lambda b, c: (b, c, 0)),
                  pl.BlockSpec((1, L, E_W), lambda b, c: (b, c, 0)),
                  per_b(conv0), per_b(S0)] + [full(a) for a in params],
        out_specs=[pl.BlockSpec((1, L, D), lambda b, c: (b, c, 0)), per_b(S0)],
        out_shape=[jax.ShapeDtypeStruct((B, T, D), BF16), jax.ShapeDtypeStruct(S0.shape, F32)],
        scratch_shapes=[pltpu.VMEM((SUBLANE, DC), F32), pltpu.VMEM((G, H // G * P, N), F32)],
        compiler_params=pltpu.CompilerParams(dimension_semantics=("parallel", "arbitrary"),
                                             vmem_limit_bytes=VMEM_LIMIT_BYTES),
        name="ssd",
    )(u, e, conv0, S0, *params)


def _dot_split(dot, x, rhs_hi, rhs_lo):
    xh, xl = _split(x)
    m = x.shape[0]
    r = dot(_cat0(xh, xl), rhs_hi)
    return r[:m] + r[m:] + dot(xh, rhs_lo)


class _BlockDiag:
    def __init__(self, L, HD, n):
        self.hm = [((_iota2((L, n * HD), 1) // HD) == h).astype(BF16) for h in range(n)]
        self.cm = [((_iota2((L, n * L), 1) // L) == h).astype(BF16) for h in range(n)]

    def heads(self, xb):
        return _cat0(*[xb * m for m in self.hm])

    def cols(self, xb):
        return _cat0(*[xb * m for m in self.cm])

    def heads2(self, x):
        xh, xl = _split(x)
        return self.heads(xh), self.heads(xl)

    def cols2(self, x):
        xh, xl = _split(x)
        return self.cols(xh), self.cols(xl)


def _unit_lower_solve(As, rhss, L, bd):
    base = min(SOLVE_BASE, L)
    W = As[0].shape[1]
    t, i = _iota2(As[0].shape, 0), _iota2(As[0].shape, 1) % L
    same = (t // base) == (i // base)
    eye = (t == i).astype(F32)
    mmc = lambda x, y: _nn(_b(x), bd.cols(_b(y)))
    mmv = lambda x, v: _nn(_b(x), bd.heads(_b(v)))
    Ads = [jnp.where(same, A, 0.0) for A in As]
    Tks = [eye + Ad for Ad in Ads]
    Pks = [mmc(Ad, Ad) for Ad in Ads]
    n = 2
    while n < base:
        if 2 * n < base:
            rs = [_nn(_b(Pk), _cat1(bd.cols(_b(Tk)), bd.cols(_b(Pk)))) for Tk, Pk in zip(Tks, Pks)]
            Tks = [Tk + r[:, :W] for Tk, r in zip(Tks, rs)]
            Pks = [r[:, W:] for r in rs]
        else:
            Tks = [Tk + mmc(Pk, Tk) for Tk, Pk in zip(Tks, Pks)]
        n *= 2
    nb = L // base
    if nb == 1:
        return [mmv(Tk, V) for Tk, V in zip(Tks, rhss)]
    HW = rhss[0].shape[1]
    rs = [_nn(_b(Tk), _cat1(bd.heads(_b(V)), bd.cols(_b(jnp.where(same, 0.0, A)))))
          for Tk, V, A in zip(Tks, rhss, As)]
    Xs, Ms = [r[:, :HW] for r in rs], [r[:, HW:] for r in rs]
    n = 1
    while n < nb:
        if 2 * n < nb:
            rs = [_nn(_b(M), _cat1(bd.heads(_b(X)), bd.cols(_b(M)))) for X, M in zip(Xs, Ms)]
            Xs = [X + r[:, :HW] for X, r in zip(Xs, rs)]
            Ms = [r[:, HW:] for r in rs]
        else:
            Xs = [X + mmv(M, X) for X, M in zip(Xs, Ms)]
        n *= 2
    return Xs


def _rwkv_kernel(u_ref, e_ref, sha_ref, shb_ref, S0_ref,
                 mua_ref, mub_ref, w0_ref, w2_ref, a0_ref, a2_ref, kk_ref, ka_ref, rk_ref, lnw_ref, lnb_ref,
                 y_ref, S1_ref, xpa_ref, xpb_ref, S_ref, *, L, H, HD):
    c = pl.program_id(1)
    D = H * HD
    GW = RWKV_PACK * HD
    NG = H // RWKV_PACK
    CW = RWKV_PACK * L

    @pl.when(c == 0)
    def _():
        xpa_ref[...] = jnp.zeros(xpa_ref.shape, F32)
        xpb_ref[...] = jnp.zeros(xpb_ref.shape, F32)
        xpa_ref[SUBLANE - 1:SUBLANE, :] = sha_ref[0]
        xpb_ref[SUBLANE - 1:SUBLANE, :] = shb_ref[0]
        S_ref[...] = jnp.zeros(S_ref.shape, F32)
        for h in range(H):
            g, j = divmod(h, RWKV_PACK)
            S_ref[g, j * HD:(j + 1) * HD, j * HD:(j + 1) * HD] = S0_ref[0, h]

    def shift_mix(xp_ref, cur, mu_ref):
        prev = pltpu.roll(_cat0(xp_ref[...], cur), 1, 0)[SUBLANE:SUBLANE + L, :]
        xp_ref[...] = cur[L - SUBLANE:L, :]
        return cur + (prev - cur) * mu_ref[...]

    u = u_ref[0].astype(F32)
    rx = shift_mix(xpa_ref, u[:, 0:3 * D], mua_ref)
    lo = shift_mix(xpb_ref, e_ref[0][:, E_LO:E_LO + LANE], mub_ref)
    rz = u[:, 3 * D:4 * D]
    rr, kr, vr = rx[:, 0:D], rx[:, D:2 * D], rx[:, 2 * D:3 * D]

    seg = ((_iota2((GW, GW), 0) // HD) == (_iota2((GW, GW), 1) // HD))
    segb = seg.astype(BF16)

    def head_sum(x):
        xh, xl = _split(x)
        rows = [t[:, g * GW:(g + 1) * GW] for t in (xh, xl) for g in range(NG)]
        r = _nn(_cat0(*rows), segb)
        return _cat1(*[r[g * L:(g + 1) * L] + r[(NG + g) * L:(NG + g + 1) * L] for g in range(NG)])

    w_log = -_softplus(-(w0_ref[...] + _dot_split(_nn, jnp.tanh(lo), *_split(w2_ref[...])))) - 0.5
    lw = -jnp.exp(w_log)
    a_sig = jax.nn.sigmoid(a0_ref[...] + _nn(_b(lo), _b(a2_ref[...])))
    kk = kr * kk_ref[...]
    kk = kk / jnp.maximum(jnp.sqrt(head_sum(kk * kk)), 1e-12)
    kmod = kr * (1.0 + (a_sig - 1.0) * ka_ref[...])
    kb = kk * a_sig

    trib = _tri_consts(L)[1].astype(BF16)
    l1, l2 = _split(lw)
    l3 = _b(lw - l1.astype(F32) - l2.astype(F32))
    cum = _nn(trib, l1) + _nn(trib, l2) + _nn(trib, l3)
    cumL = cum[L - 1:L, :]
    p_in, p_inv, p_end = jnp.exp(cum), jnp.exp(-cum), jnp.exp(cumL - cum)
    At = -kk * jnp.exp(cum - lw)
    Bt, Kt, Rt = kb * p_inv, kmod * p_inv, rr * p_in
    Bd, Kd = kb * p_end, kmod * p_end
    PL = jnp.exp(cumL)

    t, i = _iota2((L, 2 * CW), 0), _iota2((L, 2 * CW), 1) % L
    strict, incl = i < t, i <= t
    bd = _BlockDiag(L, HD, RWKV_PACK)

    G = range(NG)
    gsl = [slice(g * GW, (g + 1) * GW) for g in G]
    Sbd = [S_ref[g] for g in G]
    AR = [_cat0(_b(At[:, s]), _b(Rt[:, s])) for s in gsl]
    Bst = [bd.heads(_b(Bt[:, s])) for s in gsl]
    Kst = [bd.heads(_b(Kt[:, s])) for s in gsl]
    Vst = [bd.heads(_b(vr[:, s])) for s in gsl]
    g1 = [_nt(AR[g], _cat0(Bst[g], Kst[g])) for g in G]
    s1 = [_nt(AR[g], _b(Sbd[g])) for g in G]
    GA = [jnp.where(strict, g1[g][:L], 0.0) for g in G]
    GR = [jnp.where(incl, g1[g][L:], 0.0) for g in G]
    rhs = [s1[g][:L] + _nn(_b(GA[g][:, CW:]), Vst[g]) for g in G]
    U = _unit_lower_solve([GA[g][:, :CW] for g in G], rhs, L, bd)
    Ub = [_b(x) for x in U]
    ys = [s1[g][L:] + _nn(_b(GR[g]), _cat0(bd.heads(Ub[g]), Vst[g])) for g in G]
    for g in G:
        upd = _tn(_cat0(Ub[g], _b(vr[:, gsl[g]])), _cat0(_b(Bd[:, gsl[g]]), _b(Kd[:, gsl[g]])))
        S_ref[g] = Sbd[g] * PL[:, gsl[g]] + jnp.where(seg, upd, 0.0)

    y = jnp.concatenate(ys, axis=1)
    mu = head_sum(y) * (1.0 / HD)
    yz = y - mu
    var = head_sum(yz * yz) * (1.0 / HD)
    yn = yz * lax.rsqrt(var + RWKV_GN_EPS) * lnw_ref[...] + lnb_ref[...]
    bonus = head_sum(rr * kmod * rk_ref[...]) * vr
    y_ref[0] = ((yn + bonus) * _silu(rz)).astype(y_ref.dtype)

    @pl.when(c == pl.num_programs(1) - 1)
    def _():
        for h in range(H):
            g, j = divmod(h, RWKV_PACK)
            S1_ref[0, h] = S_ref[g, j * HD:(j + 1) * HD, j * HD:(j + 1) * HD]


def _rwkv(u, e, sha, shb, S0, p, L):
    B, T, W = u.shape
    H, HD = S0.shape[1], S0.shape[2]
    D = H * HD
    GW = RWKV_PACK * HD
    params = [p["r_mua"], p["r_mub"], p["r_w0"], p["r_w2"], p["r_a0"], p["r_a2"], p["r_kk"], p["r_ka"],
              p["r_rk"], p["r_lnw"], p["r_lnb"]]
    per_b = lambda a: pl.BlockSpec((1,) + a.shape[1:], lambda b, c: (b,) + (0,) * (a.ndim - 1))
    full = lambda a: pl.BlockSpec(a.shape, lambda b, c: (0,) * a.ndim)
    return pl.pallas_call(
        functools.partial(_rwkv_kernel, L=L, H=H, HD=HD),
        grid=(B, T // L),
        in_specs=[pl.BlockSpec((1, L, W), lambda b, c: (b, c, 0)),
                  pl.BlockSpec((1, L, E_W), lambda b, c: (b, c, 0)),
                  per_b(sha), per_b(shb), per_b(S0)] + [full(a) for a in params],
        out_specs=[pl.BlockSpec((1, L, D), lambda b, c: (b, c, 0)), per_b(S0)],
        out_shape=[jax.ShapeDtypeStruct((B, T, D), BF16), jax.ShapeDtypeStruct(S0.shape, F32)],
        scratch_shapes=[pltpu.VMEM((SUBLANE, 3 * D), F32), pltpu.VMEM((SUBLANE, LANE), F32),
                        pltpu.VMEM((H // RWKV_PACK, GW, GW), F32)],
        compiler_params=pltpu.CompilerParams(dimension_semantics=("parallel", "arbitrary"),
                                             vmem_limit_bytes=VMEM_LIMIT_BYTES),
        name="rwkv7",
    )(u, e, sha, shb, S0, *params)


def _bd_tiles(w):
    dep, nb, qb, _ = w.shape
    per = MXU_DIM // qb
    wt = w.reshape(dep, nb // per, per, qb, qb)
    tiles = jnp.einsum("ltncd,nm->ltncmd", wt, jnp.eye(per, dtype=w.dtype))
    return tiles.reshape(dep, nb // per, MXU_DIM, MXU_DIM)


def _pad_lanes(a, width=LANE):
    return jnp.pad(a, [(0, 0)] * (a.ndim - 1) + [(0, width - a.shape[-1])])


def _prep_params(D, w_in, b_gate, m_conv_w, m_conv_b, m_wq, m_wk, m_wv, m_b_if, m_norm_w, m_skip, m_w_out,
                 s_conv_w, s_conv_b, s_dt_bias, s_A_log, s_D, s_norm_w, s_w_out,
                 r_mu, r_w0, r_w2, r_a0, r_a2, r_k_k, r_k_a, r_r_k, r_ln_w, r_ln_b, r_w_out, w_out):
    H_S = D // P_S
    DC = D + 2 * G_S * N_S
    n_lo = LORA_W + LORA_A
    assert n_lo == LANE and 2 * H_M <= SUBLANE and H_S <= LANE
    off_gate = 0
    off_m = off_gate + 3 * D
    off_if = off_m + 3 * D
    off_sz = off_if + 2 * H_M
    off_dt = off_sz + D + DC
    off_rsh = off_dt + H_S
    off_rz = off_rsh + 3 * D + n_lo
    assert off_rz + D == w_in.shape[-1]
    col = lambda o, n: w_in[:, :, o:o + n]
    row = lambda a: a[:, None, :]
    DH = D // H_M
    zl = jnp.zeros_like(r_w2)
    return dict(
        wA=_b(col(off_gate, 3 * D)),
        wB=_b(col(off_m, 3 * D)),
        wC=_b(col(off_sz, D + DC)),
        wD=_b(jnp.concatenate([col(off_rsh, 3 * D), col(off_rz, D)], axis=-1)),
        wE=_b(jnp.concatenate([col(off_rsh + 3 * D, n_lo), _pad_lanes(col(off_if, 2 * H_M)),
                               _pad_lanes(col(off_dt, H_S))], axis=-1)),
        bg=row(b_gate),
        m_cw=m_conv_w, m_cb=row(m_conv_b),
        m_wq=_b(_bd_tiles(m_wq)), m_wk=_b(_bd_tiles(m_wk) * DH ** -0.5), m_wv=_b(_bd_tiles(m_wv)),
        m_bif=row(_pad_lanes(m_b_if)), m_nw=row(m_norm_w), m_skip=row(m_skip), m_wo=_b(m_w_out),
        s_cw=s_conv_w, s_cb=row(s_conv_b), s_dtb=row(_pad_lanes(s_dt_bias)), s_alog=row(_pad_lanes(s_A_log)),
        s_drow=row(jnp.repeat(s_D, P_S, axis=-1)), s_nw=row(s_norm_w), s_wo=_b(s_w_out),
        r_mua=row(r_mu[:, :3 * D]), r_mub=row(r_mu[:, 3 * D:]),
        r_w0=row(r_w0), r_w2=jnp.concatenate([r_w2, zl], axis=1),
        r_a0=row(r_a0), r_a2=jnp.concatenate([zl, r_a2], axis=1),
        r_kk=row(r_k_k), r_ka=row(r_k_a), r_rk=row(r_r_k.reshape(r_r_k.shape[0], -1)),
        r_lnw=row(r_ln_w), r_lnb=row(r_ln_b), r_wo=_b(r_w_out),
        wo=_b(w_out),
    )


def _row_tile(n, cap):
    t = min(n, cap)
    while n % t:
        t //= 2
    return t


def _run(x, states, prep, norm_w, final_norm_w):
    B, T, D = x.shape
    depth = norm_w.shape[0]
    mC, mn, mm, mconv, sS, sconv, rS, rsh = states
    L = CHUNK if T % CHUNK == 0 else T
    KW = mconv.shape[2] + 1
    assert T >= KW - 1 and L % SUBLANE == 0
    N = B * T
    tm = _row_tile(N, 1024)
    tmo = _row_tile(N, 256)
    x2 = x.reshape(N, D)
    h = _rmsnorm(x2, norm_w[0][None, :], BF16, tm)
    new = [[] for _ in range(8)]
    for l in range(depth):
        p = {k: v[l] for k, v in prep.items()}
        proj = lambda w, dt: _matmul(h, w, tm, 1024 if w.shape[1] % 1024 == 0 else w.shape[1], dt)
        uA = proj(p["wA"], BF16)
        uB = proj(p["wB"], BF16).reshape(B, T, -1)
        uC = proj(p["wC"], BF16).reshape(B, T, -1)
        uD = proj(p["wD"], BF16).reshape(B, T, -1)
        uE = proj(p["wE"], F32).reshape(B, T, -1)
        hm, mC1, mn1, mm1 = _mlstm(uB, uE, mconv[l], mC[l], mn[l], mm[l], p, L)
        ys, sS1 = _ssd(uC, uE, sconv[l], sS[l], p, L)
        yr, rS1 = _rwkv(uD, uE, rsh[l][..., :3 * D], rsh[l][..., 3 * D:], rS[l], p, L)
        last = l == depth - 1
        nw = (final_norm_w if last else norm_w[l + 1])[None, :]
        x2, h = _out_proj(hm.reshape(N, D), ys.reshape(N, D), yr.reshape(N, D), uA, x2,
                          p["m_wo"], p["s_wo"], p["r_wo"], p["wo"], p["bg"], nw, F32 if last else BF16, tmo)
        outs = (mC1, mn1, mm1, uB[:, T - (KW - 1):, 0:D].astype(F32), sS1, uC[:, T - (KW - 1):, D:].astype(F32), rS1,
                jnp.concatenate([uD[:, T - 1:, 0:3 * D].astype(F32), uE[:, T - 1:, E_LO:E_LO + LANE]], axis=-1))
        for acc, s in zip(new, outs):
            acc.append(s)
    return h.reshape(B, T, D), [jnp.stack(s) for s in new]


def kernel(x_prompt, x_sample, state_mlstm_C, state_mlstm_n, state_mlstm_m, state_mlstm_conv, state_ssd,
           state_ssd_conv, state_rwkv, state_rwkv_shift, norm_w, w_in, b_gate, m_conv_w, m_conv_b, m_wq, m_wk,
           m_wv, m_b_if, m_norm_w, m_skip, m_w_out, s_conv_w, s_conv_b, s_dt_bias, s_A_log, s_D, s_norm_w,
           s_w_out, r_mu, r_w0, r_w2, r_a0, r_a2, r_k_k, r_k_a, r_r_k, r_ln_w, r_ln_b, r_w_out, w_out,
           final_norm_w):
    D = x_prompt.shape[-1]
    prep = _prep_params(D, w_in, b_gate, m_conv_w, m_conv_b, m_wq, m_wk, m_wv, m_b_if, m_norm_w, m_skip, m_w_out,
                        s_conv_w, s_conv_b, s_dt_bias, s_A_log, s_D, s_norm_w, s_w_out,
                        r_mu, r_w0, r_w2, r_a0, r_a2, r_k_k, r_k_a, r_r_k, r_ln_w, r_ln_b, r_w_out, w_out)
    sample_states = (state_mlstm_C, state_mlstm_n, state_mlstm_m, state_mlstm_conv, state_ssd,
                     state_ssd_conv, state_rwkv, state_rwkv_shift)
    Bp = x_prompt.shape[0]
    zero_states = tuple(jnp.zeros(s.shape[:1] + (Bp,) + s.shape[2:], s.dtype) for s in sample_states)
    y_p, st_p = _run(x_prompt, zero_states, prep, norm_w, final_norm_w)
    y_s, st_s = _run(x_sample, sample_states, prep, norm_w, final_norm_w)
    return (y_p, y_s, *st_p, *st_s)
```

```python
import functools

import jax
import jax.numpy as jnp
from jax import lax
from jax.experimental import pallas as pl
from jax.experimental.pallas import tpu as pltpu

F32 = jnp.float32
BF16 = jnp.bfloat16
HI = lax.Precision.HIGHEST

CHUNK = 64
NORM_EPS = 1e-6
H_M = 4
MLSTM_LN_EPS = 1e-5
P_S = 64
G_S = 4
N_S = 128
SSD_GN_EPS = 1e-5
HD_R = 64
LORA_W = 64
LORA_A = 64
RWKV_GN_EPS = 64e-5
RWKV_PACK = 4
SOLVE_BASE = 16
MIXER_CHUNKS_PER_STEP = 4

LANE = 128
SUBLANE = 8
MXU_DIM = 256
VMEM_LIMIT_BYTES = 48 * 1024 * 1024

E_LO = 0 * LANE
E_IF = 1 * LANE
E_DT = 2 * LANE
E_W = 3 * LANE


def _nn(a, b, prec=None):
    return jnp.dot(a, b, precision=prec, preferred_element_type=F32)


def _nt(a, b, prec=None):
    return lax.dot_general(a, b, (((1,), (1,)), ((), ())), precision=prec, preferred_element_type=F32)


def _tn(a, b, prec=None):
    return lax.dot_general(a, b, (((0,), (0,)), ((), ())), precision=prec, preferred_element_type=F32)


def _b(x):
    return x.astype(BF16)


def _bf16_terms(x, n):
    terms, r = [], x
    for _ in range(n):
        t = _b(r)
        terms.append(t)
        r = r - t.astype(F32)
    return terms


def _split(x):
    return tuple(_bf16_terms(x, 2))


def _cat0(*xs):
    return jnp.concatenate(xs, axis=0)


def _cat1(*xs):
    return jnp.concatenate(xs, axis=1)


def _softplus(x):
    return jnp.maximum(x, 0.0) + jnp.log1p(jnp.exp(-jnp.abs(x)))


def _log_sigmoid(x):
    return -_softplus(-x)


def _silu(x):
    return x * jax.nn.sigmoid(x)


def _iota2(shape, dim):
    return lax.broadcasted_iota(jnp.int32, shape, dim)


def _tri_consts(L):
    r, c = _iota2((L, L), 0), _iota2((L, L), 1)
    causal = c <= r
    return causal, causal.astype(F32), (r <= c).astype(F32)


def _eye(n):
    return (_iota2((n, n), 0) == _iota2((n, n), 1)).astype(F32)


def _causal_conv(xp_ref, x, cw_ref, cb_ref, L, KW):
    xfull = _cat0(xp_ref[...], x)
    y = cb_ref[...]
    for j in range(KW):
        s = KW - 1 - j
        xs = x if s == 0 else pltpu.roll(xfull, s, 0)[SUBLANE:SUBLANE + L, :]
        y = y + xs * cw_ref[j:j + 1, :]
    xp_ref[...] = x[L - SUBLANE:L, :]
    return y


def _norm_kernel(x_ref, w_ref, o_ref):
    x = x_ref[...]
    y = x * lax.rsqrt(jnp.mean(x * x, axis=-1, keepdims=True) + NORM_EPS)
    o_ref[...] = (y * w_ref[...]).astype(o_ref.dtype)


def _rmsnorm(x, w, out_dtype, tm):
    n, d = x.shape
    return pl.pallas_call(
        _norm_kernel,
        grid=(n // tm,),
        in_specs=[pl.BlockSpec((tm, d), lambda i: (i, 0)), pl.BlockSpec((1, d), lambda i: (0, 0))],
        out_specs=pl.BlockSpec((tm, d), lambda i: (i, 0)),
        out_shape=jax.ShapeDtypeStruct((n, d), out_dtype),
        compiler_params=pltpu.CompilerParams(dimension_semantics=("parallel",), vmem_limit_bytes=VMEM_LIMIT_BYTES),
        name="rmsnorm",
    )(x, w)


def _matmul_kernel(h_ref, w_ref, o_ref):
    o_ref[...] = _nn(h_ref[...], w_ref[...]).astype(o_ref.dtype)


def _matmul(h, w, layer, tm, tn, out_dtype):
    n, d = h.shape
    nw = w.shape[2]
    return pl.pallas_call(
        _matmul_kernel,
        grid=(nw // tn, n // tm),
        in_specs=[pl.BlockSpec((tm, d), lambda j, i: (i, 0)),
                  pl.BlockSpec((None, d, tn), lambda j, i: (layer, 0, j))],
        out_specs=pl.BlockSpec((tm, tn), lambda j, i: (i, j)),
        out_shape=jax.ShapeDtypeStruct((n, nw), out_dtype),
        compiler_params=pltpu.CompilerParams(dimension_semantics=("parallel", "parallel"),
                                             vmem_limit_bytes=VMEM_LIMIT_BYTES),
        name="proj_in",
    )(h, w)


def _out_kernel(hm_ref, ys_ref, yr_ref, g_ref, x_ref, wm_ref, ws_ref, wr_ref, wo_ref, bg_ref, nw_ref,
                xo_ref, ho_ref, *, D):
    g = jax.nn.sigmoid(g_ref[...].astype(F32) + bg_ref[...])
    merged = (g[:, 0:D] * _nn(hm_ref[...], wm_ref[...])
              + g[:, D:2 * D] * _nn(ys_ref[...], ws_ref[...])
              + g[:, 2 * D:3 * D] * _nn(yr_ref[...], wr_ref[...]))
    out = x_ref[...] + _nn(_b(merged), wo_ref[...])
    xo_ref[...] = out
    y = out * lax.rsqrt(jnp.mean(out * out, axis=-1, keepdims=True) + NORM_EPS)
    ho_ref[...] = (y * nw_ref[...]).astype(ho_ref.dtype)


def _out_proj(hm, ys, yr, gate, x, wm, ws, wr, wo, layer, bg, nw, h_dtype, tm):
    n, d = x.shape
    row = lambda w: pl.BlockSpec((tm, w), lambda i: (i, 0))
    full = lambda a: pl.BlockSpec(a.shape, lambda i: (0, 0))
    wspec = pl.BlockSpec((None, d, d), lambda i: (layer, 0, 0))
    return pl.pallas_call(
        functools.partial(_out_kernel, D=d),
        grid=(n // tm,),
        in_specs=[row(d), row(d), row(d), row(3 * d), row(d),
                  wspec, wspec, wspec, wspec, full(bg), full(nw)],
        out_specs=[row(d), row(d)],
        out_shape=[jax.ShapeDtypeStruct((n, d), F32), jax.ShapeDtypeStruct((n, d), h_dtype)],
        compiler_params=pltpu.CompilerParams(dimension_semantics=("parallel",), vmem_limit_bytes=VMEM_LIMIT_BYTES),
        name="out_proj",
    )(hm, ys, yr, gate, x, wm, ws, wr, wo, bg, nw)


def _blockdiag_apply(a, w_ref):
    ab = _b(a)
    return [_nn(ab[:, t * MXU_DIM:(t + 1) * MXU_DIM], w_ref[t]) for t in range(w_ref.shape[0])]


def _mlstm_part(phase, rows, u_ref, e_ref, conv0_ref, C0_ref, n0_ref, m0_ref,
                cw_ref, cb_ref, wq_ref, wk_ref, wv_ref, bif_ref, nw_ref, skip_ref,
                h_ref, C1_ref, n1_ref, m1_ref,
                xp_ref, C_ref, n_ref, m_ref, *, L, H, DH, KW):
    D = H * DH
    if phase == "init":
        xp_ref[...] = jnp.zeros(xp_ref.shape, F32)
        xp_ref[SUBLANE - (KW - 1):SUBLANE, :] = conv0_ref[0]
        C_ref[...] = C0_ref[0]
        n_ref[...] = n0_ref[0]
        m_ref[...] = m0_ref[0]
        return
    if phase == "final":
        C1_ref[0] = C_ref[...]
        n1_ref[0] = n_ref[...]
        m1_ref[0] = m_ref[...]
        return

    u = u_ref[0, rows, :].astype(F32)
    mx, mz, mo = u[:, 0:D], u[:, D:2 * D], u[:, 2 * D:3 * D]
    mc = _silu(_causal_conv(xp_ref, mx, cw_ref, cb_ref, L, KW))
    q = _cat1(*_blockdiag_apply(mc, wq_ref))
    k = _cat1(*_blockdiag_apply(mc, wk_ref))
    v = _cat1(*_blockdiag_apply(mx, wv_ref))

    causal, tril, triu = _tri_consts(L)
    ecol = e_ref[0, rows, E_IF:E_IF + LANE] + bif_ref[...]
    eT = _nt(_eye(LANE), ecol, HI)[0:SUBLANE, :]
    b_col = _nn(tril, _log_sigmoid(ecol), HI)
    b_row = _nn(_log_sigmoid(eT), triu, HI)

    Hs = range(H)
    sl = [slice(h * DH, (h + 1) * DH) for h in Hs]
    qf, kf = [q[:, s] for s in sl], [k[:, s] for s in sl]
    qh, kh, vh = [_b(x) for x in qf], [_b(x) for x in kf], [_b(v[:, s]) for s in sl]
    bc, br = [b_col[:, H + h:H + h + 1] for h in Hs], [b_row[H + h:H + h + 1, :] for h in Hs]
    ic, ir = [ecol[:, h:h + 1] for h in Hs], [eT[h:h + 1, :] for h in Hs]
    mprev = [m_ref[h:h + 1, 0:1] for h in Hs]
    Ch, nh = [C_ref[h] for h in Hs], [n_ref[h:h + 1, :] for h in Hs]

    qk = [_nt(qh[h], kh[h]) for h in Hs]
    qC = [_nn(qh[h], _b(Ch[h])) for h in Hs]
    logD = [jnp.where(causal, bc[h] - br[h] + ir[h], -jnp.inf) for h in Hs]
    inter = [bc[h] + mprev[h] for h in Hs]
    m_t = [jnp.maximum(inter[h], jnp.max(logD[h], axis=1, keepdims=True)) for h in Hs]
    sc = [jnp.exp(inter[h] - m_t[h]) for h in Hs]
    s = [qk[h] * jnp.exp(logD[h] - m_t[h]) for h in Hs]
    sv = [_nn(_b(s[h]), vh[h]) for h in Hs]

    bL = [bc[h][L - 1:L, :] for h in Hs]
    m_new = [jnp.maximum(bL[h] + mprev[h], jnp.max(bL[h] - br[h] + ir[h], axis=1, keepdims=True)) for h in Hs]
    dec = [jnp.exp(bL[h] + mprev[h] - m_new[h]) for h in Hs]
    kw = [kf[h] * jnp.exp(bL[h] - bc[h] + ic[h] - m_new[h]) for h in Hs]
    upd = [_tn(_b(kw[h]), vh[h]) for h in Hs]

    outs = []
    for h in Hs:
        den = jnp.sum(s[h], axis=1, keepdims=True) + sc[h] * jnp.sum(qf[h] * nh[h], axis=1, keepdims=True)
        hc = (sv[h] + qC[h] * sc[h]) / jnp.maximum(jnp.abs(den), jnp.exp(-m_t[h]))
        mu = jnp.mean(hc, axis=1, keepdims=True)
        hz = hc - mu
        var = jnp.mean(hz * hz, axis=1, keepdims=True)
        outs.append(hz * lax.rsqrt(var + MLSTM_LN_EPS))
        C_ref[h] = dec[h] * Ch[h] + upd[h]
        n_ref[h:h + 1, :] = dec[h] * nh[h] + jnp.sum(kw[h], axis=0, keepdims=True)
        m_ref[h:h + 1, :] = jnp.broadcast_to(m_new[h], (1, LANE))

    hm = jnp.concatenate(outs, axis=1) * nw_ref[...]
    out = (jax.nn.sigmoid(mo) * hm + skip_ref[...] * mc) * _silu(mz)
    h_ref[0, rows, :] = out.astype(h_ref.dtype)


def _per_b(a):
    return pl.BlockSpec((1,) + a.shape[1:], lambda b, c: (b,) + (0,) * (a.ndim - 1))


def _full(a):
    return pl.BlockSpec(a.shape, lambda b, c: (0,) * a.ndim)


def _time_block(tb, w):
    return pl.BlockSpec((1, tb, w), lambda b, c: (b, c, 0))


def _mlstm_spec(u, e, conv0, C0, n0, m0, p, L, TB):
    B, T, W = u.shape
    H, DH = C0.shape[1], C0.shape[2]
    D = H * DH
    KW = p["m_cw"].shape[0]
    m0b = jnp.broadcast_to(m0[..., None], (B, H, LANE))
    params = [p["m_cw"], p["m_cb"], p["m_wq"], p["m_wk"], p["m_wv"], p["m_bif"], p["m_nw"], p["m_skip"]]
    return dict(
        body=functools.partial(_mlstm_part, L=L, H=H, DH=DH, KW=KW),
        inputs=[u, e, conv0, C0, n0, m0b] + params,
        in_specs=[_time_block(TB, W), _time_block(TB, E_W), _per_b(conv0), _per_b(C0), _per_b(n0), _per_b(m0b)]
        + [_full(a) for a in params],
        out_specs=[_time_block(TB, D), _per_b(C0), _per_b(n0), _per_b(m0b)],
        out_shape=[jax.ShapeDtypeStruct((B, T, D), BF16), jax.ShapeDtypeStruct(C0.shape, F32),
                   jax.ShapeDtypeStruct(n0.shape, F32), jax.ShapeDtypeStruct(m0b.shape, F32)],
        scratch=[pltpu.VMEM((SUBLANE, D), F32), pltpu.VMEM((H, DH, DH), F32),
                 pltpu.VMEM((H, DH), F32), pltpu.VMEM((H, LANE), F32)],
    )


def _ssd_part(phase, rows, u_ref, e_ref, conv0_ref, S0_ref, cw_ref, cb_ref, dtb_ref, alog_ref, drow_ref, nw_ref,
              xl_ref, xp_exp_ref, y_ref, S1_ref, xp_ref, S_ref, *, L, H, P, G, N, KW):
    D = H * P
    E = H // G
    EL, EP = E * L, E * P
    if phase == "init":
        xp_ref[...] = jnp.zeros(xp_ref.shape, F32)
        xp_ref[SUBLANE - (KW - 1):SUBLANE, :] = conv0_ref[0]
        S_ref[...] = S0_ref[0].reshape(G, EP, N)
        return
    if phase == "final":
        S1_ref[0] = S_ref[...].reshape(H, P, N)
        return

    u = u_ref[0, rows, :].astype(F32)
    sz = u[:, 0:D]
    xbc = _silu(_causal_conv(xp_ref, u[:, D:], cw_ref, cb_ref, L, KW))
    xs, Bm, Cm = xbc[:, 0:D], xbc[:, D:D + G * N], xbc[:, D + G * N:]

    dt = _softplus(e_ref[0, rows, E_DT:E_DT + LANE] + dtb_ref[...])
    tril = _tri_consts(L)[1]
    cs = _nn(tril, dt * (-jnp.exp(alog_ref[...])), HI)
    csL = cs[L - 1:L, :]

    def expand(items, e_ref_):
        rows = [t for x, n in items for t in _bf16_terms(x, n)]
        r = _nn(_cat0(*rows), e_ref_[...])
        outs, k = [], 0
        for _, n in items:
            outs.append(sum(r[j * L:(j + 1) * L] for j in range(k, k + n)))
            k += n
        return outs

    csx, dtx = expand([(cs, 3), (dt, 2)], xl_ref)
    f_out, f_in = expand([(jnp.exp(cs), 2), (jnp.exp(csL - cs) * dt, 2)], xp_exp_ref)
    row, col = _iota2((L, H * L), 0), _iota2((L, H * L), 1) % L
    diag = row == col
    cs_src = jnp.sum(jnp.where(diag, csx, 0.0), axis=0, keepdims=True)
    dt_src = jnp.sum(jnp.where(diag, dtx, 0.0), axis=0, keepdims=True)
    Mall = jnp.exp(jnp.where(col <= row, csx - cs_src, -jnp.inf)) * dt_src
    decay_end = jnp.exp(csL)
    xw = xs * f_in
    bd = _BlockDiag(L, P, E)

    Gs = range(G)
    Bg = [_b(Bm[:, g * N:(g + 1) * N]) for g in Gs]
    Cg = [_b(Cm[:, g * N:(g + 1) * N]) for g in Gs]
    Sg = [S_ref[g] for g in Gs]
    CB = [_nt(Cg[g], _cat0(*[Bg[g]] * E)) for g in Gs]
    CS = [_nt(Cg[g], _b(Sg[g])) for g in Gs]
    Mg = [_b(CB[g] * Mall[:, g * EL:(g + 1) * EL]) for g in Gs]
    yi = [_nn(Mg[g], bd.heads(_b(xs[:, g * EP:(g + 1) * EP]))) for g in Gs]
    upd = [_tn(_b(xw[:, g * EP:(g + 1) * EP]), Bg[g]) for g in Gs]
    for g in Gs:
        scale = _cat0(*[jnp.broadcast_to(decay_end[:, h:h + 1], (P, 1)) for h in range(g * E, (g + 1) * E)])
        S_ref[g] = scale * Sg[g] + upd[g]

    yv = (_cat1(*yi) + _cat1(*CS) * f_out + drow_ref[...] * xs) * _silu(sz)
    DG = D // G
    parts = []
    for g in range(G):
        seg = yv[:, g * DG:(g + 1) * DG]
        parts.append(seg * lax.rsqrt(jnp.mean(seg * seg, axis=1, keepdims=True) + SSD_GN_EPS))
    y_ref[0, rows, :] = (jnp.concatenate(parts, axis=1) * nw_ref[...]).astype(y_ref.dtype)


def _lane_expander(n_heads, width):
    return (jnp.arange(LANE)[:, None] == jnp.arange(n_heads * width)[None, :] // width).astype(BF16)


def _ssd_spec(u, e, conv0, S0, p, L, TB):
    B, T, W = u.shape
    H, P, N = S0.shape[1:]
    D = H * P
    DC = conv0.shape[-1]
    G = (DC - D) // (2 * N)
    KW = p["s_cw"].shape[0]
    params = [p["s_cw"], p["s_cb"], p["s_dtb"], p["s_alog"], p["s_drow"], p["s_nw"],
              _lane_expander(H, L), _lane_expander(H, P)]
    return dict(
        body=functools.partial(_ssd_part, L=L, H=H, P=P, G=G, N=N, KW=KW),
        inputs=[u, e, conv0, S0] + params,
        in_specs=[_time_block(TB, W), _time_block(TB, E_W), _per_b(conv0), _per_b(S0)] + [_full(a) for a in params],
        out_specs=[_time_block(TB, D), _per_b(S0)],
        out_shape=[jax.ShapeDtypeStruct((B, T, D), BF16), jax.ShapeDtypeStruct(S0.shape, F32)],
        scratch=[pltpu.VMEM((SUBLANE, DC), F32), pltpu.VMEM((G, H // G * P, N), F32)],
    )


def _dot_split(dot, x, rhs_hi, rhs_lo):
    xh, xl = _split(x)
    m = x.shape[0]
    r = dot(_cat0(xh, xl), rhs_hi)
    return r[:m] + r[m:] + dot(xh, rhs_lo)


class _BlockDiag:
    def __init__(self, L, HD, n):
        self.hm = [((_iota2((L, n * HD), 1) // HD) == h).astype(BF16) for h in range(n)]
        self.cm = [((_iota2((L, n * L), 1) // L) == h).astype(BF16) for h in range(n)]

    def heads(self, xb):
        return _cat0(*[xb * m for m in self.hm])

    def cols(self, xb):
        return _cat0(*[xb * m for m in self.cm])

    def heads2(self, x):
        xh, xl = _split(x)
        return self.heads(xh), self.heads(xl)

    def cols2(self, x):
        xh, xl = _split(x)
        return self.cols(xh), self.cols(xl)


def _unit_lower_solve(As, rhss, L, bd):
    base = min(SOLVE_BASE, L)
    W = As[0].shape[1]
    t, i = _iota2(As[0].shape, 0), _iota2(As[0].shape, 1) % L
    same = (t // base) == (i // base)
    eye = (t == i).astype(F32)
    mmc = lambda x, y: _nn(_b(x), bd.cols(_b(y)))
    mmv = lambda x, v: _nn(_b(x), bd.heads(_b(v)))
    Ads = [jnp.where(same, A, 0.0) for A in As]
    Tks = [eye + Ad for Ad in Ads]
    Pks = [mmc(Ad, Ad) for Ad in Ads]
    n = 2
    while n < base:
        if 2 * n < base:
            rs = [_nn(_b(Pk), _cat1(bd.cols(_b(Tk)), bd.cols(_b(Pk)))) for Tk, Pk in zip(Tks, Pks)]
            Tks = [Tk + r[:, :W] for Tk, r in zip(Tks, rs)]
            Pks = [r[:, W:] for r in rs]
        else:
            Tks = [Tk + mmc(Pk, Tk) for Tk, Pk in zip(Tks, Pks)]
        n *= 2
    nb = L // base
    if nb == 1:
        return [mmv(Tk, V) for Tk, V in zip(Tks, rhss)]
    HW = rhss[0].shape[1]
    rs = [_nn(_b(Tk), _cat1(bd.heads(_b(V)), bd.cols(_b(jnp.where(same, 0.0, A)))))
          for Tk, V, A in zip(Tks, rhss, As)]
    Xs, Ms = [r[:, :HW] for r in rs], [r[:, HW:] for r in rs]
    n = 1
    while n < nb:
        if 2 * n < nb:
            rs = [_nn(_b(M), _cat1(bd.heads(_b(X)), bd.cols(_b(M)))) for X, M in zip(Xs, Ms)]
            Xs = [X + r[:, :HW] for X, r in zip(Xs, rs)]
            Ms = [r[:, HW:] for r in rs]
        else:
            Xs = [X + mmv(M, X) for X, M in zip(Xs, Ms)]
        n *= 2
    return Xs


def _rwkv_part(phase, rows, u_ref, e_ref, sha_ref, shb_ref, S0_ref,
               mua_ref, mub_ref, w0_ref, w2_ref, a0_ref, a2_ref, kk_ref, ka_ref, rk_ref, lnw_ref, lnb_ref,
               y_ref, S1_ref, xpa_ref, xpb_ref, S_ref, *, L, H, HD):
    D = H * HD
    GW = RWKV_PACK * HD
    NG = H // RWKV_PACK
    CW = RWKV_PACK * L
    if phase == "init":
        xpa_ref[...] = jnp.zeros(xpa_ref.shape, F32)
        xpb_ref[...] = jnp.zeros(xpb_ref.shape, F32)
        xpa_ref[SUBLANE - 1:SUBLANE, :] = sha_ref[0]
        xpb_ref[SUBLANE - 1:SUBLANE, :] = shb_ref[0]
        S_ref[...] = jnp.zeros(S_ref.shape, F32)
        for h in range(H):
            g, j = divmod(h, RWKV_PACK)
            S_ref[g, j * HD:(j + 1) * HD, j * HD:(j + 1) * HD] = S0_ref[0, h]
        return
    if phase == "final":
        for h in range(H):
            g, j = divmod(h, RWKV_PACK)
            S1_ref[0, h] = S_ref[g, j * HD:(j + 1) * HD, j * HD:(j + 1) * HD]
        return

    def shift_mix(xp_ref, cur, mu_ref):
        prev = pltpu.roll(_cat0(xp_ref[...], cur), 1, 0)[SUBLANE:SUBLANE + L, :]
        xp_ref[...] = cur[L - SUBLANE:L, :]
        return cur + (prev - cur) * mu_ref[...]

    u = u_ref[0, rows, :].astype(F32)
    rx = shift_mix(xpa_ref, u[:, 0:3 * D], mua_ref)
    lo = shift_mix(xpb_ref, e_ref[0, rows, E_LO:E_LO + LANE], mub_ref)
    rz = u[:, 3 * D:4 * D]
    rr, kr, vr = rx[:, 0:D], rx[:, D:2 * D], rx[:, 2 * D:3 * D]

    seg = ((_iota2((GW, GW), 0) // HD) == (_iota2((GW, GW), 1) // HD))
    segb = seg.astype(BF16)

    def head_sum(x):
        xh, xl = _split(x)
        rows = [t[:, g * GW:(g + 1) * GW] for t in (xh, xl) for g in range(NG)]
        r = _nn(_cat0(*rows), segb)
        return _cat1(*[r[g * L:(g + 1) * L] + r[(NG + g) * L:(NG + g + 1) * L] for g in range(NG)])

    w_log = -_softplus(-(w0_ref[...] + _dot_split(_nn, jnp.tanh(lo), *_split(w2_ref[...])))) - 0.5
    lw = -jnp.exp(w_log)
    a_sig = jax.nn.sigmoid(a0_ref[...] + _nn(_b(lo), _b(a2_ref[...])))
    kk = kr * kk_ref[...]
    kk = kk / jnp.maximum(jnp.sqrt(head_sum(kk * kk)), 1e-12)
    kmod = kr * (1.0 + (a_sig - 1.0) * ka_ref[...])
    kb = kk * a_sig

    trib = _tri_consts(L)[1].astype(BF16)
    l1, l2 = _split(lw)
    l3 = _b(lw - l1.astype(F32) - l2.astype(F32))
    cum = _nn(trib, l1) + _nn(trib, l2) + _nn(trib, l3)
    cumL = cum[L - 1:L, :]
    p_in, p_inv, p_end = jnp.exp(cum), jnp.exp(-cum), jnp.exp(cumL - cum)
    At = -kk * jnp.exp(cum - lw)
    Bt, Kt, Rt = kb * p_inv, kmod * p_inv, rr * p_in
    Bd, Kd = kb * p_end, kmod * p_end
    PL = jnp.exp(cumL)

    t, i = _iota2((L, 2 * CW), 0), _iota2((L, 2 * CW), 1) % L
    strict, incl = i < t, i <= t
    bd = _BlockDiag(L, HD, RWKV_PACK)

    G = range(NG)
    gsl = [slice(g * GW, (g + 1) * GW) for g in G]
    Sbd = [S_ref[g] for g in G]
    AR = [_cat0(_b(At[:, s]), _b(Rt[:, s])) for s in gsl]
    Bst = [bd.heads(_b(Bt[:, s])) for s in gsl]
    Kst = [bd.heads(_b(Kt[:, s])) for s in gsl]
    Vst = [bd.heads(_b(vr[:, s])) for s in gsl]
    g1 = [_nt(AR[g], _cat0(Bst[g], Kst[g])) for g in G]
    s1 = [_nt(AR[g], _b(Sbd[g])) for g in G]
    GA = [jnp.where(strict, g1[g][:L], 0.0) for g in G]
    GR = [jnp.where(incl, g1[g][L:], 0.0) for g in G]
    rhs = [s1[g][:L] + _nn(_b(GA[g][:, CW:]), Vst[g]) for g in G]
    U = _unit_lower_solve([GA[g][:, :CW] for g in G], rhs, L, bd)
    Ub = [_b(x) for x in U]
    ys = [s1[g][L:] + _nn(_b(GR[g]), _cat0(bd.heads(Ub[g]), Vst[g])) for g in G]
    for g in G:
        upd = _tn(_cat0(Ub[g], _b(vr[:, gsl[g]])), _cat0(_b(Bd[:, gsl[g]]), _b(Kd[:, gsl[g]])))
        S_ref[g] = Sbd[g] * PL[:, gsl[g]] + jnp.where(seg, upd, 0.0)

    y = jnp.concatenate(ys, axis=1)
    mu = head_sum(y) * (1.0 / HD)
    yz = y - mu
    var = head_sum(yz * yz) * (1.0 / HD)
    yn = yz * lax.rsqrt(var + RWKV_GN_EPS) * lnw_ref[...] + lnb_ref[...]
    bonus = head_sum(rr * kmod * rk_ref[...]) * vr
    y_ref[0, rows, :] = ((yn + bonus) * _silu(rz)).astype(y_ref.dtype)


def _rwkv_spec(u, e, sha, shb, S0, p, L, TB):
    B, T, W = u.shape
    H, HD = S0.shape[1], S0.shape[2]
    D = H * HD
    GW = RWKV_PACK * HD
    params = [p["r_mua"], p["r_mub"], p["r_w0"], p["r_w2"], p["r_a0"], p["r_a2"], p["r_kk"], p["r_ka"],
              p["r_rk"], p["r_lnw"], p["r_lnb"]]
    return dict(
        body=functools.partial(_rwkv_part, L=L, H=H, HD=HD),
        inputs=[u, e, sha, shb, S0] + params,
        in_specs=[_time_block(TB, W), _time_block(TB, E_W), _per_b(sha), _per_b(shb), _per_b(S0)]
        + [_full(a) for a in params],
        out_specs=[_time_block(TB, D), _per_b(S0)],
        out_shape=[jax.ShapeDtypeStruct((B, T, D), BF16), jax.ShapeDtypeStruct(S0.shape, F32)],
        scratch=[pltpu.VMEM((SUBLANE, 3 * D), F32), pltpu.VMEM((SUBLANE, LANE), F32),
                 pltpu.VMEM((H // RWKV_PACK, GW, GW), F32)],
    )


def _mixers_kernel(*refs, parts, chunks, L):
    n_in, n_out = sum(p[1] for p in parts), sum(p[2] for p in parts)
    groups, i, o, s = [], 0, n_in, n_in + n_out
    for body, ni, no, ns in parts:
        groups.append((body, refs[i:i + ni] + refs[o:o + no] + refs[s:s + ns]))
        i, o, s = i + ni, o + no, s + ns
    c = pl.program_id(1)

    @pl.when(c == 0)
    def _():
        for body, r in groups:
            body("init", None, *r)

    for k in range(chunks):
        for body, r in groups:
            body("main", slice(k * L, (k + 1) * L), *r)

    @pl.when(c == pl.num_programs(1) - 1)
    def _():
        for body, r in groups:
            body("final", None, *r)


def _mixers(specs, B, T, L, TB):
    parts = tuple((s["body"], len(s["inputs"]), len(s["out_shape"]), len(s["scratch"])) for s in specs)
    outs = pl.pallas_call(
        functools.partial(_mixers_kernel, parts=parts, chunks=TB // L, L=L),
        grid=(B, T // TB),
        in_specs=[x for s in specs for x in s["in_specs"]],
        out_specs=[x for s in specs for x in s["out_specs"]],
        out_shape=[x for s in specs for x in s["out_shape"]],
        scratch_shapes=[x for s in specs for x in s["scratch"]],
        compiler_params=pltpu.CompilerParams(dimension_semantics=("parallel", "arbitrary"),
                                             vmem_limit_bytes=VMEM_LIMIT_BYTES),
        name="mixers",
    )(*[x for s in specs for x in s["inputs"]])
    res, k = [], 0
    for s in specs:
        res.append(outs[k:k + len(s["out_shape"])])
        k += len(s["out_shape"])
    return res


def _bd_tiles(w):
    dep, nb, qb, _ = w.shape
    rows = w.reshape(dep, nb * qb // MXU_DIM, MXU_DIM, qb)
    idx = jnp.arange(MXU_DIM)
    full = jnp.take(rows, idx % qb, axis=-1)
    return jnp.where((idx[:, None] // qb) == (idx[None, :] // qb), full, 0.0)


def _pad_lanes(a, width=LANE):
    return jnp.pad(a, [(0, 0)] * (a.ndim - 1) + [(0, width - a.shape[-1])])


def _prep_params(D, w_in, b_gate, m_conv_w, m_conv_b, m_wq, m_wk, m_wv, m_b_if, m_norm_w, m_skip, m_w_out,
                 s_conv_w, s_conv_b, s_dt_bias, s_A_log, s_D, s_norm_w, s_w_out,
                 r_mu, r_w0, r_w2, r_a0, r_a2, r_k_k, r_k_a, r_r_k, r_ln_w, r_ln_b, r_w_out, w_out):
    H_S = D // P_S
    DC = D + 2 * G_S * N_S
    n_lo = LORA_W + LORA_A
    assert n_lo == LANE and 2 * H_M <= SUBLANE and H_S <= LANE
    off_gate = 0
    off_m = off_gate + 3 * D
    off_if = off_m + 3 * D
    off_sz = off_if + 2 * H_M
    off_dt = off_sz + D + DC
    off_rsh = off_dt + H_S
    off_rz = off_rsh + 3 * D + n_lo
    assert off_rz + D == w_in.shape[-1]
    col = lambda o, n: w_in[:, :, o:o + n]
    row = lambda a: a[:, None, :]
    DH = D // H_M
    zl = jnp.zeros_like(r_w2)
    return dict(
        wA=_b(col(off_gate, 3 * D)),
        wB=_b(col(off_m, 3 * D)),
        wC=_b(col(off_sz, D + DC)),
        wD=_b(jnp.concatenate([col(off_rsh, 3 * D), col(off_rz, D)], axis=-1)),
        wE=_b(jnp.concatenate([col(off_rsh + 3 * D, n_lo), _pad_lanes(col(off_if, 2 * H_M)),
                               _pad_lanes(col(off_dt, H_S))], axis=-1)),
        bg=row(b_gate),
        m_cw=m_conv_w, m_cb=row(m_conv_b),
        m_wq=_b(_bd_tiles(m_wq)), m_wk=_b(_bd_tiles(m_wk) * DH ** -0.5), m_wv=_b(_bd_tiles(m_wv)),
        m_bif=row(_pad_lanes(m_b_if)), m_nw=row(m_norm_w), m_skip=row(m_skip), m_wo=_b(m_w_out),
        s_cw=s_conv_w, s_cb=row(s_conv_b), s_dtb=row(_pad_lanes(s_dt_bias)), s_alog=row(_pad_lanes(s_A_log)),
        s_drow=row(jnp.repeat(s_D, P_S, axis=-1)), s_nw=row(s_norm_w), s_wo=_b(s_w_out),
        r_mua=row(r_mu[:, :3 * D]), r_mub=row(r_mu[:, 3 * D:]),
        r_w0=row(r_w0), r_w2=jnp.concatenate([r_w2, zl], axis=1),
        r_a0=row(r_a0), r_a2=jnp.concatenate([zl, r_a2], axis=1),
        r_kk=row(r_k_k), r_ka=row(r_k_a), r_rk=row(r_r_k.reshape(r_r_k.shape[0], -1)),
        r_lnw=row(r_ln_w), r_lnb=row(r_ln_b), r_wo=_b(r_w_out),
        wo=_b(w_out),
    )


def _row_tile(n, cap):
    t = min(n, cap)
    while n % t:
        t //= 2
    return t


def _run(x, states, prep, norm_w, final_norm_w):
    B, T, D = x.shape
    depth = norm_w.shape[0]
    mC, mn, mm, mconv, sS, sconv, rS, rsh = states
    L = CHUNK if T % CHUNK == 0 else T
    TB = L * MIXER_CHUNKS_PER_STEP if T % (L * MIXER_CHUNKS_PER_STEP) == 0 else L
    KW = mconv.shape[2] + 1
    assert T >= KW - 1 and L % SUBLANE == 0
    N = B * T
    tm = _row_tile(N, 1024)
    tmo = _row_tile(N, 256)
    x2 = x.reshape(N, D)
    h = _rmsnorm(x2, norm_w[0][None, :], BF16, tm)
    new = [[] for _ in range(8)]
    for l in range(depth):
        p = {k: v[l] for k, v in prep.items()}
        proj = lambda w, dt: _matmul(h, w, l, tm, 1024 if w.shape[2] % 1024 == 0 else w.shape[2], dt)
        uA = proj(prep["wA"], BF16)
        uB = proj(prep["wB"], BF16).reshape(B, T, -1)
        uC = proj(prep["wC"], BF16).reshape(B, T, -1)
        uD = proj(prep["wD"], BF16).reshape(B, T, -1)
        uE = proj(prep["wE"], F32).reshape(B, T, -1)
        (hm, mC1, mn1, mm1), (ys, sS1), (yr, rS1) = _mixers(
            [_mlstm_spec(uB, uE, mconv[l], mC[l], mn[l], mm[l], p, L, TB),
             _ssd_spec(uC, uE, sconv[l], sS[l], p, L, TB),
             _rwkv_spec(uD, uE, rsh[l][..., :3 * D], rsh[l][..., 3 * D:], rS[l], p, L, TB)], B, T, L, TB)
        mm1 = mm1[..., 0]
        last = l == depth - 1
        nw = (final_norm_w if last else norm_w[l + 1])[None, :]
        x2, h = _out_proj(hm.reshape(N, D), ys.reshape(N, D), yr.reshape(N, D), uA, x2,
                          prep["m_wo"], prep["s_wo"], prep["r_wo"], prep["wo"], l, p["bg"], nw,
                          F32 if last else BF16, tmo)
        outs = (mC1, mn1, mm1, uB[:, T - (KW - 1):, 0:D].astype(F32), sS1, uC[:, T - (KW - 1):, D:].astype(F32), rS1,
                jnp.concatenate([uD[:, T - 1:, 0:3 * D].astype(F32), uE[:, T - 1:, E_LO:E_LO + LANE]], axis=-1))
        for acc, s in zip(new, outs):
            acc.append(s)
    return h.reshape(B, T, D), [jnp.stack(s) for s in new]


def kernel(x_prompt, x_sample, state_mlstm_C, state_mlstm_n, state_mlstm_m, state_mlstm_conv, state_ssd,
           state_ssd_conv, state_rwkv, state_rwkv_shift, norm_w, w_in, b_gate, m_conv_w, m_conv_b, m_wq, m_wk,
           m_wv, m_b_if, m_norm_w, m_skip, m_w_out, s_conv_w, s_conv_b, s_dt_bias, s_A_log, s_D, s_norm_w,
           s_w_out, r_mu, r_w0, r_w2, r_a0, r_a2, r_k_k, r_k_a, r_r_k, r_ln_w, r_ln_b, r_w_out, w_out,
           final_norm_w):
    D = x_prompt.shape[-1]
    prep = _prep_params(D, w_in, b_gate, m_conv_w, m_conv_b, m_wq, m_wk, m_wv, m_b_if, m_norm_w, m_skip, m_w_out,
                        s_conv_w, s_conv_b, s_dt_bias, s_A_log, s_D, s_norm_w, s_w_out,
                        r_mu, r_w0, r_w2, r_a0, r_a2, r_k_k, r_k_a, r_r_k, r_ln_w, r_ln_b, r_w_out, w_out)
    sample_states = (state_mlstm_C, state_mlstm_n, state_mlstm_m, state_mlstm_conv, state_ssd,
                     state_ssd_conv, state_rwkv, state_rwkv_shift)
    Bp = x_prompt.shape[0]
    zero_states = tuple(jnp.zeros(s.shape[:1] + (Bp,) + s.shape[2:], s.dtype) for s in sample_states)
    y_p, st_p = _run(x_prompt, zero_states, prep, norm_w, final_norm_w)
    y_s, st_s = _run(x_sample, sample_states, prep, norm_w, final_norm_w)
    return (y_p, y_s, *st_p, *st_s)
```

```python
import functools

import jax
import jax.numpy as jnp
from jax import lax
from jax.experimental import pallas as pl
from jax.experimental.pallas import tpu as pltpu

F32 = jnp.float32
BF16 = jnp.bfloat16
HI = lax.Precision.HIGHEST

CHUNK = 64
NORM_EPS = 1e-6
H_M = 4
MLSTM_LN_EPS = 1e-5
P_S = 64
G_S = 4
N_S = 128
SSD_GN_EPS = 1e-5
HD_R = 64
LORA_W = 64
LORA_A = 64
RWKV_GN_EPS = 64e-5
RWKV_PACK = 4
SOLVE_BASE = 16
MIXER_CHUNKS_PER_STEP = 4

LANE = 128
SUBLANE = 8
MXU_DIM = 256
VMEM_LIMIT_BYTES = 48 * 1024 * 1024

E_LO = 0 * LANE
E_IF = 1 * LANE
E_DT = 2 * LANE
E_W = 3 * LANE


def _nn(a, b, prec=None):
    return jnp.dot(a, b, precision=prec, preferred_element_type=F32)


def _nt(a, b, prec=None):
    return lax.dot_general(a, b, (((1,), (1,)), ((), ())), precision=prec, preferred_element_type=F32)


def _tn(a, b, prec=None):
    return lax.dot_general(a, b, (((0,), (0,)), ((), ())), precision=prec, preferred_element_type=F32)


def _b(x):
    return x.astype(BF16)


def _bf16_terms(x, n):
    terms, r = [], x
    for _ in range(n):
        t = _b(r)
        terms.append(t)
        r = r - t.astype(F32)
    return terms


def _split(x):
    return tuple(_bf16_terms(x, 2))


def _cat0(*xs):
    return jnp.concatenate(xs, axis=0)


def _cat1(*xs):
    return jnp.concatenate(xs, axis=1)


def _softplus(x):
    return jnp.maximum(x, 0.0) + jnp.log1p(jnp.exp(-jnp.abs(x)))


def _log_sigmoid(x):
    return -_softplus(-x)


def _silu(x):
    return x * jax.nn.sigmoid(x)


def _iota2(shape, dim):
    return lax.broadcasted_iota(jnp.int32, shape, dim)


def _tri_consts(L):
    r, c = _iota2((L, L), 0), _iota2((L, L), 1)
    causal = c <= r
    return causal, causal.astype(F32), (r <= c).astype(F32)


def _eye(n):
    return (_iota2((n, n), 0) == _iota2((n, n), 1)).astype(F32)


def _shifted_rows(xp_ref, x, shifts, L):
    xfull = _cat0(xp_ref[...], x)
    xp_ref[...] = x[L - SUBLANE:L, :]
    return [pltpu.roll(xfull, s, 0)[SUBLANE:SUBLANE + L, :] for s in shifts]


def _causal_conv(xp_ref, x, cw_ref, cb_ref, L, KW):
    taps = _shifted_rows(xp_ref, x, list(range(KW - 1, 0, -1)), L) + [x]
    y = cb_ref[...]
    for j in range(KW):
        y = y + taps[j] * cw_ref[j:j + 1, :]
    return y


def _norm_kernel(x_ref, w_ref, o_ref):
    x = x_ref[...]
    y = x * lax.rsqrt(jnp.mean(x * x, axis=-1, keepdims=True) + NORM_EPS)
    o_ref[...] = (y * w_ref[...]).astype(o_ref.dtype)


def _rmsnorm(x, w, out_dtype, tm):
    n, d = x.shape
    return pl.pallas_call(
        _norm_kernel,
        grid=(n // tm,),
        in_specs=[pl.BlockSpec((tm, d), lambda i: (i, 0)), pl.BlockSpec((1, d), lambda i: (0, 0))],
        out_specs=pl.BlockSpec((tm, d), lambda i: (i, 0)),
        out_shape=jax.ShapeDtypeStruct((n, d), out_dtype),
        compiler_params=pltpu.CompilerParams(dimension_semantics=("parallel",), vmem_limit_bytes=VMEM_LIMIT_BYTES),
        name="rmsnorm",
    )(x, w)


def _matmul_kernel(h_ref, w_ref, b_ref, o_ref):
    o_ref[...] = (_nn(h_ref[...], w_ref[...]) + b_ref[...]).astype(o_ref.dtype)


def _matmul(h, w, b, layer, tm, tn, out_dtype):
    n, d = h.shape
    nw = w.shape[2]
    return pl.pallas_call(
        _matmul_kernel,
        grid=(nw // tn, n // tm),
        in_specs=[pl.BlockSpec((tm, d), lambda j, i: (i, 0)),
                  pl.BlockSpec((None, d, tn), lambda j, i: (layer, 0, j)),
                  pl.BlockSpec((1, tn), lambda j, i: (0, j))],
        out_specs=pl.BlockSpec((tm, tn), lambda j, i: (i, j)),
        out_shape=jax.ShapeDtypeStruct((n, nw), out_dtype),
        compiler_params=pltpu.CompilerParams(dimension_semantics=("parallel", "parallel"),
                                             vmem_limit_bytes=VMEM_LIMIT_BYTES),
        name="proj_in",
    )(h, w, b)


def _out_kernel(hm_ref, ys_ref, yr_ref, g_ref, x_ref, wm_ref, ws_ref, wr_ref, wo_ref, nw_ref,
                xo_ref, ho_ref, *, D):
    g = jax.nn.sigmoid(g_ref[...].astype(F32))
    merged = (g[:, 0:D] * _nn(hm_ref[...], wm_ref[...])
              + g[:, D:2 * D] * _nn(ys_ref[...], ws_ref[...])
              + g[:, 2 * D:3 * D] * _nn(yr_ref[...], wr_ref[...]))
    out = x_ref[...] + _nn(_b(merged), wo_ref[...])
    xo_ref[...] = out
    y = out * lax.rsqrt(jnp.mean(out * out, axis=-1, keepdims=True) + NORM_EPS)
    ho_ref[...] = (y * nw_ref[...]).astype(ho_ref.dtype)


def _out_proj(hm, ys, yr, gate, x, wm, ws, wr, wo, layer, nw, h_dtype, tm):
    n, d = x.shape
    row = lambda w: pl.BlockSpec((tm, w), lambda i: (i, 0))
    full = lambda a: pl.BlockSpec(a.shape, lambda i: (0, 0))
    wspec = pl.BlockSpec((None, d, d), lambda i: (layer, 0, 0))
    return pl.pallas_call(
        functools.partial(_out_kernel, D=d),
        grid=(n // tm,),
        in_specs=[row(d), row(d), row(d), row(3 * d), row(d),
                  wspec, wspec, wspec, wspec, full(nw)],
        out_specs=[row(d), row(d)],
        out_shape=[jax.ShapeDtypeStruct((n, d), F32), jax.ShapeDtypeStruct((n, d), h_dtype)],
        compiler_params=pltpu.CompilerParams(dimension_semantics=("parallel",), vmem_limit_bytes=VMEM_LIMIT_BYTES),
        name="out_proj",
    )(hm, ys, yr, gate, x, wm, ws, wr, wo, nw)


def _blockdiag_apply(a, w_ref):
    ab = _b(a)
    return [_nn(ab[:, t * MXU_DIM:(t + 1) * MXU_DIM], w_ref[t]) for t in range(w_ref.shape[0])]


def _mlstm_part(phase, rows, mx_ref, mz_ref, mo_ref, e_ref, conv0_ref, C0_ref, n0_ref, m0_ref,
                cw_ref, cb_ref, wq_ref, wk_ref, wv_ref, nw_ref, skip_ref,
                h_ref, C1_ref, n1_ref, m1_ref,
                xp_ref, C_ref, n_ref, m_ref, *, L, H, DH, KW):
    D = H * DH
    if phase == "init":
        xp_ref[...] = jnp.zeros(xp_ref.shape, F32)
        xp_ref[SUBLANE - (KW - 1):SUBLANE, :] = conv0_ref[0]
        C_ref[...] = C0_ref[0]
        n_ref[...] = n0_ref[0]
        m_ref[...] = m0_ref[0]
        return
    if phase == "final":
        C1_ref[0] = C_ref[...]
        n1_ref[0] = n_ref[...]
        m1_ref[0] = m_ref[...]
        return

    mx, mz, mo = (r[0, rows, :].astype(F32) for r in (mx_ref, mz_ref, mo_ref))
    mc = _silu(_causal_conv(xp_ref, mx, cw_ref, cb_ref, L, KW))
    q = _cat1(*_blockdiag_apply(mc, wq_ref))
    k = _cat1(*_blockdiag_apply(mc, wk_ref))
    v = _cat1(*_blockdiag_apply(mx, wv_ref))

    causal, tril, triu = _tri_consts(L)
    ecol = e_ref[0, rows, E_IF:E_IF + LANE]
    eT = _nt(_eye(LANE), ecol, HI)[0:SUBLANE, :]
    b_col = _nn(tril, _log_sigmoid(ecol), HI)
    b_row = _nn(_log_sigmoid(eT), triu, HI)

    Hs = range(H)
    sl = [slice(h * DH, (h + 1) * DH) for h in Hs]
    qf, kf = [q[:, s] for s in sl], [k[:, s] for s in sl]
    qh, kh, vh = [_b(x) for x in qf], [_b(x) for x in kf], [_b(v[:, s]) for s in sl]
    bc, br = [b_col[:, H + h:H + h + 1] for h in Hs], [b_row[H + h:H + h + 1, :] for h in Hs]
    ic, ir = [ecol[:, h:h + 1] for h in Hs], [eT[h:h + 1, :] for h in Hs]
    mprev = [m_ref[h:h + 1, 0:1] for h in Hs]
    Ch, nh = [C_ref[h] for h in Hs], [n_ref[h:h + 1, :] for h in Hs]

    yield
    qk = [_nt(qh[h], kh[h]) for h in Hs]
    qC = [_nn(qh[h], _b(Ch[h])) for h in Hs]
    logD = [jnp.where(causal, bc[h] - br[h] + ir[h], -jnp.inf) for h in Hs]
    inter = [bc[h] + mprev[h] for h in Hs]
    m_t = [jnp.maximum(inter[h], jnp.max(logD[h], axis=1, keepdims=True)) for h in Hs]
    sc = [jnp.exp(inter[h] - m_t[h]) for h in Hs]
    s = [qk[h] * jnp.exp(logD[h] - m_t[h]) for h in Hs]
    yield
    sv = [_nn(_b(s[h]), vh[h]) for h in Hs]

    bL = [bc[h][L - 1:L, :] for h in Hs]
    m_new = [jnp.maximum(bL[h] + mprev[h], jnp.max(bL[h] - br[h] + ir[h], axis=1, keepdims=True)) for h in Hs]
    dec = [jnp.exp(bL[h] + mprev[h] - m_new[h]) for h in Hs]
    kw = [kf[h] * jnp.exp(bL[h] - bc[h] + ic[h] - m_new[h]) for h in Hs]
    upd = [_tn(_b(kw[h]), vh[h]) for h in Hs]
    yield

    outs = []
    for h in Hs:
        den = jnp.sum(s[h], axis=1, keepdims=True) + sc[h] * jnp.sum(qf[h] * nh[h], axis=1, keepdims=True)
        hc = (sv[h] + qC[h] * sc[h]) / jnp.maximum(jnp.abs(den), jnp.exp(-m_t[h]))
        mu = jnp.mean(hc, axis=1, keepdims=True)
        hz = hc - mu
        var = jnp.mean(hz * hz, axis=1, keepdims=True)
        outs.append(hz * lax.rsqrt(var + MLSTM_LN_EPS))
        C_ref[h] = dec[h] * Ch[h] + upd[h]
        n_ref[h:h + 1, :] = dec[h] * nh[h] + jnp.sum(kw[h], axis=0, keepdims=True)
        m_ref[h:h + 1, :] = jnp.broadcast_to(m_new[h], (1, LANE))

    hm = jnp.concatenate(outs, axis=1) * nw_ref[...]
    out = (jax.nn.sigmoid(mo) * hm + skip_ref[...] * mc) * _silu(mz)
    h_ref[0, rows, :] = out.astype(h_ref.dtype)


def _per_b(a):
    return pl.BlockSpec((1,) + a.shape[1:], lambda b, c: (b,) + (0,) * (a.ndim - 1))


def _full(a):
    return pl.BlockSpec(a.shape, lambda b, c: (0,) * a.ndim)


def _time_block(tb, w, j=0):
    return pl.BlockSpec((1, tb, w), lambda b, c: (b, c, j))


def _mlstm_spec(mx, mz, mo, e, conv0, C0, n0, m0, p, L, TB):
    B, T = e.shape[:2]
    H, DH = C0.shape[1], C0.shape[2]
    D = H * DH
    KW = p["m_cw"].shape[0]
    m0b = jnp.broadcast_to(m0[..., None], (B, H, LANE))
    params = [p["m_cw"], p["m_cb"], p["m_wq"], p["m_wk"], p["m_wv"], p["m_nw"], p["m_skip"]]
    return dict(
        body=functools.partial(_mlstm_part, L=L, H=H, DH=DH, KW=KW), stride=1,
        inputs=[mx[0], mz[0], mo[0], e, conv0, C0, n0, m0b] + params,
        in_specs=[_time_block(TB, D, mx[1]), _time_block(TB, D, mz[1]), _time_block(TB, D, mo[1]),
                  _time_block(TB, E_W), _per_b(conv0), _per_b(C0), _per_b(n0), _per_b(m0b)]
        + [_full(a) for a in params],
        out_specs=[_time_block(TB, D), _per_b(C0), _per_b(n0), _per_b(m0b)],
        out_shape=[jax.ShapeDtypeStruct((B, T, D), BF16), jax.ShapeDtypeStruct(C0.shape, F32),
                   jax.ShapeDtypeStruct(n0.shape, F32), jax.ShapeDtypeStruct(m0b.shape, F32)],
        scratch=[pltpu.VMEM((SUBLANE, D), F32), pltpu.VMEM((H, DH, DH), F32),
                 pltpu.VMEM((H, DH), F32), pltpu.VMEM((H, LANE), F32)],
    )


def _ssd_part(phase, rows, sz_ref, xbc_ref, e_ref, conv0_ref, S0_ref, cw_ref, cb_ref, alog_ref, drow_ref, nw_ref,
              xl_ref, xp_exp_ref, y_ref, S1_ref, xp_ref, S_ref, *, L, H, P, G, N, KW):
    D = H * P
    E = H // G
    EL, EP = E * L, E * P
    if phase == "init":
        xp_ref[...] = jnp.zeros(xp_ref.shape, F32)
        xp_ref[SUBLANE - (KW - 1):SUBLANE, :] = conv0_ref[0]
        S_ref[...] = S0_ref[0].reshape(G, EP, N)
        return
    if phase == "final":
        S1_ref[0] = S_ref[...].reshape(H, P, N)
        return

    sz = sz_ref[0, rows, :].astype(F32)
    xbc = _silu(_causal_conv(xp_ref, xbc_ref[0, rows, :].astype(F32), cw_ref, cb_ref, L, KW))
    xs, Bm, Cm = xbc[:, 0:D], xbc[:, D:D + G * N], xbc[:, D + G * N:]

    dt = _softplus(e_ref[0, rows, E_DT:E_DT + LANE])
    tril = _tri_consts(L)[1]
    cs = _nn(tril, dt * (-jnp.exp(alog_ref[...])), HI)
    csL = cs[L - 1:L, :]

    def expand(items, e_ref_):
        rows = [t for x, n in items for t in _bf16_terms(x, n)]
        r = _nn(_cat0(*rows), e_ref_[...])
        outs, k = [], 0
        for _, n in items:
            outs.append(sum(r[j * L:(j + 1) * L] for j in range(k, k + n)))
            k += n
        return outs

    yield
    csx, dtx = expand([(cs, 3), (dt, 2)], xl_ref)
    f_out, f_in = expand([(jnp.exp(cs), 2), (jnp.exp(csL - cs) * dt, 2)], xp_exp_ref)
    row, col = _iota2((L, H * L), 0), _iota2((L, H * L), 1) % L
    diag = row == col
    cs_src = jnp.sum(jnp.where(diag, csx, 0.0), axis=0, keepdims=True)
    dt_src = jnp.sum(jnp.where(diag, dtx, 0.0), axis=0, keepdims=True)
    Mall = jnp.exp(jnp.where(col <= row, csx - cs_src, -jnp.inf)) * dt_src
    decay_end = jnp.exp(csL)
    xw = xs * f_in
    bd = _BlockDiag(L, P, E)

    Gs = range(G)
    Bg = [_b(Bm[:, g * N:(g + 1) * N]) for g in Gs]
    Cg = [_b(Cm[:, g * N:(g + 1) * N]) for g in Gs]
    Sg = [S_ref[g] for g in Gs]
    yield
    CB = [_nt(Cg[g], _cat0(*[Bg[g]] * E)) for g in Gs]
    CS = [_nt(Cg[g], _b(Sg[g])) for g in Gs]
    Mg = [_b(CB[g] * Mall[:, g * EL:(g + 1) * EL]) for g in Gs]
    yield
    yi = [_nn(Mg[g], bd.heads(_b(xs[:, g * EP:(g + 1) * EP]))) for g in Gs]
    upd = [_tn(_b(xw[:, g * EP:(g + 1) * EP]), Bg[g]) for g in Gs]
    yield
    for g in Gs:
        scale = _cat0(*[jnp.broadcast_to(decay_end[:, h:h + 1], (P, 1)) for h in range(g * E, (g + 1) * E)])
        S_ref[g] = scale * Sg[g] + upd[g]

    yv = (_cat1(*yi) + _cat1(*CS) * f_out + drow_ref[...] * xs) * _silu(sz)
    DG = D // G
    parts = []
    for g in range(G):
        seg = yv[:, g * DG:(g + 1) * DG]
        parts.append(seg * lax.rsqrt(jnp.mean(seg * seg, axis=1, keepdims=True) + SSD_GN_EPS))
    y_ref[0, rows, :] = (jnp.concatenate(parts, axis=1) * nw_ref[...]).astype(y_ref.dtype)


def _lane_expander(n_heads, width):
    return (jnp.arange(LANE)[:, None] == jnp.arange(n_heads * width)[None, :] // width).astype(BF16)


def _ssd_spec(sz, xbc, e, conv0, S0, p, L, TB):
    B, T = e.shape[:2]
    H, P, N = S0.shape[1:]
    D = H * P
    DC = conv0.shape[-1]
    G = (DC - D) // (2 * N)
    KW = p["s_cw"].shape[0]
    params = [p["s_cw"], p["s_cb"], p["s_alog"], p["s_drow"], p["s_nw"],
              _lane_expander(H, L), _lane_expander(H, P)]
    return dict(
        body=functools.partial(_ssd_part, L=L, H=H, P=P, G=G, N=N, KW=KW), stride=1,
        inputs=[sz[0], xbc[0], e, conv0, S0] + params,
        in_specs=[_time_block(TB, D, sz[1]), _time_block(TB, DC, xbc[1]), _time_block(TB, E_W),
                  _per_b(conv0), _per_b(S0)] + [_full(a) for a in params],
        out_specs=[_time_block(TB, D), _per_b(S0)],
        out_shape=[jax.ShapeDtypeStruct((B, T, D), BF16), jax.ShapeDtypeStruct(S0.shape, F32)],
        scratch=[pltpu.VMEM((SUBLANE, DC), F32), pltpu.VMEM((G, H // G * P, N), F32)],
    )


def _dot_split(dot, x, rhs_hi, rhs_lo):
    xh, xl = _split(x)
    m = x.shape[0]
    r = dot(_cat0(xh, xl), rhs_hi)
    return r[:m] + r[m:] + dot(xh, rhs_lo)


class _BlockDiag:
    def __init__(self, L, HD, n):
        self.hm = [((_iota2((L, n * HD), 1) // HD) == h).astype(BF16) for h in range(n)]
        self.cm = [((_iota2((L, n * L), 1) // L) == h).astype(BF16) for h in range(n)]

    def heads(self, xb):
        return _cat0(*[xb * m for m in self.hm])

    def cols(self, xb):
        return _cat0(*[xb * m for m in self.cm])

    def heads2(self, x):
        xh, xl = _split(x)
        return self.heads(xh), self.heads(xl)

    def cols2(self, x):
        xh, xl = _split(x)
        return self.cols(xh), self.cols(xl)


def _unit_lower_solve(As, rhss, L, bd):
    base = min(SOLVE_BASE, L)
    W = As[0].shape[1]
    t, i = _iota2(As[0].shape, 0), _iota2(As[0].shape, 1) % L
    same = (t // base) == (i // base)
    eye = (t == i).astype(F32)
    mmc = lambda x, y: _nn(_b(x), bd.cols(_b(y)))
    mmv = lambda x, v: _nn(_b(x), bd.heads(_b(v)))
    Ads = [jnp.where(same, A, 0.0) for A in As]
    Tks = [eye + Ad for Ad in Ads]
    Pks = [mmc(Ad, Ad) for Ad in Ads]
    yield
    n = 2
    while n < base:
        if 2 * n < base:
            rs = [_nn(_b(Pk), _cat1(bd.cols(_b(Tk)), bd.cols(_b(Pk)))) for Tk, Pk in zip(Tks, Pks)]
            Tks = [Tk + r[:, :W] for Tk, r in zip(Tks, rs)]
            Pks = [r[:, W:] for r in rs]
        else:
            Tks = [Tk + mmc(Pk, Tk) for Tk, Pk in zip(Tks, Pks)]
        n *= 2
        yield
    nb = L // base
    if nb == 1:
        return [mmv(Tk, V) for Tk, V in zip(Tks, rhss)]
    HW = rhss[0].shape[1]
    rs = [_nn(_b(Tk), _cat1(bd.heads(_b(V)), bd.cols(_b(jnp.where(same, 0.0, A)))))
          for Tk, V, A in zip(Tks, rhss, As)]
    Xs, Ms = [r[:, :HW] for r in rs], [r[:, HW:] for r in rs]
    n = 1
    while n < nb:
        yield
        if 2 * n < nb:
            rs = [_nn(_b(M), _cat1(bd.heads(_b(X)), bd.cols(_b(M)))) for X, M in zip(Xs, Ms)]
            Xs = [X + r[:, :HW] for X, r in zip(Xs, rs)]
            Ms = [r[:, HW:] for r in rs]
        else:
            Xs = [X + mmv(M, X) for X, M in zip(Xs, Ms)]
        n *= 2
    return Xs


def _rwkv_part(phase, rows, rsh_ref, rz_ref, e_ref, sha_ref, shb_ref, S0_ref,
               mua_ref, mub_ref, w0_ref, w2_ref, a0_ref, a2_ref, kk_ref, ka_ref, rk_ref, lnw_ref, lnb_ref,
               y_ref, S1_ref, xpa_ref, xpb_ref, S_ref, *, L, H, HD):
    D = H * HD
    GW = RWKV_PACK * HD
    NG = H // RWKV_PACK
    CW = RWKV_PACK * L
    if phase == "init":
        xpa_ref[...] = jnp.zeros(xpa_ref.shape, F32)
        xpb_ref[...] = jnp.zeros(xpb_ref.shape, F32)
        xpa_ref[SUBLANE - 1:SUBLANE, :] = sha_ref[0]
        xpb_ref[SUBLANE - 1:SUBLANE, :] = shb_ref[0]
        S_ref[...] = jnp.zeros(S_ref.shape, F32)
        for h in range(H):
            g, j = divmod(h, RWKV_PACK)
            S_ref[g, j * HD:(j + 1) * HD, j * HD:(j + 1) * HD] = S0_ref[0, h]
        return
    if phase == "final":
        for h in range(H):
            g, j = divmod(h, RWKV_PACK)
            S1_ref[0, h] = S_ref[g, j * HD:(j + 1) * HD, j * HD:(j + 1) * HD]
        return

    cur = rsh_ref[0, rows, :].astype(F32)
    rz = rz_ref[0, rows, :].astype(F32)
    rx = cur + (_shifted_rows(xpa_ref, cur, [1], L)[0] - cur) * mua_ref[...]
    lo_cur = e_ref[0, rows, E_LO:E_LO + LANE]
    lo = lo_cur + (_shifted_rows(xpb_ref, lo_cur, [1], L)[0] - lo_cur) * mub_ref[...]
    rr, kr, vr = rx[:, 0:D], rx[:, D:2 * D], rx[:, 2 * D:3 * D]

    seg = ((_iota2((GW, GW), 0) // HD) == (_iota2((GW, GW), 1) // HD))
    segb = seg.astype(BF16)

    def head_sum(x):
        xh, xl = _split(x)
        rows = [t[:, g * GW:(g + 1) * GW] for t in (xh, xl) for g in range(NG)]
        r = _nn(_cat0(*rows), segb)
        return _cat1(*[r[g * L:(g + 1) * L] + r[(NG + g) * L:(NG + g + 1) * L] for g in range(NG)])

    w_log = -_softplus(-(w0_ref[...] + _dot_split(_nn, jnp.tanh(lo), *_split(w2_ref[...])))) - 0.5
    lw = -jnp.exp(w_log)
    a_sig = jax.nn.sigmoid(a0_ref[...] + _nn(_b(lo), _b(a2_ref[...])))
    kk = kr * kk_ref[...]
    kk = kk / jnp.maximum(jnp.sqrt(head_sum(kk * kk)), 1e-12)
    kmod = kr * (1.0 + (a_sig - 1.0) * ka_ref[...])
    kb = kk * a_sig
    yield

    trib = _tri_consts(L)[1].astype(BF16)
    l1, l2 = _split(lw)
    l3 = _b(lw - l1.astype(F32) - l2.astype(F32))
    cum = _nn(trib, l1) + _nn(trib, l2) + _nn(trib, l3)
    cumL = cum[L - 1:L, :]
    p_in, p_inv, p_end = jnp.exp(cum), jnp.exp(-cum), jnp.exp(cumL - cum)
    At = -kk * jnp.exp(cum - lw)
    Bt, Kt, Rt = kb * p_inv, kmod * p_inv, rr * p_in
    Bd, Kd = kb * p_end, kmod * p_end
    PL = jnp.exp(cumL)

    t, i = _iota2((L, 2 * CW), 0), _iota2((L, 2 * CW), 1) % L
    strict, incl = i < t, i <= t
    bd = _BlockDiag(L, HD, RWKV_PACK)

    G = range(NG)
    gsl = [slice(g * GW, (g + 1) * GW) for g in G]
    Sbd = [S_ref[g] for g in G]
    AR = [_cat0(_b(At[:, s]), _b(Rt[:, s])) for s in gsl]
    Bst = [bd.heads(_b(Bt[:, s])) for s in gsl]
    Kst = [bd.heads(_b(Kt[:, s])) for s in gsl]
    Vst = [bd.heads(_b(vr[:, s])) for s in gsl]
    yield
    g1 = [_nt(AR[g], _cat0(Bst[g], Kst[g])) for g in G]
    s1 = [_nt(AR[g], _b(Sbd[g])) for g in G]
    GA = [jnp.where(strict, g1[g][:L], 0.0) for g in G]
    GR = [jnp.where(incl, g1[g][L:], 0.0) for g in G]
    yield
    rhs = [s1[g][:L] + _nn(_b(GA[g][:, CW:]), Vst[g]) for g in G]
    yield
    U = yield from _unit_lower_solve([GA[g][:, :CW] for g in G], rhs, L, bd)
    yield
    Ub = [_b(x) for x in U]
    ys = [s1[g][L:] + _nn(_b(GR[g]), _cat0(bd.heads(Ub[g]), Vst[g])) for g in G]
    yield
    for g in G:
        upd = _tn(_cat0(Ub[g], _b(vr[:, gsl[g]])), _cat0(_b(Bd[:, gsl[g]]), _b(Kd[:, gsl[g]])))
        S_ref[g] = Sbd[g] * PL[:, gsl[g]] + jnp.where(seg, upd, 0.0)
    yield

    y = jnp.concatenate(ys, axis=1)
    mu = head_sum(y) * (1.0 / HD)
    yz = y - mu
    var = head_sum(yz * yz) * (1.0 / HD)
    yn = yz * lax.rsqrt(var + RWKV_GN_EPS) * lnw_ref[...] + lnb_ref[...]
    bonus = head_sum(rr * kmod * rk_ref[...]) * vr
    y_ref[0, rows, :] = ((yn + bonus) * _silu(rz)).astype(y_ref.dtype)


def _rwkv_spec(rsh, rz, e, sha, shb, S0, p, L, TB):
    B, T = e.shape[:2]
    H, HD = S0.shape[1], S0.shape[2]
    D = H * HD
    GW = RWKV_PACK * HD
    params = [p["r_mua"], p["r_mub"], p["r_w0"], p["r_w2"], p["r_a0"], p["r_a2"], p["r_kk"], p["r_ka"],
              p["r_rk"], p["r_lnw"], p["r_lnb"]]
    return dict(
        body=functools.partial(_rwkv_part, L=L, H=H, HD=HD), stride=2,
        inputs=[rsh[0], rz[0], e, sha, shb, S0] + params,
        in_specs=[_time_block(TB, 3 * D, rsh[1]), _time_block(TB, D, rz[1]), _time_block(TB, E_W),
                  _per_b(sha), _per_b(shb), _per_b(S0)]
        + [_full(a) for a in params],
        out_specs=[_time_block(TB, D), _per_b(S0)],
        out_shape=[jax.ShapeDtypeStruct((B, T, D), BF16), jax.ShapeDtypeStruct(S0.shape, F32)],
        scratch=[pltpu.VMEM((SUBLANE, 3 * D), F32), pltpu.VMEM((SUBLANE, LANE), F32),
                 pltpu.VMEM((H // RWKV_PACK, GW, GW), F32)],
    )


def _mixers_kernel(*refs, parts, strides, chunks, L):
    n_in, n_out = sum(p[1] for p in parts), sum(p[2] for p in parts)
    groups, i, o, s = [], 0, n_in, n_in + n_out
    for body, ni, no, ns in parts:
        groups.append((body, refs[i:i + ni] + refs[o:o + no] + refs[s:s + ns]))
        i, o, s = i + ni, o + no, s + ns
    c = pl.program_id(1)

    def run(phase, rows):
        live = [(body(phase, rows, *r), stride) for (body, r), stride in zip(groups, strides)]
        while live:
            nxt = []
            for gen, stride in live:
                try:
                    for _ in range(stride):
                        next(gen)
                    nxt.append((gen, stride))
                except StopIteration:
                    pass
            live = nxt

    @pl.when(c == 0)
    def _():
        run("init", None)

    for k in range(chunks):
        run("main", slice(k * L, (k + 1) * L))

    @pl.when(c == pl.num_programs(1) - 1)
    def _():
        run("final", None)


def _mixers(specs, B, T, L, TB):
    parts = tuple((s["body"], len(s["inputs"]), len(s["out_shape"]), len(s["scratch"])) for s in specs)
    outs = pl.pallas_call(
        functools.partial(_mixers_kernel, parts=parts, strides=tuple(s["stride"] for s in specs),
                          chunks=TB // L, L=L),
        grid=(B, T // TB),
        in_specs=[x for s in specs for x in s["in_specs"]],
        out_specs=[x for s in specs for x in s["out_specs"]],
        out_shape=[x for s in specs for x in s["out_shape"]],
        scratch_shapes=[x for s in specs for x in s["scratch"]],
        compiler_params=pltpu.CompilerParams(dimension_semantics=("parallel", "arbitrary"),
                                             vmem_limit_bytes=VMEM_LIMIT_BYTES),
        name="mixers",
    )(*[x for s in specs for x in s["inputs"]])
    res, k = [], 0
    for s in specs:
        res.append(outs[k:k + len(s["out_shape"])])
        k += len(s["out_shape"])
    return res


def _bd_tiles(w):
    dep, nb, qb, _ = w.shape
    rows = w.reshape(dep, nb * qb // MXU_DIM, MXU_DIM, qb)
    idx = jnp.arange(MXU_DIM)
    full = jnp.take(rows, idx % qb, axis=-1)
    return jnp.where((idx[:, None] // qb) == (idx[None, :] // qb), full, 0.0)


def _pad_lanes(a, width=LANE):
    return jnp.pad(a, [(0, 0)] * (a.ndim - 1) + [(0, width - a.shape[-1])])


def _prep_params(D, w_in, b_gate, m_conv_w, m_conv_b, m_wq, m_wk, m_wv, m_b_if, m_norm_w, m_skip, m_w_out,
                 s_conv_w, s_conv_b, s_dt_bias, s_A_log, s_D, s_norm_w, s_w_out,
                 r_mu, r_w0, r_w2, r_a0, r_a2, r_k_k, r_k_a, r_r_k, r_ln_w, r_ln_b, r_w_out, w_out):
    H_S = D // P_S
    DC = D + 2 * G_S * N_S
    n_lo = LORA_W + LORA_A
    assert n_lo == LANE and 2 * H_M <= SUBLANE and H_S <= LANE
    off_gate = 0
    off_m = off_gate + 3 * D
    off_if = off_m + 3 * D
    off_sz = off_if + 2 * H_M
    off_dt = off_sz + D + DC
    off_rsh = off_dt + H_S
    off_rz = off_rsh + 3 * D + n_lo
    assert off_rz + D == w_in.shape[-1]
    col = lambda o, n: w_in[:, :, o:o + n]
    row = lambda a: a[:, None, :]
    cat = lambda *a: jnp.concatenate(a, axis=-1)
    DH = D // H_M
    zl = jnp.zeros_like(r_w2)
    return dict(
        wS=_b(cat(col(off_gate, 3 * D), col(off_m + 2 * D, D))),
        bS=row(cat(b_gate, jnp.zeros((w_in.shape[0], D), F32))),
        wL=_b(cat(col(off_m + D, D), col(off_sz, D), col(off_rz, D))),
        wN=_b(cat(col(off_sz + D, DC), col(off_m, D), col(off_rsh, 3 * D))),
        wE=_b(cat(col(off_rsh + 3 * D, n_lo), _pad_lanes(col(off_if, 2 * H_M)), _pad_lanes(col(off_dt, H_S)))),
        m_cw=m_conv_w, m_cb=row(m_conv_b),
        m_wq=_b(_bd_tiles(m_wq)), m_wk=_b(_bd_tiles(m_wk) * DH ** -0.5), m_wv=_b(_bd_tiles(m_wv)),
        bE=row(jnp.concatenate([jnp.zeros((w_in.shape[0], n_lo), F32), _pad_lanes(m_b_if),
                                _pad_lanes(s_dt_bias)], axis=-1)),
        m_nw=row(m_norm_w), m_skip=row(m_skip), m_wo=_b(m_w_out),
        s_cw=s_conv_w, s_cb=row(s_conv_b), s_alog=row(_pad_lanes(s_A_log)),
        s_drow=row(jnp.repeat(s_D, P_S, axis=-1)), s_nw=row(s_norm_w), s_wo=_b(s_w_out),
        r_mua=row(r_mu[:, :3 * D]), r_mub=row(r_mu[:, 3 * D:]),
        r_w0=row(r_w0), r_w2=jnp.concatenate([r_w2, zl], axis=1),
        r_a0=row(r_a0), r_a2=jnp.concatenate([zl, r_a2], axis=1),
        r_kk=row(r_k_k), r_ka=row(r_k_a), r_rk=row(r_r_k.reshape(r_r_k.shape[0], -1)),
        r_lnw=row(r_ln_w), r_lnb=row(r_ln_b), r_wo=_b(r_w_out),
        wo=_b(w_out),
    )


def _row_tile(n, cap):
    t = min(n, cap)
    while n % t:
        t //= 2
    return t


def _run(x, states, prep, norm_w, final_norm_w):
    B, T, D = x.shape
    depth = norm_w.shape[0]
    mC, mn, mm, mconv, sS, sconv, rS, rsh = states
    L = CHUNK if T % CHUNK == 0 else T
    TB = L * MIXER_CHUNKS_PER_STEP if T % (L * MIXER_CHUNKS_PER_STEP) == 0 else L
    KW = mconv.shape[2] + 1
    DC = sconv.shape[-1]
    assert T >= KW - 1 and L % SUBLANE == 0
    assert DC % D == 0 and (DC + D) % (3 * D) == 0
    N = B * T
    tm = _row_tile(N, 1024)
    tmo = _row_tile(N, 256)
    x2 = x.reshape(N, D)
    h = _rmsnorm(x2, norm_w[0][None, :], BF16, tm)
    new = [[] for _ in range(8)]
    for l in range(depth):
        p = {k: v[l] for k, v in prep.items()}
        def proj(w, dt, bias=None):
            nw_ = w.shape[2]
            bias = jnp.zeros((1, nw_), F32) if bias is None else bias
            return _matmul(h, w, bias, l, tm, 1024 if nw_ % 1024 == 0 else nw_, dt)

        uS = proj(prep["wS"], BF16, p["bS"])
        uL = proj(prep["wL"], BF16).reshape(B, T, -1)
        uN = proj(prep["wN"], BF16).reshape(B, T, -1)
        uE = proj(prep["wE"], F32, p["bE"]).reshape(B, T, -1)
        uS3 = uS.reshape(B, T, -1)
        (hm, mC1, mn1, mm1), (ys, sS1), (yr, rS1) = _mixers(
            [_mlstm_spec((uN, DC // D), (uL, 0), (uS3, 3), uE, mconv[l], mC[l], mn[l], mm[l], p, L, TB),
             _ssd_spec((uL, 1), (uN, 0), uE, sconv[l], sS[l], p, L, TB),
             _rwkv_spec((uN, (DC + D) // (3 * D)), (uL, 2), uE, rsh[l][..., :3 * D], rsh[l][..., 3 * D:], rS[l], p, L, TB)],
            B, T, L, TB)
        mm1 = mm1[..., 0]
        last = l == depth - 1
        nw = (final_norm_w if last else norm_w[l + 1])[None, :]
        x2, h = _out_proj(hm.reshape(N, D), ys.reshape(N, D), yr.reshape(N, D), uS, x2,
                          prep["m_wo"], prep["s_wo"], prep["r_wo"], prep["wo"], l, nw, F32 if last else BF16, tmo)
        tail = uN[:, T - (KW - 1):].astype(F32)
        outs = (mC1, mn1, mm1, tail[..., DC:DC + D], sS1, tail[..., 0:DC], rS1,
                jnp.concatenate([tail[:, KW - 2:, DC + D:], uE[:, T - 1:, E_LO:E_LO + LANE]], axis=-1))
        for acc, s in zip(new, outs):
            acc.append(s)
    return h.reshape(B, T, D), [jnp.stack(s) for s in new]


def kernel(x_prompt, x_sample, state_mlstm_C, state_mlstm_n, state_mlstm_m, state_mlstm_conv, state_ssd,
           state_ssd_conv, state_rwkv, state_rwkv_shift, norm_w, w_in, b_gate, m_conv_w, m_conv_b, m_wq, m_wk,
           m_wv, m_b_if, m_norm_w, m_skip, m_w_out, s_conv_w, s_conv_b, s_dt_bias, s_A_log, s_D, s_norm_w,
           s_w_out, r_mu, r_w0, r_w2, r_a0, r_a2, r_k_k, r_k_a, r_r_k, r_ln_w, r_ln_b, r_w_out, w_out,
           final_norm_w):
    D = x_prompt.shape[-1]
    prep = _prep_params(D, w_in, b_gate, m_conv_w, m_conv_b, m_wq, m_wk, m_wv, m_b_if, m_norm_w, m_skip, m_w_out,
                        s_conv_w, s_conv_b, s_dt_bias, s_A_log, s_D, s_norm_w, s_w_out,
                        r_mu, r_w0, r_w2, r_a0, r_a2, r_k_k, r_k_a, r_r_k, r_ln_w, r_ln_b, r_w_out, w_out)
    sample_states = (state_mlstm_C, state_mlstm_n, state_mlstm_m, state_mlstm_conv, state_ssd,
                     state_ssd_conv, state_rwkv, state_rwkv_shift)
    Bp = x_prompt.shape[0]
    zero_states = tuple(jnp.zeros(s.shape[:1] + (Bp,) + s.shape[2:], s.dtype) for s in sample_states)
    y_p, st_p = _run(x_prompt, zero_states, prep, norm_w, final_norm_w)
    y_s, st_s = _run(x_sample, sample_states, prep, norm_w, final_norm_w)
    return (y_p, y_s, *st_p, *st_s)
```

```python
import functools

import jax
import jax.numpy as jnp
from jax import lax
from jax.experimental import pallas as pl
from jax.experimental.pallas import tpu as pltpu

F32 = jnp.float32
BF16 = jnp.bfloat16
HI = lax.Precision.HIGHEST

CHUNK = 64
NORM_EPS = 1e-6
H_M = 4
MLSTM_LN_EPS = 1e-5
P_S = 64
G_S = 4
N_S = 128
SSD_GN_EPS = 1e-5
HD_R = 64
LORA_W = 64
LORA_A = 64
RWKV_GN_EPS = 64e-5
RWKV_PACK = 4
SOLVE_BASE = 16
MIXER_CHUNKS_PER_STEP = 4

LANE = 128
SUBLANE = 8
MXU_DIM = 256
VMEM_LIMIT_BYTES = 48 * 1024 * 1024

E_LO = 0 * LANE
E_IF = 1 * LANE
E_DT = 2 * LANE
E_W = 3 * LANE


def _nn(a, b, prec=None):
    return jnp.dot(a, b, precision=prec, preferred_element_type=F32)


def _nt(a, b, prec=None):
    return lax.dot_general(a, b, (((1,), (1,)), ((), ())), precision=prec, preferred_element_type=F32)


def _tn(a, b, prec=None):
    return lax.dot_general(a, b, (((0,), (0,)), ((), ())), precision=prec, preferred_element_type=F32)


def _b(x):
    return x.astype(BF16)


def _bf16_terms(x, n):
    terms, r = [], x
    for _ in range(n):
        t = _b(r)
        terms.append(t)
        r = r - t.astype(F32)
    return terms


def _split(x):
    return tuple(_bf16_terms(x, 2))


def _cat0(*xs):
    return jnp.concatenate(xs, axis=0)


def _cat1(*xs):
    return jnp.concatenate(xs, axis=1)


def _softplus(x):
    return jnp.maximum(x, 0.0) + jnp.log1p(jnp.exp(-jnp.abs(x)))


def _log_sigmoid(x):
    return -_softplus(-x)


def _silu(x):
    return x * jax.nn.sigmoid(x)


def _iota2(shape, dim):
    return lax.broadcasted_iota(jnp.int32, shape, dim)


def _memo(consts, key, build):
    if key not in consts:
        consts[key] = build()
    return consts[key]


def _tri_consts(consts, L):
    def build():
        r, c = _iota2((L, L), 0), _iota2((L, L), 1)
        causal = c <= r
        return causal, causal.astype(F32), (r <= c).astype(F32)
    return _memo(consts, ("tri", L), build)


def _eye(consts, n):
    return _memo(consts, ("eye", n), lambda: (_iota2((n, n), 0) == _iota2((n, n), 1)).astype(F32))


def _shifted_rows(xp_ref, x, shifts, L):
    xfull = _cat0(xp_ref[...], x)
    xp_ref[...] = x[L - SUBLANE:L, :]
    return [pltpu.roll(xfull, s, 0)[SUBLANE:SUBLANE + L, :] for s in shifts]


def _causal_conv(xp_ref, x, cw_ref, cb_ref, L, KW):
    taps = _shifted_rows(xp_ref, x, list(range(KW - 1, 0, -1)), L) + [x]
    y = cb_ref[...]
    for j in range(KW):
        y = y + taps[j] * cw_ref[j:j + 1, :]
    return y


def _norm_kernel(x_ref, w_ref, o_ref):
    x = x_ref[...]
    y = x * lax.rsqrt(jnp.mean(x * x, axis=-1, keepdims=True) + NORM_EPS)
    o_ref[...] = (y * w_ref[...]).astype(o_ref.dtype)


def _rmsnorm(x, w, out_dtype, tm):
    n, d = x.shape
    return pl.pallas_call(
        _norm_kernel,
        grid=(n // tm,),
        in_specs=[pl.BlockSpec((tm, d), lambda i: (i, 0)), pl.BlockSpec((1, d), lambda i: (0, 0))],
        out_specs=pl.BlockSpec((tm, d), lambda i: (i, 0)),
        out_shape=jax.ShapeDtypeStruct((n, d), out_dtype),
        compiler_params=pltpu.CompilerParams(dimension_semantics=("parallel",), vmem_limit_bytes=VMEM_LIMIT_BYTES),
        name="rmsnorm",
    )(x, w)


def _matmul_kernel(h_ref, w_ref, b_ref, o_ref):
    o_ref[...] = (_nn(h_ref[...], w_ref[...]) + b_ref[...]).astype(o_ref.dtype)


def _matmul(h, w, b, layer, tm, tn, out_dtype):
    n, d = h.shape
    nw = w.shape[2]
    return pl.pallas_call(
        _matmul_kernel,
        grid=(nw // tn, n // tm),
        in_specs=[pl.BlockSpec((tm, d), lambda j, i: (i, 0)),
                  pl.BlockSpec((None, d, tn), lambda j, i: (layer, 0, j)),
                  pl.BlockSpec((1, tn), lambda j, i: (0, j))],
        out_specs=pl.BlockSpec((tm, tn), lambda j, i: (i, j)),
        out_shape=jax.ShapeDtypeStruct((n, nw), out_dtype),
        compiler_params=pltpu.CompilerParams(dimension_semantics=("parallel", "parallel"),
                                             vmem_limit_bytes=VMEM_LIMIT_BYTES),
        name="proj_in",
    )(h, w, b)


def _out_kernel(hm_ref, ys_ref, yr_ref, g_ref, x_ref, wm_ref, ws_ref, wr_ref, wo_ref, nw_ref,
                xo_ref, ho_ref, *, D):
    g = jax.nn.sigmoid(g_ref[...].astype(F32))
    merged = (g[:, 0:D] * _nn(hm_ref[...], wm_ref[...])
              + g[:, D:2 * D] * _nn(ys_ref[...], ws_ref[...])
              + g[:, 2 * D:3 * D] * _nn(yr_ref[...], wr_ref[...]))
    out = x_ref[...] + _nn(_b(merged), wo_ref[...])
    xo_ref[...] = out
    y = out * lax.rsqrt(jnp.mean(out * out, axis=-1, keepdims=True) + NORM_EPS)
    ho_ref[...] = (y * nw_ref[...]).astype(ho_ref.dtype)


def _out_proj(hm, ys, yr, gate, x, wm, ws, wr, wo, layer, nw, h_dtype, tm):
    n, d = x.shape
    row = lambda w: pl.BlockSpec((tm, w), lambda i: (i, 0))
    full = lambda a: pl.BlockSpec(a.shape, lambda i: (0, 0))
    wspec = pl.BlockSpec((None, d, d), lambda i: (layer, 0, 0))
    return pl.pallas_call(
        functools.partial(_out_kernel, D=d),
        grid=(n // tm,),
        in_specs=[row(d), row(d), row(d), row(3 * d), row(d),
                  wspec, wspec, wspec, wspec, full(nw)],
        out_specs=[row(d), row(d)],
        out_shape=[jax.ShapeDtypeStruct((n, d), F32), jax.ShapeDtypeStruct((n, d), h_dtype)],
        compiler_params=pltpu.CompilerParams(dimension_semantics=("parallel",), vmem_limit_bytes=VMEM_LIMIT_BYTES),
        name="out_proj",
    )(hm, ys, yr, gate, x, wm, ws, wr, wo, nw)


def _blockdiag_apply(a, w_ref):
    ab = _b(a)
    return [_nn(ab[:, t * MXU_DIM:(t + 1) * MXU_DIM], w_ref[t]) for t in range(w_ref.shape[0])]


def _mlstm_part(phase, rows, consts, mx_ref, mz_ref, mo_ref, e_ref, conv0_ref, C0_ref, n0_ref, m0_ref,
                cw_ref, cb_ref, wq_ref, wk_ref, wv_ref, nw_ref, skip_ref,
                h_ref, C1_ref, n1_ref, m1_ref,
                xp_ref, C_ref, n_ref, m_ref, *, L, H, DH, KW):
    D = H * DH
    if phase == "init":
        xp_ref[...] = jnp.zeros(xp_ref.shape, F32)
        xp_ref[SUBLANE - (KW - 1):SUBLANE, :] = conv0_ref[0]
        C_ref[...] = C0_ref[0]
        n_ref[...] = n0_ref[0]
        m_ref[...] = m0_ref[0]
        return
    if phase == "final":
        C1_ref[0] = C_ref[...]
        n1_ref[0] = n_ref[...]
        m1_ref[0] = m_ref[...]
        return

    mx, mz, mo = (r[0, rows, :].astype(F32) for r in (mx_ref, mz_ref, mo_ref))
    mc = _silu(_causal_conv(xp_ref, mx, cw_ref, cb_ref, L, KW))
    yield
    q = _cat1(*_blockdiag_apply(mc, wq_ref))
    k = _cat1(*_blockdiag_apply(mc, wk_ref))
    v = _cat1(*_blockdiag_apply(mx, wv_ref))

    causal, tril, triu = _tri_consts(consts, L)
    ecol = e_ref[0, rows, E_IF:E_IF + LANE]
    eT = _nt(_eye(consts, LANE), ecol, HI)[0:SUBLANE, :]
    b_col = _nn(tril, _log_sigmoid(ecol), HI)
    b_row = _nn(_log_sigmoid(eT), triu, HI)

    Hs = range(H)
    sl = [slice(h * DH, (h + 1) * DH) for h in Hs]
    qf, kf = [q[:, s] for s in sl], [k[:, s] for s in sl]
    qh, kh, vh = [_b(x) for x in qf], [_b(x) for x in kf], [_b(v[:, s]) for s in sl]
    bc, br = [b_col[:, H + h:H + h + 1] for h in Hs], [b_row[H + h:H + h + 1, :] for h in Hs]
    ic, ir = [ecol[:, h:h + 1] for h in Hs], [eT[h:h + 1, :] for h in Hs]
    mprev = [m_ref[h:h + 1, 0:1] for h in Hs]
    Ch, nh = [C_ref[h] for h in Hs], [n_ref[h:h + 1, :] for h in Hs]

    yield
    qk = [_nt(qh[h], kh[h]) for h in Hs]
    qC = [_nn(qh[h], _b(Ch[h])) for h in Hs]
    logD = [jnp.where(causal, bc[h] - br[h] + ir[h], -jnp.inf) for h in Hs]
    inter = [bc[h] + mprev[h] for h in Hs]
    m_t = [jnp.maximum(inter[h], jnp.max(logD[h], axis=1, keepdims=True)) for h in Hs]
    sc = [jnp.exp(inter[h] - m_t[h]) for h in Hs]
    s = [qk[h] * jnp.exp(logD[h] - m_t[h]) for h in Hs]
    yield
    sv = [_nn(_b(s[h]), vh[h]) for h in Hs]

    bL = [bc[h][L - 1:L, :] for h in Hs]
    m_new = [jnp.maximum(bL[h] + mprev[h], jnp.max(bL[h] - br[h] + ir[h], axis=1, keepdims=True)) for h in Hs]
    dec = [jnp.exp(bL[h] + mprev[h] - m_new[h]) for h in Hs]
    kw = [kf[h] * jnp.exp(bL[h] - bc[h] + ic[h] - m_new[h]) for h in Hs]
    upd = [_tn(_b(kw[h]), vh[h]) for h in Hs]
    yield

    outs = []
    for h in Hs:
        den = jnp.sum(s[h], axis=1, keepdims=True) + sc[h] * jnp.sum(qf[h] * nh[h], axis=1, keepdims=True)
        hc = (sv[h] + qC[h] * sc[h]) / jnp.maximum(jnp.abs(den), jnp.exp(-m_t[h]))
        mu = jnp.mean(hc, axis=1, keepdims=True)
        hz = hc - mu
        var = jnp.mean(hz * hz, axis=1, keepdims=True)
        outs.append(hz * lax.rsqrt(var + MLSTM_LN_EPS))
        C_ref[h] = dec[h] * Ch[h] + upd[h]
        n_ref[h:h + 1, :] = dec[h] * nh[h] + jnp.sum(kw[h], axis=0, keepdims=True)
        m_ref[h:h + 1, :] = jnp.broadcast_to(m_new[h], (1, LANE))

    hm = jnp.concatenate(outs, axis=1) * nw_ref[...]
    out = (jax.nn.sigmoid(mo) * hm + skip_ref[...] * mc) * _silu(mz)
    h_ref[0, rows, :] = out.astype(h_ref.dtype)


def _per_b(a):
    return pl.BlockSpec((1,) + a.shape[1:], lambda b, c: (b,) + (0,) * (a.ndim - 1))


def _full(a):
    return pl.BlockSpec(a.shape, lambda b, c: (0,) * a.ndim)


def _time_block(tb, w, j=0):
    return pl.BlockSpec((1, tb, w), lambda b, c: (b, c, j))


def _mlstm_spec(mx, mz, mo, e, conv0, C0, n0, m0, p, L, TB):
    B, T = e.shape[:2]
    H, DH = C0.shape[1], C0.shape[2]
    D = H * DH
    KW = p["m_cw"].shape[0]
    m0b = jnp.broadcast_to(m0[..., None], (B, H, LANE))
    params = [p["m_cw"], p["m_cb"], p["m_wq"], p["m_wk"], p["m_wv"], p["m_nw"], p["m_skip"]]
    return dict(
        body=functools.partial(_mlstm_part, L=L, H=H, DH=DH, KW=KW), stride=1, stacked=1,
        inputs=[mx[0], mz[0], mo[0], e, conv0, C0, n0, m0b] + params,
        in_specs=[_time_block(TB, D, mx[1]), _time_block(TB, D, mz[1]), _time_block(TB, D, mo[1]),
                  _time_block(TB, E_W), _per_b(conv0), _per_b(C0), _per_b(n0), _per_b(m0b)]
        + [_full(a) for a in params],
        out_specs=[_time_block(TB, D), _per_b(C0), _per_b(n0), _per_b(m0b)],
        out_shape=[jax.ShapeDtypeStruct((B, T, D), BF16), jax.ShapeDtypeStruct(C0.shape, F32),
                   jax.ShapeDtypeStruct(n0.shape, F32), jax.ShapeDtypeStruct(m0b.shape, F32)],
        scratch=[pltpu.VMEM((SUBLANE, D), F32), pltpu.VMEM((H, DH, DH), F32),
                 pltpu.VMEM((H, DH), F32), pltpu.VMEM((H, LANE), F32)],
    )


def _ssd_part(phase, rows, consts, sz_ref, xbc_ref, e_ref, conv0_ref, S0_ref, cw_ref, cb_ref, alog_ref, drow_ref, nw_ref,
              xl_ref, xp_exp_ref, y_ref, S1_ref, xp_ref, S_ref, *, L, H, P, G, N, KW):
    D = H * P
    E = H // G
    EL, EP = E * L, E * P
    if phase == "init":
        xp_ref[...] = jnp.zeros(xp_ref.shape, F32)
        xp_ref[SUBLANE - (KW - 1):SUBLANE, :] = conv0_ref[0]
        S_ref[...] = S0_ref[0].reshape(G, EP, N)
        return
    if phase == "final":
        S1_ref[0] = S_ref[...].reshape(H, P, N)
        return

    sz = sz_ref[0, rows, :].astype(F32)
    xbc = _silu(_causal_conv(xp_ref, xbc_ref[0, rows, :].astype(F32), cw_ref, cb_ref, L, KW))
    xs, Bm, Cm = xbc[:, 0:D], xbc[:, D:D + G * N], xbc[:, D + G * N:]
    yield

    dt = _softplus(e_ref[0, rows, E_DT:E_DT + LANE])
    tril = _tri_consts(consts, L)[1]
    cs = _nn(tril, dt * (-jnp.exp(alog_ref[...])), HI)
    csL = cs[L - 1:L, :]

    def expand(items, e_ref_):
        rows = [t for x, n in items for t in _bf16_terms(x, n)]
        r = _nn(_cat0(*rows), e_ref_[...])
        outs, k = [], 0
        for _, n in items:
            outs.append(sum(r[j * L:(j + 1) * L] for j in range(k, k + n)))
            k += n
        return outs

    yield
    csx, dtx = expand([(cs, 3), (dt, 2)], xl_ref)
    f_out, f_in = expand([(jnp.exp(cs), 2), (jnp.exp(csL - cs) * dt, 2)], xp_exp_ref)
    def masks():
        row, col = _iota2((L, H * L), 0), _iota2((L, H * L), 1) % L
        return row == col, col <= row
    diag, lower = _memo(consts, ("ssd_masks", L, H), masks)
    cs_src = jnp.sum(jnp.where(diag, csx, 0.0), axis=0, keepdims=True)
    dt_src = jnp.sum(jnp.where(diag, dtx, 0.0), axis=0, keepdims=True)
    Mall = jnp.exp(jnp.where(lower, csx - cs_src, -jnp.inf)) * dt_src
    decay_end = jnp.exp(csL)
    xw = xs * f_in
    bd = _BlockDiag.get(consts, L, P, E)

    Gs = range(G)
    Bg = [_b(Bm[:, g * N:(g + 1) * N]) for g in Gs]
    Cg = [_b(Cm[:, g * N:(g + 1) * N]) for g in Gs]
    Sg = [S_ref[g] for g in Gs]
    yield
    CB = [_nt(Cg[g], _cat0(*[Bg[g]] * E)) for g in Gs]
    CS = [_nt(Cg[g], _b(Sg[g])) for g in Gs]
    Mg = [_b(CB[g] * Mall[:, g * EL:(g + 1) * EL]) for g in Gs]
    yield
    yi = [_nn(Mg[g], bd.heads(_b(xs[:, g * EP:(g + 1) * EP]))) for g in Gs]
    upd = [_tn(_b(xw[:, g * EP:(g + 1) * EP]), Bg[g]) for g in Gs]
    yield
    for g in Gs:
        scale = _cat0(*[jnp.broadcast_to(decay_end[:, h:h + 1], (P, 1)) for h in range(g * E, (g + 1) * E)])
        S_ref[g] = scale * Sg[g] + upd[g]

    yv = (_cat1(*yi) + _cat1(*CS) * f_out + drow_ref[...] * xs) * _silu(sz)
    DG = D // G
    parts = []
    for g in range(G):
        seg = yv[:, g * DG:(g + 1) * DG]
        parts.append(seg * lax.rsqrt(jnp.mean(seg * seg, axis=1, keepdims=True) + SSD_GN_EPS))
    y_ref[0, rows, :] = (jnp.concatenate(parts, axis=1) * nw_ref[...]).astype(y_ref.dtype)


def _lane_expander(n_heads, width):
    return (jnp.arange(LANE)[:, None] == jnp.arange(n_heads * width)[None, :] // width).astype(BF16)


def _ssd_spec(sz, xbc, e, conv0, S0, p, L, TB):
    B, T = e.shape[:2]
    H, P, N = S0.shape[1:]
    D = H * P
    DC = conv0.shape[-1]
    G = (DC - D) // (2 * N)
    KW = p["s_cw"].shape[0]
    params = [p["s_cw"], p["s_cb"], p["s_alog"], p["s_drow"], p["s_nw"],
              _lane_expander(H, L), _lane_expander(H, P)]
    return dict(
        body=functools.partial(_ssd_part, L=L, H=H, P=P, G=G, N=N, KW=KW), stride=1, stacked=1,
        inputs=[sz[0], xbc[0], e, conv0, S0] + params,
        in_specs=[_time_block(TB, D, sz[1]), _time_block(TB, DC, xbc[1]), _time_block(TB, E_W),
                  _per_b(conv0), _per_b(S0)] + [_full(a) for a in params],
        out_specs=[_time_block(TB, D), _per_b(S0)],
        out_shape=[jax.ShapeDtypeStruct((B, T, D), BF16), jax.ShapeDtypeStruct(S0.shape, F32)],
        scratch=[pltpu.VMEM((SUBLANE, DC), F32), pltpu.VMEM((G, H // G * P, N), F32)],
    )


def _dot_split(dot, x, rhs_hi, rhs_lo):
    xh, xl = _split(x)
    m = x.shape[0]
    r = dot(_cat0(xh, xl), rhs_hi)
    return r[:m] + r[m:] + dot(xh, rhs_lo)


class _BlockDiag:
    @classmethod
    def get(cls, consts, L, HD, n):
        return _memo(consts, ("blockdiag", L, HD, n), lambda: cls(L, HD, n))

    def __init__(self, L, HD, n):
        self.L = L
        self.hm = [((_iota2((L, n * HD), 1) // HD) == h).astype(BF16) for h in range(n)]
        self.cm = [((_iota2((L, n * L), 1) // L) == h).astype(BF16) for h in range(n)]

    def heads(self, xb):
        return _cat0(*[xb * m for m in self.hm])

    def cols(self, xb):
        return _cat0(*[xb * m for m in self.cm])

    def heads2(self, x):
        xh, xl = _split(x)
        return self.heads(xh), self.heads(xl)

    def cols2(self, x):
        xh, xl = _split(x)
        return self.cols(xh), self.cols(xl)


def _unit_lower_solve(consts, As, rhss, L, bd):
    base = min(SOLVE_BASE, L)
    W = As[0].shape[1]
    def masks():
        t, i = _iota2(As[0].shape, 0), _iota2(As[0].shape, 1) % L
        return (t // base) == (i // base), (t == i).astype(F32)
    same, eye = _memo(consts, ("solve_masks", As[0].shape, L, base), masks)
    mmc = lambda x, y: _nn(_b(x), bd.cols(_b(y)))
    mmv = lambda x, v: _nn(_b(x), bd.heads(_b(v)))
    Ads = [jnp.where(same, A, 0.0) for A in As]
    Tks = [eye + Ad for Ad in Ads]
    Pks = [mmc(Ad, Ad) for Ad in Ads]
    yield
    n = 2
    while n < base:
        if 2 * n < base:
            rs = [_nn(_b(Pk), _cat1(bd.cols(_b(Tk)), bd.cols(_b(Pk)))) for Tk, Pk in zip(Tks, Pks)]
            Tks = [Tk + r[:, :W] for Tk, r in zip(Tks, rs)]
            Pks = [r[:, W:] for r in rs]
        else:
            Tks = [Tk + mmc(Pk, Tk) for Tk, Pk in zip(Tks, Pks)]
        n *= 2
        yield
    nb = L // base
    if nb == 1:
        return [mmv(Tk, V) for Tk, V in zip(Tks, rhss)]
    HW = rhss[0].shape[1]
    rs = [_nn(_b(Tk), _cat1(bd.heads(_b(V)), bd.cols(_b(jnp.where(same, 0.0, A)))))
          for Tk, V, A in zip(Tks, rhss, As)]
    Xs, Ms = [r[:, :HW] for r in rs], [r[:, HW:] for r in rs]
    n = 1
    while n < nb:
        yield
        if 2 * n < nb:
            rs = [_nn(_b(M), _cat1(bd.heads(_b(X)), bd.cols(_b(M)))) for X, M in zip(Xs, Ms)]
            Xs = [X + r[:, :HW] for X, r in zip(Xs, rs)]
            Ms = [r[:, HW:] for r in rs]
        else:
            Xs = [X + mmv(M, X) for X, M in zip(Xs, Ms)]
        n *= 2
    return Xs


def _rwkv_part(phase, rows, consts, rsh_ref, rz_ref, e_ref, sha_ref, shb_ref, S0_ref,
               mua_ref, mub_ref, w0_ref, w2_ref, a0_ref, a2_ref, kk_ref, ka_ref, rk_ref, lnw_ref, lnb_ref,
               y_ref, S1_ref, xpa_ref, xpb_ref, S_ref, *, L, H, HD):
    D = H * HD
    GW = RWKV_PACK * HD
    NG = H // RWKV_PACK
    CW = RWKV_PACK * L
    if phase == "init":
        xpa_ref[...] = jnp.zeros(xpa_ref.shape, F32)
        xpb_ref[...] = jnp.zeros(xpb_ref.shape, F32)
        xpa_ref[SUBLANE - 1:SUBLANE, :] = sha_ref[0]
        xpb_ref[SUBLANE - 1:SUBLANE, :] = shb_ref[0]
        S_ref[...] = jnp.zeros(S_ref.shape, F32)
        for h in range(H):
            g, j = divmod(h, RWKV_PACK)
            S_ref[g, j * HD:(j + 1) * HD, j * HD:(j + 1) * HD] = S0_ref[0, h]
        return
    if phase == "final":
        for h in range(H):
            g, j = divmod(h, RWKV_PACK)
            S1_ref[0, h] = S_ref[g, j * HD:(j + 1) * HD, j * HD:(j + 1) * HD]
        return

    cur = rsh_ref[0, rows, :].astype(F32)
    rz = rz_ref[0, rows, :].astype(F32)
    rx = cur + (_shifted_rows(xpa_ref, cur, [1], L)[0] - cur) * mua_ref[...]
    lo_cur = e_ref[0, rows, E_LO:E_LO + LANE]
    lo = lo_cur + (_shifted_rows(xpb_ref, lo_cur, [1], L)[0] - lo_cur) * mub_ref[...]
    rr, kr, vr = rx[:, 0:D], rx[:, D:2 * D], rx[:, 2 * D:3 * D]
    yield

    def seg_masks():
        seg_ = (_iota2((GW, GW), 0) // HD) == (_iota2((GW, GW), 1) // HD)
        return seg_, seg_.astype(BF16)
    seg, segb = _memo(consts, ("rwkv_seg", GW, HD), seg_masks)

    def head_sum(x):
        xh, xl = _split(x)
        rows = [t[:, g * GW:(g + 1) * GW] for t in (xh, xl) for g in range(NG)]
        r = _nn(_cat0(*rows), segb)
        return _cat1(*[r[g * L:(g + 1) * L] + r[(NG + g) * L:(NG + g + 1) * L] for g in range(NG)])

    w_log = -_softplus(-(w0_ref[...] + _dot_split(_nn, jnp.tanh(lo), *_split(w2_ref[...])))) - 0.5
    lw = -jnp.exp(w_log)
    a_sig = jax.nn.sigmoid(a0_ref[...] + _nn(_b(lo), _b(a2_ref[...])))
    yield
    kk = kr * kk_ref[...]
    kk = kk / jnp.maximum(jnp.sqrt(head_sum(kk * kk)), 1e-12)
    kmod = kr * (1.0 + (a_sig - 1.0) * ka_ref[...])
    kb = kk * a_sig
    yield

    trib = _memo(consts, ("trib", L), lambda: _tri_consts(consts, L)[1].astype(BF16))
    l1, l2 = _split(lw)
    l3 = _b(lw - l1.astype(F32) - l2.astype(F32))
    cum = _nn(trib, l1) + _nn(trib, l2) + _nn(trib, l3)
    cumL = cum[L - 1:L, :]
    yield
    p_in, p_inv = jnp.exp(cum), jnp.exp(-cum)
    At = -kk * jnp.exp(cum - lw)
    Bt, Kt, Rt = kb * p_inv, kmod * p_inv, rr * p_in
    PL = jnp.exp(cumL)

    def tri_masks():
        t, i = _iota2((L, 2 * CW), 0), _iota2((L, 2 * CW), 1) % L
        return i < t, i <= t
    strict, incl = _memo(consts, ("rwkv_tri", L, CW), tri_masks)
    bd = _BlockDiag.get(consts, L, HD, RWKV_PACK)

    G = range(NG)
    gsl = [slice(g * GW, (g + 1) * GW) for g in G]
    Sbd = [S_ref[g] for g in G]
    AR = [_cat0(_b(At[:, s]), _b(Rt[:, s])) for s in gsl]
    Bb, Kb, Vb = ([_b(x[:, s]) for s in gsl] for x in (Bt, Kt, vr))
    Bst, Kst, Vst = ([bd.heads(x) for x in xs] for xs in (Bb, Kb, Vb))
    yield
    g1 = [_nt(AR[g], _cat0(Bst[g], Kst[g])) for g in G]
    s1 = [_nt(AR[g], _b(Sbd[g])) for g in G]
    GA = [jnp.where(strict, g1[g][:L], 0.0) for g in G]
    GR = [jnp.where(incl, g1[g][L:], 0.0) for g in G]
    yield
    rhs = [s1[g][:L] + _nn(_b(GA[g][:, CW:]), Vst[g]) for g in G]
    yield
    U = yield from _unit_lower_solve(consts, [GA[g][:, :CW] for g in G], rhs, L, bd)
    yield
    Ub = [_b(x) for x in U]
    ys = [s1[g][L:] + _nn(_b(GR[g]), _cat0(bd.heads(Ub[g]), Vst[g])) for g in G]
    yield
    for g in G:
        upd = _tn(_cat0(Ub[g], Vb[g]), _cat0(Bb[g], Kb[g]))
        S_ref[g] = (Sbd[g] + jnp.where(seg, upd, 0.0)) * PL[:, gsl[g]]
    yield

    y = jnp.concatenate(ys, axis=1)
    mu = head_sum(y) * (1.0 / HD)
    yz = y - mu
    yield
    var = head_sum(yz * yz) * (1.0 / HD)
    yn = yz * lax.rsqrt(var + RWKV_GN_EPS) * lnw_ref[...] + lnb_ref[...]
    bonus = head_sum(rr * kmod * rk_ref[...]) * vr
    y_ref[0, rows, :] = ((yn + bonus) * _silu(rz)).astype(y_ref.dtype)


def _rwkv_spec(rsh, rz, e, sha, shb, S0, p, L, TB):
    B, T = e.shape[:2]
    H, HD = S0.shape[1], S0.shape[2]
    D = H * HD
    GW = RWKV_PACK * HD
    params = [p["r_mua"], p["r_mub"], p["r_w0"], p["r_w2"], p["r_a0"], p["r_a2"], p["r_kk"], p["r_ka"],
              p["r_rk"], p["r_lnw"], p["r_lnb"]]
    return dict(
        body=functools.partial(_rwkv_part, L=L, H=H, HD=HD), stride=3, stacked=1,
        inputs=[rsh[0], rz[0], e, sha, shb, S0] + params,
        in_specs=[_time_block(TB, 3 * D, rsh[1]), _time_block(TB, D, rz[1]), _time_block(TB, E_W),
                  _per_b(sha), _per_b(shb), _per_b(S0)]
        + [_full(a) for a in params],
        out_specs=[_time_block(TB, D), _per_b(S0)],
        out_shape=[jax.ShapeDtypeStruct((B, T, D), BF16), jax.ShapeDtypeStruct(S0.shape, F32)],
        scratch=[pltpu.VMEM((SUBLANE, 3 * D), F32), pltpu.VMEM((SUBLANE, LANE), F32),
                 pltpu.VMEM((H // RWKV_PACK, GW, GW), F32)],
    )


def _mixers_kernel(*refs, parts, n_alias, strides, chunks, L):
    n_in, n_out = sum(p[1] for p in parts), sum(p[2] for p in parts)
    groups, i, o, s = [], 0, n_in + n_alias, n_in + n_alias + n_out
    for body, ni, no, ns in parts:
        groups.append((body, refs[i:i + ni] + refs[o:o + no] + refs[s:s + ns]))
        i, o, s = i + ni, o + no, s + ns
    c = pl.program_id(1)

    consts = {}

    def run(phase, rows):
        live = [(body(phase, rows, consts, *r), stride) for (body, r), stride in zip(groups, strides)]
        while live:
            nxt = []
            for gen, stride in live:
                try:
                    for _ in range(stride):
                        next(gen)
                    nxt.append((gen, stride))
                except StopIteration:
                    pass
            live = nxt

    @pl.when(c == 0)
    def _():
        run("init", None)

    for k in range(chunks):
        run("main", slice(k * L, (k + 1) * L))

    @pl.when(c == pl.num_programs(1) - 1)
    def _():
        run("final", None)


def _mixers(specs, B, T, L, TB, layer, depth, stacked):
    parts = tuple((s["body"], len(s["inputs"]), len(s["out_shape"]), len(s["scratch"])) for s in specs)
    n_in = sum(p[1] for p in parts)
    out_specs, out_shape, alias_in, aliases = [], [], [], {}
    for s, acc in zip(specs, stacked):
        for j, (spec, shape) in enumerate(zip(s["out_specs"], s["out_shape"])):
            if j == s["stacked"]:
                full = (depth,) + shape.shape
                aliases[n_in + len(alias_in)] = len(out_shape)
                alias_in.append(jnp.zeros(full, shape.dtype) if acc is None else acc)
                spec = pl.BlockSpec((None,) + tuple(spec.block_shape),
                                    lambda b, c, im=spec.index_map: (layer,) + tuple(im(b, c)))
                shape = jax.ShapeDtypeStruct(full, shape.dtype)
            out_specs.append(spec)
            out_shape.append(shape)
    outs = pl.pallas_call(
        functools.partial(_mixers_kernel, parts=parts, n_alias=len(alias_in),
                          strides=tuple(s["stride"] for s in specs), chunks=TB // L, L=L),
        grid=(B, T // TB),
        in_specs=[x for s in specs for x in s["in_specs"]] + [pl.BlockSpec(memory_space=pl.ANY)] * len(alias_in),
        out_specs=out_specs,
        out_shape=out_shape,
        scratch_shapes=[x for s in specs for x in s["scratch"]],
        input_output_aliases=aliases,
        compiler_params=pltpu.CompilerParams(dimension_semantics=("parallel", "arbitrary"),
                                             vmem_limit_bytes=VMEM_LIMIT_BYTES),
        name="mixers",
    )(*[x for s in specs for x in s["inputs"]], *alias_in)
    res, k = [], 0
    for s in specs:
        res.append(outs[k:k + len(s["out_shape"])])
        k += len(s["out_shape"])
    return res


def _bd_tiles(w):
    dep, nb, qb, _ = w.shape
    rows = w.reshape(dep, nb * qb // MXU_DIM, MXU_DIM, qb)
    idx = jnp.arange(MXU_DIM)
    full = jnp.take(rows, idx % qb, axis=-1)
    return jnp.where((idx[:, None] // qb) == (idx[None, :] // qb), full, 0.0)


def _pad_lanes(a, width=LANE):
    return jnp.pad(a, [(0, 0)] * (a.ndim - 1) + [(0, width - a.shape[-1])])


def _prep_params(D, w_in, b_gate, m_conv_w, m_conv_b, m_wq, m_wk, m_wv, m_b_if, m_norm_w, m_skip, m_w_out,
                 s_conv_w, s_conv_b, s_dt_bias, s_A_log, s_D, s_norm_w, s_w_out,
                 r_mu, r_w0, r_w2, r_a0, r_a2, r_k_k, r_k_a, r_r_k, r_ln_w, r_ln_b, r_w_out, w_out):
    H_S = D // P_S
    DC = D + 2 * G_S * N_S
    n_lo = LORA_W + LORA_A
    assert n_lo == LANE and 2 * H_M <= SUBLANE and H_S <= LANE
    off_gate = 0
    off_m = off_gate + 3 * D
    off_if = off_m + 3 * D
    off_sz = off_if + 2 * H_M
    off_dt = off_sz + D + DC
    off_rsh = off_dt + H_S
    off_rz = off_rsh + 3 * D + n_lo
    assert off_rz + D == w_in.shape[-1]
    col = lambda o, n: w_in[:, :, o:o + n]
    row = lambda a: a[:, None, :]
    cat = lambda *a: jnp.concatenate(a, axis=-1)
    DH = D // H_M
    zl = jnp.zeros_like(r_w2)
    return dict(
        wS=_b(cat(col(off_gate, 3 * D), col(off_m + 2 * D, D))),
        bS=row(cat(b_gate, jnp.zeros((w_in.shape[0], D), F32))),
        wL=_b(cat(col(off_m + D, D), col(off_sz, D), col(off_rz, D))),
        wN=_b(cat(col(off_sz + D, DC), col(off_m, D), col(off_rsh, 3 * D))),
        wE=_b(cat(col(off_rsh + 3 * D, n_lo), _pad_lanes(col(off_if, 2 * H_M)), _pad_lanes(col(off_dt, H_S)))),
        m_cw=m_conv_w, m_cb=row(m_conv_b),
        m_wq=_b(_bd_tiles(m_wq)), m_wk=_b(_bd_tiles(m_wk) * DH ** -0.5), m_wv=_b(_bd_tiles(m_wv)),
        bE=row(jnp.concatenate([jnp.zeros((w_in.shape[0], n_lo), F32), _pad_lanes(m_b_if),
                                _pad_lanes(s_dt_bias)], axis=-1)),
        m_nw=row(m_norm_w), m_skip=row(m_skip), m_wo=_b(m_w_out),
        s_cw=s_conv_w, s_cb=row(s_conv_b), s_alog=row(_pad_lanes(s_A_log)),
        s_drow=row(jnp.repeat(s_D, P_S, axis=-1)), s_nw=row(s_norm_w), s_wo=_b(s_w_out),
        r_mua=row(r_mu[:, :3 * D]), r_mub=row(r_mu[:, 3 * D:]),
        r_w0=row(r_w0), r_w2=jnp.concatenate([r_w2, zl], axis=1),
        r_a0=row(r_a0), r_a2=jnp.concatenate([zl, r_a2], axis=1),
        r_kk=row(r_k_k), r_ka=row(r_k_a), r_rk=row(r_r_k.reshape(r_r_k.shape[0], -1)),
        r_lnw=row(r_ln_w), r_lnb=row(r_ln_b), r_wo=_b(r_w_out),
        wo=_b(w_out),
    )


def _row_tile(n, cap):
    t = min(n, cap)
    while n % t:
        t //= 2
    return t


def _run(x, states, prep, norm_w, final_norm_w):
    B, T, D = x.shape
    depth = norm_w.shape[0]
    mC, mn, mm, mconv, sS, sconv, rS, rsh = states
    L = CHUNK if T % CHUNK == 0 else T
    TB = L * MIXER_CHUNKS_PER_STEP if T % (L * MIXER_CHUNKS_PER_STEP) == 0 else L
    KW = mconv.shape[2] + 1
    DC = sconv.shape[-1]
    assert T >= KW - 1 and L % SUBLANE == 0
    assert DC % D == 0 and (DC + D) % (3 * D) == 0
    N = B * T
    tm = _row_tile(N, 1024)
    tmo = _row_tile(N, 256)
    x2 = x.reshape(N, D)
    h = _rmsnorm(x2, norm_w[0][None, :], BF16, tm)
    new = [[] for _ in range(5)]
    stacked = (None, None, None)
    for l in range(depth):
        p = {k: v[l] for k, v in prep.items()}
        def proj(w, dt, bias=None):
            nw_ = w.shape[2]
            bias = jnp.zeros((1, nw_), F32) if bias is None else bias
            return _matmul(h, w, bias, l, tm, 1024 if nw_ % 1024 == 0 else nw_, dt)

        uS = proj(prep["wS"], BF16, p["bS"])
        uL = proj(prep["wL"], BF16).reshape(B, T, -1)
        uN = proj(prep["wN"], BF16).reshape(B, T, -1)
        uE = proj(prep["wE"], F32, p["bE"]).reshape(B, T, -1)
        uS3 = uS.reshape(B, T, -1)
        (yr, rS_all), (hm, mC_all, mn1, mm1), (ys, sS_all) = _mixers(
            [_rwkv_spec((uN, (DC + D) // (3 * D)), (uL, 2), uE, rsh[l][..., :3 * D], rsh[l][..., 3 * D:], rS[l], p, L, TB),
             _mlstm_spec((uN, DC // D), (uL, 0), (uS3, 3), uE, mconv[l], mC[l], mn[l], mm[l], p, L, TB),
             _ssd_spec((uL, 1), (uN, 0), uE, sconv[l], sS[l], p, L, TB)],
            B, T, L, TB, l, depth, stacked)
        stacked = (rS_all, mC_all, sS_all)
        mm1 = mm1[..., 0]
        last = l == depth - 1
        nw = (final_norm_w if last else norm_w[l + 1])[None, :]
        x2, h = _out_proj(hm.reshape(N, D), ys.reshape(N, D), yr.reshape(N, D), uS, x2,
                          prep["m_wo"], prep["s_wo"], prep["r_wo"], prep["wo"], l, nw, F32 if last else BF16, tmo)
        tail = uN[:, T - (KW - 1):].astype(F32)
        outs = (mn1, mm1, tail[..., DC:DC + D], tail[..., 0:DC],
                jnp.concatenate([tail[:, KW - 2:, DC + D:], uE[:, T - 1:, E_LO:E_LO + LANE]], axis=-1))
        for acc, s in zip(new, outs):
            acc.append(s)
    mn_all, mm_all, mconv_all, sconv_all, rsh_all = (jnp.stack(s) for s in new)
    rS_all, mC_all, sS_all = stacked
    return h.reshape(B, T, D), [mC_all, mn_all, mm_all, mconv_all, sS_all, sconv_all, rS_all, rsh_all]


def kernel(x_prompt, x_sample, state_mlstm_C, state_mlstm_n, state_mlstm_m, state_mlstm_conv, state_ssd,
           state_ssd_conv, state_rwkv, state_rwkv_shift, norm_w, w_in, b_gate, m_conv_w, m_conv_b, m_wq, m_wk,
           m_wv, m_b_if, m_norm_w, m_skip, m_w_out, s_conv_w, s_conv_b, s_dt_bias, s_A_log, s_D, s_norm_w,
           s_w_out, r_mu, r_w0, r_w2, r_a0, r_a2, r_k_k, r_k_a, r_r_k, r_ln_w, r_ln_b, r_w_out, w_out,
           final_norm_w):
    D = x_prompt.shape[-1]
    prep = _prep_params(D, w_in, b_gate, m_conv_w, m_conv_b, m_wq, m_wk, m_wv, m_b_if, m_norm_w, m_skip, m_w_out,
                        s_conv_w, s_conv_b, s_dt_bias, s_A_log, s_D, s_norm_w, s_w_out,
                        r_mu, r_w0, r_w2, r_a0, r_a2, r_k_k, r_k_a, r_r_k, r_ln_w, r_ln_b, r_w_out, w_out)
    sample_states = (state_mlstm_C, state_mlstm_n, state_mlstm_m, state_mlstm_conv, state_ssd,
                     state_ssd_conv, state_rwkv, state_rwkv_shift)
    Bp = x_prompt.shape[0]
    zero_states = tuple(jnp.zeros(s.shape[:1] + (Bp,) + s.shape[2:], s.dtype) for s in sample_states)
    y_p, st_p = _run(x_prompt, zero_states, prep, norm_w, final_norm_w)
    y_s, st_s = _run(x_sample, sample_states, prep, norm_w, final_norm_w)
    return (y_p, y_s, *st_p, *st_s)
```

```python
import functools

import jax
import jax.numpy as jnp
from jax import lax
from jax.experimental import pallas as pl
from jax.experimental.pallas import tpu as pltpu

F32 = jnp.float32
BF16 = jnp.bfloat16
HI = lax.Precision.HIGHEST

CHUNK = 64
NORM_EPS = 1e-6
H_M = 4
MLSTM_LN_EPS = 1e-5
P_S = 64
G_S = 4
N_S = 128
SSD_GN_EPS = 1e-5
HD_R = 64
LORA_W = 64
LORA_A = 64
RWKV_GN_EPS = 64e-5
RWKV_DECAY_SCALE = 0.6065306597126334
RWKV_PACK = 4
SOLVE_BASE = 16
MIXER_CHUNKS_PER_STEP = 4
MIXER_ROWS_PER_STEP = 1

LANE = 128
SUBLANE = 8
MXU_DIM = 256
VMEM_LIMIT_BYTES = 56 * 1024 * 1024

E_LO = 0 * LANE
E_IF = 1 * LANE
E_DT = 2 * LANE
E_W = 3 * LANE


def _nn(a, b, prec=None):
    return jnp.dot(a, b, precision=prec, preferred_element_type=F32)


def _nt(a, b, prec=None):
    return lax.dot_general(a, b, (((1,), (1,)), ((), ())), precision=prec, preferred_element_type=F32)


def _tn(a, b, prec=None):
    return lax.dot_general(a, b, (((0,), (0,)), ((), ())), precision=prec, preferred_element_type=F32)


def _b(x):
    return x.astype(BF16)


def _bf16_terms(x, n):
    terms, r = [], x
    for _ in range(n):
        t = _b(r)
        terms.append(t)
        r = r - t.astype(F32)
    return terms


def _split(x):
    return tuple(_bf16_terms(x, 2))


def _cat0(*xs):
    return jnp.concatenate(xs, axis=0)


def _cat1(*xs):
    return jnp.concatenate(xs, axis=1)


def _softplus(x):
    return jnp.maximum(x, 0.0) + jnp.log1p(jnp.exp(-jnp.abs(x)))


def _log_sigmoid(x):
    return -_softplus(-x)


def _silu(x):
    return x * jax.nn.sigmoid(x)


def _iota2(shape, dim):
    return lax.broadcasted_iota(jnp.int32, shape, dim)


def _memo(consts, key, build):
    if key not in consts:
        consts[key] = build()
    return consts[key]


def _tri_consts(consts, L):
    def build():
        r, c = _iota2((L, L), 0), _iota2((L, L), 1)
        causal = c <= r
        return causal, causal.astype(F32), (r <= c).astype(F32)
    return _memo(consts, ("tri", L), build)


def _eye(consts, n):
    return _memo(consts, ("eye", n), lambda: (_iota2((n, n), 0) == _iota2((n, n), 1)).astype(F32))


def _shifted_rows(xp_ref, x, shifts, L):
    xfull = _cat0(xp_ref[...], x)
    xp_ref[...] = x[L - SUBLANE:L, :]
    return [pltpu.roll(xfull, s, 0)[SUBLANE:SUBLANE + L, :] for s in shifts]


def _causal_conv(xp_ref, x, cw_ref, cb_ref, L, KW):
    taps = _shifted_rows(xp_ref, x, list(range(KW - 1, 0, -1)), L) + [x]
    y = cb_ref[...]
    for j in range(KW):
        y = y + taps[j] * cw_ref[j:j + 1, :]
    return y


def _norm_kernel(x_ref, w_ref, o_ref):
    x = x_ref[...]
    y = x * lax.rsqrt(jnp.mean(x * x, axis=-1, keepdims=True) + NORM_EPS)
    o_ref[...] = (y * w_ref[...]).astype(o_ref.dtype)


def _rmsnorm(x, w, out_dtype, tm):
    n, d = x.shape
    return pl.pallas_call(
        _norm_kernel,
        grid=(n // tm,),
        in_specs=[pl.BlockSpec((tm, d), lambda i: (i, 0)), pl.BlockSpec((1, d), lambda i: (0, 0))],
        out_specs=pl.BlockSpec((tm, d), lambda i: (i, 0)),
        out_shape=jax.ShapeDtypeStruct((n, d), out_dtype),
        compiler_params=pltpu.CompilerParams(dimension_semantics=("parallel",), vmem_limit_bytes=VMEM_LIMIT_BYTES),
        name="rmsnorm",
    )(x, w)


def _matmul_kernel(h_ref, w_ref, b_ref, o_ref):
    o_ref[...] = (_nn(h_ref[...], w_ref[...]) + b_ref[...]).astype(o_ref.dtype)


def _matmul(h, w, b, layer, tm, tn, out_dtype):
    n, d = h.shape
    nw = w.shape[2]
    return pl.pallas_call(
        _matmul_kernel,
        grid=(nw // tn, n // tm),
        in_specs=[pl.BlockSpec((tm, d), lambda j, i: (i, 0)),
                  pl.BlockSpec((None, d, tn), lambda j, i: (layer, 0, j)),
                  pl.BlockSpec((1, tn), lambda j, i: (0, j))],
        out_specs=pl.BlockSpec((tm, tn), lambda j, i: (i, j)),
        out_shape=jax.ShapeDtypeStruct((n, nw), out_dtype),
        compiler_params=pltpu.CompilerParams(dimension_semantics=("parallel", "parallel"),
                                             vmem_limit_bytes=VMEM_LIMIT_BYTES),
        name="proj_in",
    )(h, w, b)


def _out_kernel(hm_ref, ys_ref, yr_ref, g_ref, x_ref, wm_ref, ws_ref, wr_ref, wo_ref, nw_ref,
                xo_ref, ho_ref, *, D):
    g = jax.nn.sigmoid(g_ref[...].astype(F32))
    merged = (g[:, 0:D] * _nn(hm_ref[...], wm_ref[...])
              + g[:, D:2 * D] * _nn(ys_ref[...], ws_ref[...])
              + g[:, 2 * D:3 * D] * _nn(yr_ref[...], wr_ref[...]))
    out = x_ref[...] + _nn(_b(merged), wo_ref[...])
    xo_ref[...] = out
    y = out * lax.rsqrt(jnp.mean(out * out, axis=-1, keepdims=True) + NORM_EPS)
    ho_ref[...] = (y * nw_ref[...]).astype(ho_ref.dtype)


def _out_proj(hm, ys, yr, gate, x, wm, ws, wr, wo, layer, nw, h_dtype, tm):
    n, d = x.shape
    row = lambda w: pl.BlockSpec((tm, w), lambda i: (i, 0))
    full = lambda a: pl.BlockSpec(a.shape, lambda i: (0, 0))
    wspec = pl.BlockSpec((None, d, d), lambda i: (layer, 0, 0))
    return pl.pallas_call(
        functools.partial(_out_kernel, D=d),
        grid=(n // tm,),
        in_specs=[row(d), row(d), row(d), row(3 * d), row(d),
                  wspec, wspec, wspec, wspec, full(nw)],
        out_specs=[row(d), row(d)],
        out_shape=[jax.ShapeDtypeStruct((n, d), F32), jax.ShapeDtypeStruct((n, d), h_dtype)],
        compiler_params=pltpu.CompilerParams(dimension_semantics=("parallel",), vmem_limit_bytes=VMEM_LIMIT_BYTES),
        name="out_proj",
    )(hm, ys, yr, gate, x, wm, ws, wr, wo, nw)


def _blockdiag_apply(a, w_ref):
    ab = _b(a)
    return [_nn(ab[:, t * MXU_DIM:(t + 1) * MXU_DIM], w_ref[t]) for t in range(w_ref.shape[0])]


def _mlstm_part(phase, rows, consts, mx_ref, mz_ref, mo_ref, e_ref, conv0_ref, C0_ref, n0_ref, m0_ref,
                cw_ref, cb_ref, wq_ref, wk_ref, wv_ref, nw_ref, skip_ref,
                h_ref, C1_ref, n1_ref, m1_ref,
                xp_ref, C_ref, n_ref, m_ref, *, L, H, DH, KW):
    D = H * DH
    if phase == "init":
        xp_ref[...] = jnp.zeros(xp_ref.shape, F32)
        xp_ref[SUBLANE - (KW - 1):SUBLANE, :] = conv0_ref[0]
        C_ref[...] = C0_ref[0]
        n_ref[...] = n0_ref[0]
        m_ref[...] = m0_ref[0]
        return
    if phase == "final":
        C1_ref[0] = C_ref[...]
        n1_ref[0] = n_ref[...]
        m1_ref[0] = m_ref[...]
        return

    mx, mz, mo = (r[0, rows, :].astype(F32) for r in (mx_ref, mz_ref, mo_ref))
    mc = _silu(_causal_conv(xp_ref, mx, cw_ref, cb_ref, L, KW))
    yield
    q = _cat1(*_blockdiag_apply(mc, wq_ref))
    k = _cat1(*_blockdiag_apply(mc, wk_ref))
    v = _cat1(*_blockdiag_apply(mx, wv_ref))

    causal, tril, triu = _tri_consts(consts, L)
    ecol = e_ref[0, rows, E_IF:E_IF + LANE]
    eyeb, trilb, triub = _memo(consts, ("mlstm_b", L), lambda: (_b(_eye(consts, LANE)), _b(tril), _b(triu)))
    eT = sum(_nt(eyeb, t) for t in _bf16_terms(ecol, 3))[0:SUBLANE, :]
    b_col = sum(_nn(trilb, t) for t in _bf16_terms(_log_sigmoid(ecol), 3))
    b_row = sum(_nn(t, triub) for t in _bf16_terms(_log_sigmoid(eT), 3))

    Hs = range(H)
    sl = [slice(h * DH, (h + 1) * DH) for h in Hs]
    qf, kf = [q[:, s] for s in sl], [k[:, s] for s in sl]
    qh, kh, vh = [_b(x) for x in qf], [_b(x) for x in kf], [_b(v[:, s]) for s in sl]
    bc, br = [b_col[:, H + h:H + h + 1] for h in Hs], [b_row[H + h:H + h + 1, :] for h in Hs]
    ic, ir = [ecol[:, h:h + 1] for h in Hs], [eT[h:h + 1, :] for h in Hs]
    mprev = [m_ref[h:h + 1, 0:1] for h in Hs]
    Ch, nh = [C_ref[h] for h in Hs], [n_ref[h:h + 1, :] for h in Hs]

    yield
    qk = [_nt(qh[h], kh[h]) for h in Hs]
    qC = [_nn(qh[h], _b(Ch[h])) for h in Hs]
    logD = [jnp.where(causal, bc[h] - br[h] + ir[h], -jnp.inf) for h in Hs]
    inter = [bc[h] + mprev[h] for h in Hs]
    m_t = [jnp.maximum(inter[h], jnp.max(logD[h], axis=1, keepdims=True)) for h in Hs]
    sc = [jnp.exp(inter[h] - m_t[h]) for h in Hs]
    s = [qk[h] * jnp.exp(logD[h] - m_t[h]) for h in Hs]
    yield
    sv = [_nn(_b(s[h]), vh[h]) for h in Hs]

    bL = [bc[h][L - 1:L, :] for h in Hs]
    m_new = [jnp.maximum(bL[h] + mprev[h], jnp.max(bL[h] - br[h] + ir[h], axis=1, keepdims=True)) for h in Hs]
    dec = [jnp.exp(bL[h] + mprev[h] - m_new[h]) for h in Hs]
    kw = [kf[h] * jnp.exp(bL[h] - bc[h] + ic[h] - m_new[h]) for h in Hs]
    upd = [_tn(_b(kw[h]), vh[h]) for h in Hs]
    yield

    outs = []
    for h in Hs:
        den = jnp.sum(s[h], axis=1, keepdims=True) + sc[h] * jnp.sum(qf[h] * nh[h], axis=1, keepdims=True)
        hc = (sv[h] + qC[h] * sc[h]) * (1.0 / jnp.maximum(jnp.abs(den), jnp.exp(-m_t[h])))
        mu = jnp.mean(hc, axis=1, keepdims=True)
        hz = hc - mu
        var = jnp.mean(hz * hz, axis=1, keepdims=True)
        outs.append(hz * lax.rsqrt(var + MLSTM_LN_EPS))
        C_ref[h] = dec[h] * Ch[h] + upd[h]
        n_ref[h:h + 1, :] = dec[h] * nh[h] + jnp.sum(kw[h], axis=0, keepdims=True)
        m_ref[h:h + 1, :] = jnp.broadcast_to(m_new[h], (1, LANE))

    hm = jnp.concatenate(outs, axis=1) * nw_ref[...]
    out = (jax.nn.sigmoid(mo) * hm + skip_ref[...] * mc) * _silu(mz)
    h_ref[0, rows, :] = out.astype(h_ref.dtype)


def _per_b(a):
    return pl.BlockSpec((MIXER_ROWS_PER_STEP,) + a.shape[1:], lambda b, c: (b,) + (0,) * (a.ndim - 1))


def _full(a):
    return pl.BlockSpec(a.shape, lambda b, c: (0,) * a.ndim)


def _time_block(tb, w, j=0):
    return pl.BlockSpec((MIXER_ROWS_PER_STEP, tb, w), lambda b, c: (b, c, j))


def _mlstm_spec(mx, mz, mo, e, conv0, C0, n0, m0, p, L, TB):
    B, T = e.shape[:2]
    H, DH = C0.shape[1], C0.shape[2]
    D = H * DH
    KW = p["m_cw"].shape[0]
    m0b = jnp.broadcast_to(m0[..., None], (B, H, LANE))
    params = [p["m_cw"], p["m_cb"], p["m_wq"], p["m_wk"], p["m_wv"], p["m_nw"], p["m_skip"]]
    return dict(
        body=functools.partial(_mlstm_part, L=L, H=H, DH=DH, KW=KW), stride=1, stacked=1, n_batched=8,
        inputs=[mx[0], mz[0], mo[0], e, conv0, C0, n0, m0b] + params,
        in_specs=[_time_block(TB, D, mx[1]), _time_block(TB, D, mz[1]), _time_block(TB, D, mo[1]),
                  _time_block(TB, E_W), _per_b(conv0), _per_b(C0), _per_b(n0), _per_b(m0b)]
        + [_full(a) for a in params],
        out_specs=[_time_block(TB, D), _per_b(C0), _per_b(n0), _per_b(m0b)],
        out_shape=[jax.ShapeDtypeStruct((B, T, D), BF16), jax.ShapeDtypeStruct(C0.shape, F32),
                   jax.ShapeDtypeStruct(n0.shape, F32), jax.ShapeDtypeStruct(m0b.shape, F32)],
        scratch=[pltpu.VMEM((SUBLANE, D), F32), pltpu.VMEM((H, DH, DH), F32),
                 pltpu.VMEM((H, DH), F32), pltpu.VMEM((H, LANE), F32)],
    )


def _ssd_part(phase, rows, consts, sz_ref, xbc_ref, e_ref, conv0_ref, S0_ref, cw_ref, cb_ref, alog_ref, drow_ref, nw_ref,
              xl_ref, xp_exp_ref, y_ref, S1_ref, xp_ref, S_ref, *, L, H, P, G, N, KW):
    D = H * P
    E = H // G
    EL, EP = E * L, E * P
    if phase == "init":
        xp_ref[...] = jnp.zeros(xp_ref.shape, F32)
        xp_ref[SUBLANE - (KW - 1):SUBLANE, :] = conv0_ref[0]
        S_ref[...] = S0_ref[0].reshape(G, EP, N)
        return
    if phase == "final":
        S1_ref[0] = S_ref[...].reshape(H, P, N)
        return

    sz = sz_ref[0, rows, :].astype(F32)
    xbc = _silu(_causal_conv(xp_ref, xbc_ref[0, rows, :].astype(F32), cw_ref, cb_ref, L, KW))
    xs, Bm, Cm = xbc[:, 0:D], xbc[:, D:D + G * N], xbc[:, D + G * N:]
    yield

    dt = _softplus(e_ref[0, rows, E_DT:E_DT + LANE])
    tril = _tri_consts(consts, L)[1]
    trilb = _memo(consts, ("trib", L), lambda: _b(tril))
    cs = sum(_nn(trilb, t) for t in _bf16_terms(dt * (-jnp.exp(alog_ref[...])), 3))
    csL = cs[L - 1:L, :]

    def expand(items, e_ref_):
        rows = [t for x, n in items for t in _bf16_terms(x, n)]
        r = _nn(_cat0(*rows), e_ref_[...])
        outs, k = [], 0
        for _, n in items:
            outs.append(sum(r[j * L:(j + 1) * L] for j in range(k, k + n)))
            k += n
        return outs

    yield
    csx, dtx = expand([(cs, 3), (dt, 2)], xl_ref)
    f_out, f_in = expand([(jnp.exp(cs), 2), (jnp.exp(csL - cs) * dt, 2)], xp_exp_ref)
    def masks():
        row, col = _iota2((L, H * L), 0), _iota2((L, H * L), 1) % L
        return row == col, col <= row
    diag, lower = _memo(consts, ("ssd_masks", L, H), masks)
    cs_src = jnp.sum(jnp.where(diag, csx, 0.0), axis=0, keepdims=True)
    dt_src = jnp.sum(jnp.where(diag, dtx, 0.0), axis=0, keepdims=True)
    Mall = jnp.exp(jnp.where(lower, csx - cs_src, -jnp.inf)) * dt_src
    decay_end = jnp.exp(csL)
    xw = xs * f_in
    bd = _BlockDiag.get(consts, L, P, E)

    Gs = range(G)
    Bg = [_b(Bm[:, g * N:(g + 1) * N]) for g in Gs]
    Cg = [_b(Cm[:, g * N:(g + 1) * N]) for g in Gs]
    Sg = [S_ref[g] for g in Gs]
    yield
    CB = [_nt(Cg[g], _cat0(*[Bg[g]] * E)) for g in Gs]
    CS = [_nt(Cg[g], _b(Sg[g])) for g in Gs]
    Mg = [_b(CB[g] * Mall[:, g * EL:(g + 1) * EL]) for g in Gs]
    yield
    yi = [_nn(Mg[g], bd.heads(_b(xs[:, g * EP:(g + 1) * EP]))) for g in Gs]
    upd = [_tn(_b(xw[:, g * EP:(g + 1) * EP]), Bg[g]) for g in Gs]
    yield
    for g in Gs:
        scale = _cat0(*[jnp.broadcast_to(decay_end[:, h:h + 1], (P, 1)) for h in range(g * E, (g + 1) * E)])
        S_ref[g] = scale * Sg[g] + upd[g]

    yv = (_cat1(*yi) + _cat1(*CS) * f_out + drow_ref[...] * xs) * _silu(sz)
    DG = D // G
    parts = []
    for g in range(G):
        seg = yv[:, g * DG:(g + 1) * DG]
        parts.append(seg * lax.rsqrt(jnp.mean(seg * seg, axis=1, keepdims=True) + SSD_GN_EPS))
    y_ref[0, rows, :] = (jnp.concatenate(parts, axis=1) * nw_ref[...]).astype(y_ref.dtype)


def _lane_expander(n_heads, width):
    return (jnp.arange(LANE)[:, None] == jnp.arange(n_heads * width)[None, :] // width).astype(BF16)


def _ssd_spec(sz, xbc, e, conv0, S0, p, L, TB):
    B, T = e.shape[:2]
    H, P, N = S0.shape[1:]
    D = H * P
    DC = conv0.shape[-1]
    G = (DC - D) // (2 * N)
    KW = p["s_cw"].shape[0]
    params = [p["s_cw"], p["s_cb"], p["s_alog"], p["s_drow"], p["s_nw"],
              _lane_expander(H, L), _lane_expander(H, P)]
    return dict(
        body=functools.partial(_ssd_part, L=L, H=H, P=P, G=G, N=N, KW=KW), stride=1, stacked=1, n_batched=5,
        inputs=[sz[0], xbc[0], e, conv0, S0] + params,
        in_specs=[_time_block(TB, D, sz[1]), _time_block(TB, DC, xbc[1]), _time_block(TB, E_W),
                  _per_b(conv0), _per_b(S0)] + [_full(a) for a in params],
        out_specs=[_time_block(TB, D), _per_b(S0)],
        out_shape=[jax.ShapeDtypeStruct((B, T, D), BF16), jax.ShapeDtypeStruct(S0.shape, F32)],
        scratch=[pltpu.VMEM((SUBLANE, DC), F32), pltpu.VMEM((G, H // G * P, N), F32)],
    )


def _dot_split(dot, x, rhs_hi, rhs_lo):
    xh, xl = _split(x)
    m = x.shape[0]
    r = dot(_cat0(xh, xl), rhs_hi)
    return r[:m] + r[m:] + dot(xh, rhs_lo)


class _BlockDiag:
    @classmethod
    def get(cls, consts, L, HD, n):
        return _memo(consts, ("blockdiag", L, HD, n), lambda: cls(L, HD, n))

    def __init__(self, L, HD, n):
        self.hm = [((_iota2((L, n * HD), 1) // HD) == h).astype(BF16) for h in range(n)]
        self.cm = [((_iota2((L, n * L), 1) // L) == h).astype(BF16) for h in range(n)]

    def heads(self, xb):
        return _cat0(*[xb * m for m in self.hm])

    def cols(self, xb):
        return _cat0(*[xb * m for m in self.cm])

    def heads2(self, x):
        xh, xl = _split(x)
        return self.heads(xh), self.heads(xl)

    def cols2(self, x):
        xh, xl = _split(x)
        return self.cols(xh), self.cols(xl)


def _unit_lower_solve(consts, As, rhss, L, bd):
    base = min(SOLVE_BASE, L)
    W = As[0].shape[1]
    def masks():
        t, i = _iota2(As[0].shape, 0), _iota2(As[0].shape, 1) % L
        return (t // base) == (i // base), (t == i).astype(F32)
    same, eye = _memo(consts, ("solve_masks", As[0].shape, L, base), masks)
    mmc = lambda x, y: _nn(_b(x), bd.cols(_b(y)))
    mmv = lambda x, v: _nn(_b(x), bd.heads(_b(v)))
    Ads = [jnp.where(same, A, 0.0) for A in As]
    Tks = [eye + Ad for Ad in Ads]
    Pks = [mmc(Ad, Ad) for Ad in Ads]
    yield
    n = 2
    while n < base:
        if 2 * n < base:
            rs = [_nn(_b(Pk), _cat1(bd.cols(_b(Tk)), bd.cols(_b(Pk)))) for Tk, Pk in zip(Tks, Pks)]
            Tks = [Tk + r[:, :W] for Tk, r in zip(Tks, rs)]
            Pks = [r[:, W:] for r in rs]
        else:
            Tks = [Tk + mmc(Pk, Tk) for Tk, Pk in zip(Tks, Pks)]
        n *= 2
        yield
    nb = L // base
    if nb == 1:
        return [mmv(Tk, V) for Tk, V in zip(Tks, rhss)]
    HW = rhss[0].shape[1]
    rs = [_nn(_b(Tk), _cat1(bd.heads(_b(V)), bd.cols(_b(jnp.where(same, 0.0, A)))))
          for Tk, V, A in zip(Tks, rhss, As)]
    Xs, Ms = [r[:, :HW] for r in rs], [r[:, HW:] for r in rs]
    n = 1
    while n < nb:
        yield
        if 2 * n < nb:
            rs = [_nn(_b(M), _cat1(bd.heads(_b(X)), bd.cols(_b(M)))) for X, M in zip(Xs, Ms)]
            Xs = [X + r[:, :HW] for X, r in zip(Xs, rs)]
            Ms = [r[:, HW:] for r in rs]
        else:
            Xs = [X + mmv(M, X) for X, M in zip(Xs, Ms)]
        n *= 2
    return Xs


def _rwkv_part(phase, rows, consts, rsh_ref, rz_ref, e_ref, sha_ref, shb_ref, S0_ref,
               mua_ref, mub_ref, w0_ref, w2h_ref, w2l_ref, a0_ref, a2_ref, kk_ref, ka_ref, rk_ref, lnw_ref, lnb_ref,
               y_ref, S1_ref, xpa_ref, xpb_ref, S_ref, *, L, H, HD):
    D = H * HD
    GW = RWKV_PACK * HD
    NG = H // RWKV_PACK
    CW = RWKV_PACK * L
    if phase == "init":
        xpa_ref[...] = jnp.zeros(xpa_ref.shape, F32)
        xpb_ref[...] = jnp.zeros(xpb_ref.shape, F32)
        xpa_ref[SUBLANE - 1:SUBLANE, :] = sha_ref[0]
        xpb_ref[SUBLANE - 1:SUBLANE, :] = shb_ref[0]
        S_ref[...] = jnp.zeros(S_ref.shape, F32)
        for h in range(H):
            g, j = divmod(h, RWKV_PACK)
            S_ref[g, j * HD:(j + 1) * HD, j * HD:(j + 1) * HD] = S0_ref[0, h]
        return
    if phase == "final":
        for h in range(H):
            g, j = divmod(h, RWKV_PACK)
            S1_ref[0, h] = S_ref[g, j * HD:(j + 1) * HD, j * HD:(j + 1) * HD]
        return

    cur = rsh_ref[0, rows, :].astype(F32)
    rz = rz_ref[0, rows, :].astype(F32)
    rx = cur + (_shifted_rows(xpa_ref, cur, [1], L)[0] - cur) * mua_ref[...]
    lo_cur = e_ref[0, rows, E_LO:E_LO + LANE]
    lo = lo_cur + (_shifted_rows(xpb_ref, lo_cur, [1], L)[0] - lo_cur) * mub_ref[...]
    rr, kr, vr = rx[:, 0:D], rx[:, D:2 * D], rx[:, 2 * D:3 * D]
    yield

    def seg_masks():
        seg_ = (_iota2((GW, GW), 0) // HD) == (_iota2((GW, GW), 1) // HD)
        return seg_, seg_.astype(BF16)
    seg, segb = _memo(consts, ("rwkv_seg", GW, HD), seg_masks)

    def head_sum(x):
        xb = _b(x)
        r = _nn(_cat0(*[xb[:, g * GW:(g + 1) * GW] for g in range(NG)]), segb)
        return _cat1(*[r[g * L:(g + 1) * L] for g in range(NG)])

    z = w0_ref[...] + _dot_split(_nn, jnp.tanh(lo), w2h_ref[...], w2l_ref[...])
    lw = -RWKV_DECAY_SCALE * jax.nn.sigmoid(z)
    a_sig = jax.nn.sigmoid(a0_ref[...] + _nn(_b(lo), a2_ref[...]))
    yield
    kk = kr * kk_ref[...]
    kk = kk / jnp.maximum(jnp.sqrt(head_sum(kk * kk)), 1e-12)
    kmod = kr * (1.0 + (a_sig - 1.0) * ka_ref[...])
    kb = kk * a_sig
    yield

    trib = _memo(consts, ("trib", L), lambda: _tri_consts(consts, L)[1].astype(BF16))
    l1, l2 = _split(lw)
    cum = _nn(trib, l1) + _nn(trib, l2)
    cumL = cum[L - 1:L, :]
    yield
    p_in = jnp.exp(cum)
    p_inv = 1.0 / p_in
    At = -kk * jnp.exp(cum - lw)
    Bt, Kt, Rt = kb * p_inv, kmod * p_inv, rr * p_in
    PL = jnp.exp(cumL)

    def tri_masks():
        t, i = _iota2((L, 2 * CW), 0), _iota2((L, 2 * CW), 1) % L
        return i < t, i <= t
    strict, incl = _memo(consts, ("rwkv_tri", L, CW), tri_masks)
    bd = _BlockDiag.get(consts, L, HD, RWKV_PACK)

    G = range(NG)
    gsl = [slice(g * GW, (g + 1) * GW) for g in G]
    Sbd = [S_ref[g] for g in G]
    AR = [_cat0(_b(At[:, s]), _b(Rt[:, s])) for s in gsl]
    Bb, Kb, Vb = ([_b(x[:, s]) for s in gsl] for x in (Bt, Kt, vr))
    Bst, Kst, Vst = ([bd.heads(x) for x in xs] for xs in (Bb, Kb, Vb))
    yield
    g1 = [_nt(AR[g], _cat0(Bst[g], Kst[g])) for g in G]
    s1 = [_nt(AR[g], _b(Sbd[g])) for g in G]
    GA = [jnp.where(strict, g1[g][:L], 0.0) for g in G]
    GR = [jnp.where(incl, g1[g][L:], 0.0) for g in G]
    yield
    rhs = [s1[g][:L] + _nn(_b(GA[g][:, CW:]), Vst[g]) for g in G]
    yield
    U = yield from _unit_lower_solve(consts, [GA[g][:, :CW] for g in G], rhs, L, bd)
    yield
    Ub = [_b(x) for x in U]
    ys = [s1[g][L:] + _nn(_b(GR[g]), _cat0(bd.heads(Ub[g]), Vst[g])) for g in G]
    yield
    for g in G:
        upd = _tn(_cat0(Ub[g], Vb[g]), _cat0(Bb[g], Kb[g]))
        PLg = PL[:, gsl[g]]
        for j in range(RWKV_PACK):
            rs = slice(j * HD, (j + 1) * HD)
            lo_ = (j * HD // LANE) * LANE
            ws = slice(lo_, max(lo_ + LANE, (j + 1) * HD))
            S_ref[g, rs, ws] = (Sbd[g][rs, ws] + jnp.where(seg[rs, ws], upd[rs, ws], 0.0)) * PLg[:, ws]
    yield

    y = jnp.concatenate(ys, axis=1)
    mu = head_sum(y) * (1.0 / HD)
    yz = y - mu
    yield
    var = head_sum(yz * yz) * (1.0 / HD)
    yn = yz * lax.rsqrt(var + RWKV_GN_EPS) * lnw_ref[...] + lnb_ref[...]
    bonus = head_sum(rr * kmod * rk_ref[...]) * vr
    y_ref[0, rows, :] = ((yn + bonus) * _silu(rz)).astype(y_ref.dtype)


def _rwkv_spec(rsh, rz, e, sha, shb, S0, p, L, TB):
    B, T = e.shape[:2]
    H, HD = S0.shape[1], S0.shape[2]
    D = H * HD
    GW = RWKV_PACK * HD
    params = [p["r_mua"], p["r_mub"], p["r_w0"], p["r_w2h"], p["r_w2l"], p["r_a0"], p["r_a2"], p["r_kk"], p["r_ka"],
              p["r_rk"], p["r_lnw"], p["r_lnb"]]
    return dict(
        body=functools.partial(_rwkv_part, L=L, H=H, HD=HD), stride=3, stacked=1, n_batched=6,
        inputs=[rsh[0], rz[0], e, sha, shb, S0] + params,
        in_specs=[_time_block(TB, 3 * D, rsh[1]), _time_block(TB, D, rz[1]), _time_block(TB, E_W),
                  _per_b(sha), _per_b(shb), _per_b(S0)]
        + [_full(a) for a in params],
        out_specs=[_time_block(TB, D), _per_b(S0)],
        out_shape=[jax.ShapeDtypeStruct((B, T, D), BF16), jax.ShapeDtypeStruct(S0.shape, F32)],
        scratch=[pltpu.VMEM((SUBLANE, 3 * D), F32), pltpu.VMEM((SUBLANE, LANE), F32),
                 pltpu.VMEM((H // RWKV_PACK, GW, GW), F32)],
    )


def _mixers_kernel(*refs, parts, n_alias, strides, chunks, L):
    R = MIXER_ROWS_PER_STEP
    n_in, n_out = sum(p[1] for p in parts), sum(p[2] for p in parts)
    groups, i, o, s = [], 0, n_in + n_alias, n_in + n_alias + n_out
    for (body, ni, no, ns, nb), stride in zip(parts, strides):
        for r in range(R):
            row = lambda ref: ref.at[r:r + 1]
            ins = tuple(row(x) for x in refs[i:i + nb]) + refs[i + nb:i + ni]
            outs = tuple(row(x) for x in refs[o:o + no])
            groups.append((body, ins + outs + refs[s + r * ns:s + (r + 1) * ns], stride))
        i, o, s = i + ni, o + no, s + R * ns
    c = pl.program_id(1)

    consts = {}

    def run(phase, rows):
        live = [(body(phase, rows, consts, *r), stride) for body, r, stride in groups]
        while live:
            nxt = []
            for gen, stride in live:
                try:
                    for _ in range(stride):
                        next(gen)
                    nxt.append((gen, stride))
                except StopIteration:
                    pass
            live = nxt

    @pl.when(c == 0)
    def _():
        run("init", None)

    for k in range(chunks):
        run("main", slice(k * L, (k + 1) * L))

    @pl.when(c == pl.num_programs(1) - 1)
    def _():
        run("final", None)


def _mixers(specs, B, T, L, TB, layer, depth, stacked):
    R = MIXER_ROWS_PER_STEP
    assert B % R == 0
    parts = tuple((s["body"], len(s["inputs"]), len(s["out_shape"]), len(s["scratch"]), s["n_batched"])
                  for s in specs)
    n_in = sum(p[1] for p in parts)
    out_specs, out_shape, alias_in, aliases = [], [], [], {}
    for s, acc in zip(specs, stacked):
        for j, (spec, shape) in enumerate(zip(s["out_specs"], s["out_shape"])):
            if j == s["stacked"]:
                full = (depth,) + shape.shape
                aliases[n_in + len(alias_in)] = len(out_shape)
                alias_in.append(jnp.zeros(full, shape.dtype) if acc is None else acc)
                spec = pl.BlockSpec((None,) + tuple(spec.block_shape),
                                    lambda b, c, im=spec.index_map: (layer,) + tuple(im(b, c)))
                shape = jax.ShapeDtypeStruct(full, shape.dtype)
            out_specs.append(spec)
            out_shape.append(shape)
    outs = pl.pallas_call(
        functools.partial(_mixers_kernel, parts=parts, n_alias=len(alias_in),
                          strides=tuple(s["stride"] for s in specs), chunks=TB // L, L=L),
        grid=(B // R, T // TB),
        in_specs=[x for s in specs for x in s["in_specs"]] + [pl.BlockSpec(memory_space=pl.ANY)] * len(alias_in),
        out_specs=out_specs,
        out_shape=out_shape,
        scratch_shapes=[x for s in specs for _ in range(R) for x in s["scratch"]],
        input_output_aliases=aliases,
        compiler_params=pltpu.CompilerParams(dimension_semantics=("parallel", "arbitrary"),
                                             vmem_limit_bytes=VMEM_LIMIT_BYTES),
        name="mixers",
    )(*[x for s in specs for x in s["inputs"]], *alias_in)
    res, k = [], 0
    for s in specs:
        res.append(outs[k:k + len(s["out_shape"])])
        k += len(s["out_shape"])
    return res


def _bd_tiles(w):
    dep, nb, qb, _ = w.shape
    rows = w.reshape(dep, nb * qb // MXU_DIM, MXU_DIM, qb)
    idx = jnp.arange(MXU_DIM)
    full = jnp.take(rows, idx % qb, axis=-1)
    return jnp.where((idx[:, None] // qb) == (idx[None, :] // qb), full, 0.0)


def _pad_lanes(a, width=LANE):
    return jnp.pad(a, [(0, 0)] * (a.ndim - 1) + [(0, width - a.shape[-1])])


def _prep_params(D, w_in, b_gate, m_conv_w, m_conv_b, m_wq, m_wk, m_wv, m_b_if, m_norm_w, m_skip, m_w_out,
                 s_conv_w, s_conv_b, s_dt_bias, s_A_log, s_D, s_norm_w, s_w_out,
                 r_mu, r_w0, r_w2, r_a0, r_a2, r_k_k, r_k_a, r_r_k, r_ln_w, r_ln_b, r_w_out, w_out):
    H_S = D // P_S
    DC = D + 2 * G_S * N_S
    n_lo = LORA_W + LORA_A
    assert n_lo == LANE and 2 * H_M <= SUBLANE and H_S <= LANE
    off_gate = 0
    off_m = off_gate + 3 * D
    off_if = off_m + 3 * D
    off_sz = off_if + 2 * H_M
    off_dt = off_sz + D + DC
    off_rsh = off_dt + H_S
    off_rz = off_rsh + 3 * D + n_lo
    assert off_rz + D == w_in.shape[-1]
    col = lambda o, n: w_in[:, :, o:o + n]
    row = lambda a: a[:, None, :]
    cat = lambda *a: jnp.concatenate(a, axis=-1)
    DH = D // H_M
    zl = jnp.zeros_like(r_w2)
    w2_pad = jnp.concatenate([r_w2, zl], axis=1)
    w2_hi = _b(w2_pad)
    return dict(
        wS=_b(cat(col(off_gate, 3 * D), col(off_m + 2 * D, D))),
        bS=row(cat(b_gate, jnp.zeros((w_in.shape[0], D), F32))),
        wL=_b(cat(col(off_m + D, D), col(off_sz, D), col(off_rz, D))),
        wN=_b(cat(col(off_sz + D, DC), col(off_m, D), col(off_rsh, 3 * D))),
        wE=_b(cat(col(off_rsh + 3 * D, n_lo), _pad_lanes(col(off_if, 2 * H_M)), _pad_lanes(col(off_dt, H_S)))),
        m_cw=m_conv_w, m_cb=row(m_conv_b),
        m_wq=_b(_bd_tiles(m_wq)), m_wk=_b(_bd_tiles(m_wk) * DH ** -0.5), m_wv=_b(_bd_tiles(m_wv)),
        bE=row(jnp.concatenate([jnp.zeros((w_in.shape[0], n_lo), F32), _pad_lanes(m_b_if),
                                _pad_lanes(s_dt_bias)], axis=-1)),
        m_nw=row(m_norm_w), m_skip=row(m_skip), m_wo=_b(m_w_out),
        s_cw=s_conv_w, s_cb=row(s_conv_b), s_alog=row(_pad_lanes(s_A_log)),
        s_drow=row(jnp.repeat(s_D, P_S, axis=-1)), s_nw=row(s_norm_w), s_wo=_b(s_w_out),
        r_mua=row(r_mu[:, :3 * D]), r_mub=row(r_mu[:, 3 * D:]),
        r_w0=row(r_w0), r_w2h=w2_hi, r_w2l=_b(w2_pad - w2_hi.astype(F32)),
        r_a0=row(r_a0), r_a2=_b(jnp.concatenate([zl, r_a2], axis=1)),
        r_kk=row(r_k_k), r_ka=row(r_k_a), r_rk=row(r_r_k.reshape(r_r_k.shape[0], -1)),
        r_lnw=row(r_ln_w), r_lnb=row(r_ln_b), r_wo=_b(r_w_out),
        wo=_b(w_out),
    )


def _row_tile(n, cap):
    t = min(n, cap)
    while n % t:
        t //= 2
    return t


def _run(x, states, prep, norm_w, final_norm_w):
    B, T, D = x.shape
    depth = norm_w.shape[0]
    mC, mn, mm, mconv, sS, sconv, rS, rsh = states
    L = CHUNK if T % CHUNK == 0 else T
    TB = L * MIXER_CHUNKS_PER_STEP if T % (L * MIXER_CHUNKS_PER_STEP) == 0 else L
    KW = mconv.shape[2] + 1
    DC = sconv.shape[-1]
    assert T >= KW - 1 and L % SUBLANE == 0
    assert DC % D == 0 and (DC + D) % (3 * D) == 0
    N = B * T
    tm = _row_tile(N, 1024)
    tmo = _row_tile(N, 256)
    x2 = x.reshape(N, D)
    h = _rmsnorm(x2, norm_w[0][None, :], BF16, tm)
    new = [[] for _ in range(5)]
    stacked = (None, None, None)
    for l in range(depth):
        p = {k: v[l] for k, v in prep.items()}
        def proj(w, dt, bias=None):
            nw_ = w.shape[2]
            bias = jnp.zeros((1, nw_), F32) if bias is None else bias
            return _matmul(h, w, bias, l, tm, 1024 if nw_ % 1024 == 0 else nw_, dt)

        uS = proj(prep["wS"], BF16, p["bS"])
        uL = proj(prep["wL"], BF16).reshape(B, T, -1)
        uN = proj(prep["wN"], BF16).reshape(B, T, -1)
        uE = proj(prep["wE"], F32, p["bE"]).reshape(B, T, -1)
        uS3 = uS.reshape(B, T, -1)
        (yr, rS_all), (hm, mC_all, mn1, mm1), (ys, sS_all) = _mixers(
            [_rwkv_spec((uN, (DC + D) // (3 * D)), (uL, 2), uE, rsh[l][..., :3 * D], rsh[l][..., 3 * D:], rS[l], p, L, TB),
             _mlstm_spec((uN, DC // D), (uL, 0), (uS3, 3), uE, mconv[l], mC[l], mn[l], mm[l], p, L, TB),
             _ssd_spec((uL, 1), (uN, 0), uE, sconv[l], sS[l], p, L, TB)],
            B, T, L, TB, l, depth, stacked)
        stacked = (rS_all, mC_all, sS_all)
        mm1 = mm1[..., 0]
        last = l == depth - 1
        nw = (final_norm_w if last else norm_w[l + 1])[None, :]
        x2, h = _out_proj(hm.reshape(N, D), ys.reshape(N, D), yr.reshape(N, D), uS, x2,
                          prep["m_wo"], prep["s_wo"], prep["r_wo"], prep["wo"], l, nw, F32 if last else BF16, tmo)
        tail = uN[:, T - (KW - 1):].astype(F32)
        outs = (mn1, mm1, tail[..., DC:DC + D], tail[..., 0:DC],
                jnp.concatenate([tail[:, KW - 2:, DC + D:], uE[:, T - 1:, E_LO:E_LO + LANE]], axis=-1))
        for acc, s in zip(new, outs):
            acc.append(s)
    mn_all, mm_all, mconv_all, sconv_all, rsh_all = (jnp.stack(s) for s in new)
    rS_all, mC_all, sS_all = stacked
    return h.reshape(B, T, D), [mC_all, mn_all, mm_all, mconv_all, sS_all, sconv_all, rS_all, rsh_all]


def kernel(x_prompt, x_sample, state_mlstm_C, state_mlstm_n, state_mlstm_m, state_mlstm_conv, state_ssd,
           state_ssd_conv, state_rwkv, state_rwkv_shift, norm_w, w_in, b_gate, m_conv_w, m_conv_b, m_wq, m_wk,
           m_wv, m_b_if, m_norm_w, m_skip, m_w_out, s_conv_w, s_conv_b, s_dt_bias, s_A_log, s_D, s_norm_w,
           s_w_out, r_mu, r_w0, r_w2, r_a0, r_a2, r_k_k, r_k_a, r_r_k, r_ln_w, r_ln_b, r_w_out, w_out,
           final_norm_w):
    D = x_prompt.shape[-1]
    prep = _prep_params(D, w_in, b_gate, m_conv_w, m_conv_b, m_wq, m_wk, m_wv, m_b_if, m_norm_w, m_skip, m_w_out,
                        s_conv_w, s_conv_b, s_dt_bias, s_A_log, s_D, s_norm_w, s_w_out,
                        r_mu, r_w0, r_w2, r_a0, r_a2, r_k_k, r_k_a, r_r_k, r_ln_w, r_ln_b, r_w_out, w_out)
    sample_states = (state_mlstm_C, state_mlstm_n, state_mlstm_m, state_mlstm_conv, state_ssd,
                     state_ssd_conv, state_rwkv, state_rwkv_shift)
    Bp = x_prompt.shape[0]
    zero_states = tuple(jnp.zeros(s.shape[:1] + (Bp,) + s.shape[2:], s.dtype) for s in sample_states)
    y_p, st_p = _run(x_prompt, zero_states, prep, norm_w, final_norm_w)
    y_s, st_s = _run(x_sample, sample_states, prep, norm_w, final_norm_w)
    return (y_p, y_s, *st_p, *st_s)
```

```python
import functools

import jax
import jax.numpy as jnp
from jax import lax
from jax.experimental import pallas as pl
from jax.experimental.pallas import tpu as pltpu

F32 = jnp.float32
BF16 = jnp.bfloat16
HI = lax.Precision.HIGHEST

CHUNK = 64
NORM_EPS = 1e-6
H_M = 4
MLSTM_LN_EPS = 1e-5
P_S = 64
G_S = 4
N_S = 128
SSD_GN_EPS = 1e-5
HD_R = 64
LORA_W = 64
LORA_A = 64
RWKV_GN_EPS = 64e-5
RWKV_DECAY_SCALE = 0.6065306597126334
RWKV_WAVE = 4
RWKV_PACK = 4
SOLVE_BASE = 16
MIXER_CHUNKS_PER_STEP = 4
MIXER_ROWS_PER_STEP = 1

LANE = 128
SUBLANE = 8
MXU_DIM = 256
VMEM_LIMIT_BYTES = 56 * 1024 * 1024

E_LO = 0 * LANE
E_IF = 1 * LANE
E_DT = 2 * LANE
E_W = 3 * LANE


def _nn(a, b, prec=None):
    return jnp.dot(a, b, precision=prec, preferred_element_type=F32)


def _nt(a, b, prec=None):
    return lax.dot_general(a, b, (((1,), (1,)), ((), ())), precision=prec, preferred_element_type=F32)


def _tn(a, b, prec=None):
    return lax.dot_general(a, b, (((0,), (0,)), ((), ())), precision=prec, preferred_element_type=F32)


def _b(x):
    return x.astype(BF16)


def _bf16_terms(x, n):
    terms, r = [], x
    for _ in range(n):
        t = _b(r)
        terms.append(t)
        r = r - t.astype(F32)
    return terms


def _split(x):
    return tuple(_bf16_terms(x, 2))


def _cat0(*xs):
    return jnp.concatenate(xs, axis=0)


def _cat1(*xs):
    return jnp.concatenate(xs, axis=1)


def _softplus(x):
    return jnp.maximum(x, 0.0) + jnp.log1p(jnp.exp(-jnp.abs(x)))


def _log_sigmoid(x):
    return -_softplus(-x)


def _silu(x):
    return x * jax.nn.sigmoid(x)


def _iota2(shape, dim):
    return lax.broadcasted_iota(jnp.int32, shape, dim)


def _memo(consts, key, build):
    if key not in consts:
        consts[key] = build()
    return consts[key]


def _tri_consts(consts, L):
    def build():
        r, c = _iota2((L, L), 0), _iota2((L, L), 1)
        causal = c <= r
        return causal, causal.astype(F32), (r <= c).astype(F32)
    return _memo(consts, ("tri", L), build)


def _eye(consts, n):
    return _memo(consts, ("eye", n), lambda: (_iota2((n, n), 0) == _iota2((n, n), 1)).astype(F32))


def _shifted_rows(xp_ref, x, shifts, L):
    xfull = _cat0(xp_ref[...], x)
    xp_ref[...] = x[L - SUBLANE:L, :]
    return [pltpu.roll(xfull, s, 0)[SUBLANE:SUBLANE + L, :] for s in shifts]


def _causal_conv(xp_ref, x, cw_ref, cb_ref, L, KW):
    taps = _shifted_rows(xp_ref, x, list(range(KW - 1, 0, -1)), L) + [x]
    y = cb_ref[...]
    for j in range(KW):
        y = y + taps[j] * cw_ref[j:j + 1, :]
    return y


def _norm_kernel(x_ref, w_ref, o_ref):
    x = x_ref[...]
    y = x * lax.rsqrt(jnp.mean(x * x, axis=-1, keepdims=True) + NORM_EPS)
    o_ref[...] = (y * w_ref[...]).astype(o_ref.dtype)


def _rmsnorm(x, w, out_dtype, tm):
    n, d = x.shape
    return pl.pallas_call(
        _norm_kernel,
        grid=(n // tm,),
        in_specs=[pl.BlockSpec((tm, d), lambda i: (i, 0)), pl.BlockSpec((1, d), lambda i: (0, 0))],
        out_specs=pl.BlockSpec((tm, d), lambda i: (i, 0)),
        out_shape=jax.ShapeDtypeStruct((n, d), out_dtype),
        compiler_params=pltpu.CompilerParams(dimension_semantics=("parallel",), vmem_limit_bytes=VMEM_LIMIT_BYTES),
        name="rmsnorm",
    )(x, w)


def _matmul_kernel(h_ref, w_ref, b_ref, o_ref):
    o_ref[...] = (_nn(h_ref[...], w_ref[...]) + b_ref[...]).astype(o_ref.dtype)


def _matmul(h, w, b, layer, tm, tn, out_dtype):
    n, d = h.shape
    nw = w.shape[2]
    return pl.pallas_call(
        _matmul_kernel,
        grid=(nw // tn, n // tm),
        in_specs=[pl.BlockSpec((tm, d), lambda j, i: (i, 0)),
                  pl.BlockSpec((None, d, tn), lambda j, i: (layer, 0, j)),
                  pl.BlockSpec((1, tn), lambda j, i: (0, j))],
        out_specs=pl.BlockSpec((tm, tn), lambda j, i: (i, j)),
        out_shape=jax.ShapeDtypeStruct((n, nw), out_dtype),
        compiler_params=pltpu.CompilerParams(dimension_semantics=("parallel", "parallel"),
                                             vmem_limit_bytes=VMEM_LIMIT_BYTES),
        name="proj_in",
    )(h, w, b)


def _out_kernel(hm_ref, ys_ref, yr_ref, g_ref, x_ref, wm_ref, ws_ref, wr_ref, wo_ref, nw_ref,
                xo_ref, ho_ref, *, D):
    g = jax.nn.sigmoid(g_ref[...].astype(F32))
    merged = (g[:, 0:D] * _nn(hm_ref[...], wm_ref[...])
              + g[:, D:2 * D] * _nn(ys_ref[...], ws_ref[...])
              + g[:, 2 * D:3 * D] * _nn(yr_ref[...], wr_ref[...]))
    out = x_ref[...] + _nn(_b(merged), wo_ref[...])
    xo_ref[...] = out
    y = out * lax.rsqrt(jnp.mean(out * out, axis=-1, keepdims=True) + NORM_EPS)
    ho_ref[...] = (y * nw_ref[...]).astype(ho_ref.dtype)


def _out_proj(hm, ys, yr, gate, x, wm, ws, wr, wo, layer, nw, h_dtype, tm):
    n, d = x.shape
    row = lambda w: pl.BlockSpec((tm, w), lambda i: (i, 0))
    full = lambda a: pl.BlockSpec(a.shape, lambda i: (0, 0))
    wspec = pl.BlockSpec((None, d, d), lambda i: (layer, 0, 0))
    return pl.pallas_call(
        functools.partial(_out_kernel, D=d),
        grid=(n // tm,),
        in_specs=[row(d), row(d), row(d), row(3 * d), row(d),
                  wspec, wspec, wspec, wspec, full(nw)],
        out_specs=[row(d), row(d)],
        out_shape=[jax.ShapeDtypeStruct((n, d), F32), jax.ShapeDtypeStruct((n, d), h_dtype)],
        compiler_params=pltpu.CompilerParams(dimension_semantics=("parallel",), vmem_limit_bytes=VMEM_LIMIT_BYTES),
        name="out_proj",
    )(hm, ys, yr, gate, x, wm, ws, wr, wo, nw)


def _blockdiag_apply(a, w_ref):
    ab = _b(a)
    return [_nn(ab[:, t * MXU_DIM:(t + 1) * MXU_DIM], w_ref[t]) for t in range(w_ref.shape[0])]


def _mlstm_part(phase, rows, consts, mx_ref, mz_ref, mo_ref, e_ref, conv0_ref, C0_ref, n0_ref, m0_ref,
                cw_ref, cb_ref, wq_ref, wk_ref, wv_ref, nw_ref, skip_ref,
                h_ref, C1_ref, n1_ref, m1_ref,
                xp_ref, C_ref, n_ref, m_ref, *, L, H, DH, KW):
    D = H * DH
    if phase == "init":
        xp_ref[...] = jnp.zeros(xp_ref.shape, F32)
        xp_ref[SUBLANE - (KW - 1):SUBLANE, :] = conv0_ref[0]
        C_ref[...] = C0_ref[0]
        n_ref[...] = n0_ref[0]
        m_ref[...] = m0_ref[0]
        return
    if phase == "final":
        C1_ref[0] = C_ref[...]
        n1_ref[0] = n_ref[...]
        m1_ref[0] = m_ref[...]
        return

    mx, mz, mo = (r[0, rows, :].astype(F32) for r in (mx_ref, mz_ref, mo_ref))
    mc = _silu(_causal_conv(xp_ref, mx, cw_ref, cb_ref, L, KW))
    yield
    q = _cat1(*_blockdiag_apply(mc, wq_ref))
    k = _cat1(*_blockdiag_apply(mc, wk_ref))
    v = _cat1(*_blockdiag_apply(mx, wv_ref))

    causal, tril, triu = _tri_consts(consts, L)
    ecol = e_ref[0, rows, E_IF:E_IF + LANE]
    eyeb, trilb, triub = _memo(consts, ("mlstm_b", L), lambda: (_b(_eye(consts, LANE)), _b(tril), _b(triu)))
    eT = sum(_nt(eyeb, t) for t in _bf16_terms(ecol, 3))[0:SUBLANE, :]
    b_col = sum(_nn(trilb, t) for t in _bf16_terms(_log_sigmoid(ecol), 3))
    b_row = sum(_nn(t, triub) for t in _bf16_terms(_log_sigmoid(eT), 3))

    Hs = range(H)
    sl = [slice(h * DH, (h + 1) * DH) for h in Hs]
    qf, kf = [q[:, s] for s in sl], [k[:, s] for s in sl]
    qh, kh, vh = [_b(x) for x in qf], [_b(x) for x in kf], [_b(v[:, s]) for s in sl]
    bc, br = [b_col[:, H + h:H + h + 1] for h in Hs], [b_row[H + h:H + h + 1, :] for h in Hs]
    ic, ir = [ecol[:, h:h + 1] for h in Hs], [eT[h:h + 1, :] for h in Hs]
    mprev = [m_ref[h:h + 1, 0:1] for h in Hs]
    Ch, nh = [C_ref[h] for h in Hs], [n_ref[h:h + 1, :] for h in Hs]

    yield
    qk = [_nt(qh[h], kh[h]) for h in Hs]
    qC = [_nn(qh[h], _b(Ch[h])) for h in Hs]
    logD = [jnp.where(causal, bc[h] - br[h] + ir[h], -jnp.inf) for h in Hs]
    inter = [bc[h] + mprev[h] for h in Hs]
    m_t = [jnp.maximum(inter[h], jnp.max(logD[h], axis=1, keepdims=True)) for h in Hs]
    sc = [jnp.exp(inter[h] - m_t[h]) for h in Hs]
    s = [qk[h] * jnp.exp(logD[h] - m_t[h]) for h in Hs]
    yield
    sv = [_nn(_b(s[h]), vh[h]) for h in Hs]

    bL = [bc[h][L - 1:L, :] for h in Hs]
    m_new = [jnp.maximum(bL[h] + mprev[h], jnp.max(bL[h] - br[h] + ir[h], axis=1, keepdims=True)) for h in Hs]
    dec = [jnp.exp(bL[h] + mprev[h] - m_new[h]) for h in Hs]
    kw = [kf[h] * jnp.exp(bL[h] - bc[h] + ic[h] - m_new[h]) for h in Hs]
    upd = [_tn(_b(kw[h]), vh[h]) for h in Hs]
    yield

    outs = []
    for h in Hs:
        den = jnp.sum(s[h], axis=1, keepdims=True) + sc[h] * jnp.sum(qf[h] * nh[h], axis=1, keepdims=True)
        hc = (sv[h] + qC[h] * sc[h]) * (1.0 / jnp.maximum(jnp.abs(den), jnp.exp(-m_t[h])))
        mu = jnp.mean(hc, axis=1, keepdims=True)
        hz = hc - mu
        var = jnp.mean(hz * hz, axis=1, keepdims=True)
        outs.append(hz * lax.rsqrt(var + MLSTM_LN_EPS))
        C_ref[h] = dec[h] * Ch[h] + upd[h]
        n_ref[h:h + 1, :] = dec[h] * nh[h] + jnp.sum(kw[h], axis=0, keepdims=True)
        m_ref[h:h + 1, :] = jnp.broadcast_to(m_new[h], (1, LANE))

    hm = jnp.concatenate(outs, axis=1) * nw_ref[...]
    out = (jax.nn.sigmoid(mo) * hm + skip_ref[...] * mc) * _silu(mz)
    h_ref[0, rows, :] = out.astype(h_ref.dtype)


def _per_b(a):
    return pl.BlockSpec((MIXER_ROWS_PER_STEP,) + a.shape[1:], lambda b, c: (b,) + (0,) * (a.ndim - 1))


def _full(a):
    return pl.BlockSpec(a.shape, lambda b, c: (0,) * a.ndim)


def _time_block(tb, w, j=0):
    return pl.BlockSpec((MIXER_ROWS_PER_STEP, tb, w), lambda b, c: (b, c, j))


def _mlstm_spec(mx, mz, mo, e, conv0, C0, n0, m0, p, L, TB):
    B, T = e.shape[:2]
    H, DH = C0.shape[1], C0.shape[2]
    D = H * DH
    KW = p["m_cw"].shape[0]
    m0b = jnp.broadcast_to(m0[..., None], (B, H, LANE))
    params = [p["m_cw"], p["m_cb"], p["m_wq"], p["m_wk"], p["m_wv"], p["m_nw"], p["m_skip"]]
    return dict(
        body=functools.partial(_mlstm_part, L=L, H=H, DH=DH, KW=KW), stride=1, stacked=1, n_batched=8,
        inputs=[mx[0], mz[0], mo[0], e, conv0, C0, n0, m0b] + params,
        in_specs=[_time_block(TB, D, mx[1]), _time_block(TB, D, mz[1]), _time_block(TB, D, mo[1]),
                  _time_block(TB, E_W), _per_b(conv0), _per_b(C0), _per_b(n0), _per_b(m0b)]
        + [_full(a) for a in params],
        out_specs=[_time_block(TB, D), _per_b(C0), _per_b(n0), _per_b(m0b)],
        out_shape=[jax.ShapeDtypeStruct((B, T, D), BF16), jax.ShapeDtypeStruct(C0.shape, F32),
                   jax.ShapeDtypeStruct(n0.shape, F32), jax.ShapeDtypeStruct(m0b.shape, F32)],
        scratch=[pltpu.VMEM((SUBLANE, D), F32), pltpu.VMEM((H, DH, DH), F32),
                 pltpu.VMEM((H, DH), F32), pltpu.VMEM((H, LANE), F32)],
    )


def _ssd_part(phase, rows, consts, sz_ref, xbc_ref, e_ref, conv0_ref, S0_ref, cw_ref, cb_ref, alog_ref, drow_ref, nw_ref,
              xl_ref, xp_exp_ref, y_ref, S1_ref, xp_ref, S_ref, *, L, H, P, G, N, KW):
    D = H * P
    E = H // G
    EL, EP = E * L, E * P
    if phase == "init":
        xp_ref[...] = jnp.zeros(xp_ref.shape, F32)
        xp_ref[SUBLANE - (KW - 1):SUBLANE, :] = conv0_ref[0]
        S_ref[...] = S0_ref[0].reshape(G, EP, N)
        return
    if phase == "final":
        S1_ref[0] = S_ref[...].reshape(H, P, N)
        return

    sz = sz_ref[0, rows, :].astype(F32)
    xbc = _silu(_causal_conv(xp_ref, xbc_ref[0, rows, :].astype(F32), cw_ref, cb_ref, L, KW))
    xs, Bm, Cm = xbc[:, 0:D], xbc[:, D:D + G * N], xbc[:, D + G * N:]
    yield

    dt = _softplus(e_ref[0, rows, E_DT:E_DT + LANE])
    tril = _tri_consts(consts, L)[1]
    trilb = _memo(consts, ("trib", L), lambda: _b(tril))
    cs = sum(_nn(trilb, t) for t in _bf16_terms(dt * (-jnp.exp(alog_ref[...])), 3))
    csL = cs[L - 1:L, :]

    def expand(items, e_ref_):
        rows = [t for x, n in items for t in _bf16_terms(x, n)]
        r = _nn(_cat0(*rows), e_ref_[...])
        outs, k = [], 0
        for _, n in items:
            outs.append(sum(r[j * L:(j + 1) * L] for j in range(k, k + n)))
            k += n
        return outs

    yield
    csx, dtx = expand([(cs, 3), (dt, 2)], xl_ref)
    f_out, f_in = expand([(jnp.exp(cs), 2), (jnp.exp(csL - cs) * dt, 2)], xp_exp_ref)
    def masks():
        row, col = _iota2((L, H * L), 0), _iota2((L, H * L), 1) % L
        return row == col, col <= row
    diag, lower = _memo(consts, ("ssd_masks", L, H), masks)
    cs_src = jnp.sum(jnp.where(diag, csx, 0.0), axis=0, keepdims=True)
    dt_src = jnp.sum(jnp.where(diag, dtx, 0.0), axis=0, keepdims=True)
    Mall = jnp.exp(jnp.where(lower, csx - cs_src, -jnp.inf)) * dt_src
    decay_end = jnp.exp(csL)
    xw = xs * f_in
    bd = _BlockDiag.get(consts, L, P, E)

    Gs = range(G)
    Bg = [_b(Bm[:, g * N:(g + 1) * N]) for g in Gs]
    Cg = [_b(Cm[:, g * N:(g + 1) * N]) for g in Gs]
    Sg = [S_ref[g] for g in Gs]
    yield
    CB = [_nt(Cg[g], _cat0(*[Bg[g]] * E)) for g in Gs]
    CS = [_nt(Cg[g], _b(Sg[g])) for g in Gs]
    Mg = [_b(CB[g] * Mall[:, g * EL:(g + 1) * EL]) for g in Gs]
    yield
    yi = [_nn(Mg[g], bd.heads(_b(xs[:, g * EP:(g + 1) * EP]))) for g in Gs]
    upd = [_tn(_b(xw[:, g * EP:(g + 1) * EP]), Bg[g]) for g in Gs]
    yield
    for g in Gs:
        scale = _cat0(*[jnp.broadcast_to(decay_end[:, h:h + 1], (P, 1)) for h in range(g * E, (g + 1) * E)])
        S_ref[g] = scale * Sg[g] + upd[g]

    yv = (_cat1(*yi) + _cat1(*CS) * f_out + drow_ref[...] * xs) * _silu(sz)
    DG = D // G
    parts = []
    for g in range(G):
        seg = yv[:, g * DG:(g + 1) * DG]
        parts.append(seg * lax.rsqrt(jnp.mean(seg * seg, axis=1, keepdims=True) + SSD_GN_EPS))
    y_ref[0, rows, :] = (jnp.concatenate(parts, axis=1) * nw_ref[...]).astype(y_ref.dtype)


def _lane_expander(n_heads, width):
    return (jnp.arange(LANE)[:, None] == jnp.arange(n_heads * width)[None, :] // width).astype(BF16)


def _ssd_spec(sz, xbc, e, conv0, S0, p, L, TB):
    B, T = e.shape[:2]
    H, P, N = S0.shape[1:]
    D = H * P
    DC = conv0.shape[-1]
    G = (DC - D) // (2 * N)
    KW = p["s_cw"].shape[0]
    params = [p["s_cw"], p["s_cb"], p["s_alog"], p["s_drow"], p["s_nw"],
              _lane_expander(H, L), _lane_expander(H, P)]
    return dict(
        body=functools.partial(_ssd_part, L=L, H=H, P=P, G=G, N=N, KW=KW), stride=1, stacked=1, n_batched=5,
        inputs=[sz[0], xbc[0], e, conv0, S0] + params,
        in_specs=[_time_block(TB, D, sz[1]), _time_block(TB, DC, xbc[1]), _time_block(TB, E_W),
                  _per_b(conv0), _per_b(S0)] + [_full(a) for a in params],
        out_specs=[_time_block(TB, D), _per_b(S0)],
        out_shape=[jax.ShapeDtypeStruct((B, T, D), BF16), jax.ShapeDtypeStruct(S0.shape, F32)],
        scratch=[pltpu.VMEM((SUBLANE, DC), F32), pltpu.VMEM((G, H // G * P, N), F32)],
    )


def _dot_split(dot, x, rhs_hi, rhs_lo):
    xh, xl = _split(x)
    m = x.shape[0]
    r = dot(_cat0(xh, xl), rhs_hi)
    return r[:m] + r[m:] + dot(xh, rhs_lo)


class _BlockDiag:
    @classmethod
    def get(cls, consts, L, HD, n):
        return _memo(consts, ("blockdiag", L, HD, n), lambda: cls(L, HD, n))

    def __init__(self, L, HD, n):
        self.hm = [((_iota2((L, n * HD), 1) // HD) == h).astype(BF16) for h in range(n)]
        self.cm = [((_iota2((L, n * L), 1) // L) == h).astype(BF16) for h in range(n)]

    def heads(self, xb):
        return _cat0(*[xb * m for m in self.hm])

    def cols(self, xb):
        return _cat0(*[xb * m for m in self.cm])

    def heads2(self, x):
        xh, xl = _split(x)
        return self.heads(xh), self.heads(xl)

    def cols2(self, x):
        xh, xl = _split(x)
        return self.cols(xh), self.cols(xl)


def _unit_lower_solve(consts, As, rhss, L, bd):
    base = min(SOLVE_BASE, L)
    W = As[0].shape[1]
    def masks():
        t, i = _iota2(As[0].shape, 0), _iota2(As[0].shape, 1) % L
        return (t // base) == (i // base), (t == i).astype(F32)
    same, eye = _memo(consts, ("solve_masks", As[0].shape, L, base), masks)
    mmc = lambda x, y: _nn(_b(x), bd.cols(_b(y)))
    mmv = lambda x, v: _nn(_b(x), bd.heads(_b(v)))
    Ads = [jnp.where(same, A, 0.0) for A in As]
    Tks = [eye + Ad for Ad in Ads]
    Pks = [mmc(Ad, Ad) for Ad in Ads]
    yield
    n = 2
    while n < base:
        if 2 * n < base:
            rs = [_nn(_b(Pk), _cat1(bd.cols(_b(Tk)), bd.cols(_b(Pk)))) for Tk, Pk in zip(Tks, Pks)]
            Tks = [Tk + r[:, :W] for Tk, r in zip(Tks, rs)]
            Pks = [r[:, W:] for r in rs]
        else:
            Tks = [Tk + mmc(Pk, Tk) for Tk, Pk in zip(Tks, Pks)]
        n *= 2
        yield
    nb = L // base
    if nb == 1:
        return [mmv(Tk, V) for Tk, V in zip(Tks, rhss)]
    HW = rhss[0].shape[1]
    rs = [_nn(_b(Tk), _cat1(bd.heads(_b(V)), bd.cols(_b(jnp.where(same, 0.0, A)))))
          for Tk, V, A in zip(Tks, rhss, As)]
    Xs, Ms = [r[:, :HW] for r in rs], [r[:, HW:] for r in rs]
    n = 1
    while n < nb:
        yield
        if 2 * n < nb:
            rs = [_nn(_b(M), _cat1(bd.heads(_b(X)), bd.cols(_b(M)))) for X, M in zip(Xs, Ms)]
            Xs = [X + r[:, :HW] for X, r in zip(Xs, rs)]
            Ms = [r[:, HW:] for r in rs]
        else:
            Xs = [X + mmv(M, X) for X, M in zip(Xs, Ms)]
        n *= 2
    return Xs


def _rwkv_part(phase, rows, consts, rsh_ref, rz_ref, e_ref, sha_ref, shb_ref, S0_ref,
               mua_ref, mub_ref, w0_ref, w2h_ref, w2l_ref, a0_ref, a2_ref, kk_ref, ka_ref, rk_ref, lnw_ref, lnb_ref,
               y_ref, S1_ref, xpa_ref, xpb_ref, S_ref, *, L, H, HD):
    D = H * HD
    GW = RWKV_PACK * HD
    NG = H // RWKV_PACK
    CW = RWKV_PACK * L
    if phase == "init":
        xpa_ref[...] = jnp.zeros(xpa_ref.shape, F32)
        xpb_ref[...] = jnp.zeros(xpb_ref.shape, F32)
        xpa_ref[SUBLANE - 1:SUBLANE, :] = sha_ref[0]
        xpb_ref[SUBLANE - 1:SUBLANE, :] = shb_ref[0]
        S_ref[...] = jnp.zeros(S_ref.shape, F32)
        for h in range(H):
            g, j = divmod(h, RWKV_PACK)
            S_ref[g, j * HD:(j + 1) * HD, j * HD:(j + 1) * HD] = S0_ref[0, h]
        return
    if phase == "final":
        for h in range(H):
            g, j = divmod(h, RWKV_PACK)
            S1_ref[0, h] = S_ref[g, j * HD:(j + 1) * HD, j * HD:(j + 1) * HD]
        return

    cur = rsh_ref[0, rows, :].astype(F32)
    rz = rz_ref[0, rows, :].astype(F32)
    rx = cur + (_shifted_rows(xpa_ref, cur, [1], L)[0] - cur) * mua_ref[...]
    lo_cur = e_ref[0, rows, E_LO:E_LO + LANE]
    lo = lo_cur + (_shifted_rows(xpb_ref, lo_cur, [1], L)[0] - lo_cur) * mub_ref[...]
    rr, kr, vr = rx[:, 0:D], rx[:, D:2 * D], rx[:, 2 * D:3 * D]
    yield

    def seg_masks():
        seg_ = (_iota2((GW, GW), 0) // HD) == (_iota2((GW, GW), 1) // HD)
        return seg_, seg_.astype(BF16), jnp.where(seg_, 1.0 / HD, 0.0).astype(BF16)
    seg, segb, seg_mean = _memo(consts, ("rwkv_seg", GW, HD), seg_masks)
    assert HD & (HD - 1) == 0

    def head_sum(x, w=segb):
        xb = _b(x)
        r = _nn(_cat0(*[xb[:, g * GW:(g + 1) * GW] for g in range(NG)]), w)
        return _cat1(*[r[g * L:(g + 1) * L] for g in range(NG)])

    z = w0_ref[...] + _dot_split(_nn, jnp.tanh(lo), w2h_ref[...], w2l_ref[...])
    lw = -RWKV_DECAY_SCALE * jax.nn.sigmoid(z)
    a_sig = jax.nn.sigmoid(a0_ref[...] + _nn(_b(lo), a2_ref[...]))
    yield
    kk = kr * kk_ref[...]
    kk = kk / jnp.maximum(jnp.sqrt(head_sum(kk * kk)), 1e-12)
    kmod = kr * (1.0 + (a_sig - 1.0) * ka_ref[...])
    kb = kk * a_sig
    yield

    trib = _memo(consts, ("trib", L), lambda: _tri_consts(consts, L)[1].astype(BF16))
    l1, l2 = _split(lw)
    cum = _nn(trib, l1) + _nn(trib, l2)
    cumL = cum[L - 1:L, :]
    yield
    p_in = jnp.exp(cum)
    p_inv = 1.0 / p_in
    At = -kk * jnp.exp(cum - lw)
    Bt, Kt, Rt = kb * p_inv, kmod * p_inv, rr * p_in
    PL = jnp.exp(cumL)

    def tri_masks():
        t, i = _iota2((L, 2 * CW), 0), _iota2((L, 2 * CW), 1) % L
        return i < t, i <= t
    strict, incl = _memo(consts, ("rwkv_tri", L, CW), tri_masks)
    bd = _BlockDiag.get(consts, L, HD, RWKV_PACK)

    def wave(groups):
        G = range(len(groups))
        gsl = [slice(g * GW, (g + 1) * GW) for g in groups]
        Sbd = [S_ref[g] for g in groups]
        AR = [_cat0(_b(At[:, s]), _b(Rt[:, s])) for s in gsl]
        Bb, Kb, Vb = ([_b(x[:, s]) for s in gsl] for x in (Bt, Kt, vr))
        Bst, Kst, Vst = ([bd.heads(x) for x in xs] for xs in (Bb, Kb, Vb))
        yield
        g1 = [_nt(AR[g], _cat0(Bst[g], Kst[g])) for g in G]
        s1 = [_nt(AR[g], _b(Sbd[g])) for g in G]
        GA = [jnp.where(strict, g1[g][:L], 0.0) for g in G]
        GR = [jnp.where(incl, g1[g][L:], 0.0) for g in G]
        yield
        rhs = [s1[g][:L] + _nn(_b(GA[g][:, CW:]), Vst[g]) for g in G]
        yield
        U = yield from _unit_lower_solve(consts, [GA[g][:, :CW] for g in G], rhs, L, bd)
        yield
        Ub = [_b(x) for x in U]
        ys_ = [s1[g][L:] + _nn(_b(GR[g]), _cat0(bd.heads(Ub[g]), Vst[g])) for g in G]
        yield
        for g in G:
            upd = _tn(_cat0(Ub[g], Vb[g]), _cat0(Bb[g], Kb[g]))
            PLg = PL[:, gsl[g]]
            for j in range(RWKV_PACK):
                rs = slice(j * HD, (j + 1) * HD)
                lo_ = (j * HD // LANE) * LANE
                ws = slice(lo_, max(lo_ + LANE, (j + 1) * HD))
                S_ref[groups[g], rs, ws] = ((Sbd[g][rs, ws] + jnp.where(seg[rs, ws], upd[rs, ws], 0.0))
                                            * PLg[:, ws])
        yield
        return ys_

    ys = []
    for first in range(0, NG, RWKV_WAVE):
        ys += yield from wave(list(range(first, min(first + RWKV_WAVE, NG))))

    y = jnp.concatenate(ys, axis=1)
    mu = head_sum(y, seg_mean)
    yz = y - mu
    yield
    var = head_sum(yz * yz, seg_mean)
    yn = yz * lax.rsqrt(var + RWKV_GN_EPS) * lnw_ref[...] + lnb_ref[...]
    bonus = head_sum(rr * kmod * rk_ref[...]) * vr
    y_ref[0, rows, :] = ((yn + bonus) * _silu(rz)).astype(y_ref.dtype)


def _rwkv_spec(rsh, rz, e, sha, shb, S0, p, L, TB):
    B, T = e.shape[:2]
    H, HD = S0.shape[1], S0.shape[2]
    D = H * HD
    GW = RWKV_PACK * HD
    params = [p["r_mua"], p["r_mub"], p["r_w0"], p["r_w2h"], p["r_w2l"], p["r_a0"], p["r_a2"], p["r_kk"], p["r_ka"],
              p["r_rk"], p["r_lnw"], p["r_lnb"]]
    return dict(
        body=functools.partial(_rwkv_part, L=L, H=H, HD=HD), stride=3, stacked=1, n_batched=6,
        inputs=[rsh[0], rz[0], e, sha, shb, S0] + params,
        in_specs=[_time_block(TB, 3 * D, rsh[1]), _time_block(TB, D, rz[1]), _time_block(TB, E_W),
                  _per_b(sha), _per_b(shb), _per_b(S0)]
        + [_full(a) for a in params],
        out_specs=[_time_block(TB, D), _per_b(S0)],
        out_shape=[jax.ShapeDtypeStruct((B, T, D), BF16), jax.ShapeDtypeStruct(S0.shape, F32)],
        scratch=[pltpu.VMEM((SUBLANE, 3 * D), F32), pltpu.VMEM((SUBLANE, LANE), F32),
                 pltpu.VMEM((H // RWKV_PACK, GW, GW), F32)],
    )


def _mixers_kernel(*refs, parts, n_alias, strides, chunks, L):
    R = MIXER_ROWS_PER_STEP
    n_in, n_out = sum(p[1] for p in parts), sum(p[2] for p in parts)
    groups, i, o, s = [], 0, n_in + n_alias, n_in + n_alias + n_out
    for (body, ni, no, ns, nb), stride in zip(parts, strides):
        for r in range(R):
            row = lambda ref: ref.at[r:r + 1]
            ins = tuple(row(x) for x in refs[i:i + nb]) + refs[i + nb:i + ni]
            outs = tuple(row(x) for x in refs[o:o + no])
            groups.append((body, ins + outs + refs[s + r * ns:s + (r + 1) * ns], stride))
        i, o, s = i + ni, o + no, s + R * ns
    c = pl.program_id(1)

    consts = {}

    def run(phase, rows):
        live = [(body(phase, rows, consts, *r), stride) for body, r, stride in groups]
        while live:
            nxt = []
            for gen, stride in live:
                try:
                    for _ in range(stride):
                        next(gen)
                    nxt.append((gen, stride))
                except StopIteration:
                    pass
            live = nxt

    @pl.when(c == 0)
    def _():
        run("init", None)

    for k in range(chunks):
        run("main", slice(k * L, (k + 1) * L))

    @pl.when(c == pl.num_programs(1) - 1)
    def _():
        run("final", None)


def _mixers(specs, B, T, L, TB, layer, depth, stacked):
    R = MIXER_ROWS_PER_STEP
    assert B % R == 0
    parts = tuple((s["body"], len(s["inputs"]), len(s["out_shape"]), len(s["scratch"]), s["n_batched"])
                  for s in specs)
    n_in = sum(p[1] for p in parts)
    out_specs, out_shape, alias_in, aliases = [], [], [], {}
    for s, acc in zip(specs, stacked):
        for j, (spec, shape) in enumerate(zip(s["out_specs"], s["out_shape"])):
            if j == s["stacked"]:
                full = (depth,) + shape.shape
                aliases[n_in + len(alias_in)] = len(out_shape)
                alias_in.append(jnp.zeros(full, shape.dtype) if acc is None else acc)
                spec = pl.BlockSpec((None,) + tuple(spec.block_shape),
                                    lambda b, c, im=spec.index_map: (layer,) + tuple(im(b, c)))
                shape = jax.ShapeDtypeStruct(full, shape.dtype)
            out_specs.append(spec)
            out_shape.append(shape)
    outs = pl.pallas_call(
        functools.partial(_mixers_kernel, parts=parts, n_alias=len(alias_in),
                          strides=tuple(s["stride"] for s in specs), chunks=TB // L, L=L),
        grid=(B // R, T // TB),
        in_specs=[x for s in specs for x in s["in_specs"]] + [pl.BlockSpec(memory_space=pl.ANY)] * len(alias_in),
        out_specs=out_specs,
        out_shape=out_shape,
        scratch_shapes=[x for s in specs for _ in range(R) for x in s["scratch"]],
        input_output_aliases=aliases,
        compiler_params=pltpu.CompilerParams(dimension_semantics=("parallel", "arbitrary"),
                                             vmem_limit_bytes=VMEM_LIMIT_BYTES),
        name="mixers",
    )(*[x for s in specs for x in s["inputs"]], *alias_in)
    res, k = [], 0
    for s in specs:
        res.append(outs[k:k + len(s["out_shape"])])
        k += len(s["out_shape"])
    return res


def _bd_tiles(w):
    dep, nb, qb, _ = w.shape
    rows = w.reshape(dep, nb * qb // MXU_DIM, MXU_DIM, qb)
    idx = jnp.arange(MXU_DIM)
    full = jnp.take(rows, idx % qb, axis=-1)
    return jnp.where((idx[:, None] // qb) == (idx[None, :] // qb), full, 0.0)


def _pad_lanes(a, width=LANE):
    return jnp.pad(a, [(0, 0)] * (a.ndim - 1) + [(0, width - a.shape[-1])])


def _prep_params(D, w_in, b_gate, m_conv_w, m_conv_b, m_wq, m_wk, m_wv, m_b_if, m_norm_w, m_skip, m_w_out,
                 s_conv_w, s_conv_b, s_dt_bias, s_A_log, s_D, s_norm_w, s_w_out,
                 r_mu, r_w0, r_w2, r_a0, r_a2, r_k_k, r_k_a, r_r_k, r_ln_w, r_ln_b, r_w_out, w_out):
    H_S = D // P_S
    DC = D + 2 * G_S * N_S
    n_lo = LORA_W + LORA_A
    assert n_lo == LANE and 2 * H_M <= SUBLANE and H_S <= LANE
    off_gate = 0
    off_m = off_gate + 3 * D
    off_if = off_m + 3 * D
    off_sz = off_if + 2 * H_M
    off_dt = off_sz + D + DC
    off_rsh = off_dt + H_S
    off_rz = off_rsh + 3 * D + n_lo
    assert off_rz + D == w_in.shape[-1]
    col = lambda o, n: w_in[:, :, o:o + n]
    row = lambda a: a[:, None, :]
    cat = lambda *a: jnp.concatenate(a, axis=-1)
    DH = D // H_M
    zl = jnp.zeros_like(r_w2)
    w2_pad = jnp.concatenate([r_w2, zl], axis=1)
    w2_hi = _b(w2_pad)
    return dict(
        wS=_b(cat(col(off_gate, 3 * D), col(off_m + 2 * D, D))),
        bS=row(cat(b_gate, jnp.zeros((w_in.shape[0], D), F32))),
        wL=_b(cat(col(off_m + D, D), col(off_sz, D), col(off_rz, D))),
        wN=_b(cat(col(off_sz + D, DC), col(off_m, D), col(off_rsh, 3 * D))),
        wE=_b(cat(col(off_rsh + 3 * D, n_lo), _pad_lanes(col(off_if, 2 * H_M)), _pad_lanes(col(off_dt, H_S)))),
        m_cw=m_conv_w, m_cb=row(m_conv_b),
        m_wq=_b(_bd_tiles(m_wq)), m_wk=_b(_bd_tiles(m_wk) * DH ** -0.5), m_wv=_b(_bd_tiles(m_wv)),
        bE=row(jnp.concatenate([jnp.zeros((w_in.shape[0], n_lo), F32), _pad_lanes(m_b_if),
                                _pad_lanes(s_dt_bias)], axis=-1)),
        m_nw=row(m_norm_w), m_skip=row(m_skip), m_wo=_b(m_w_out),
        s_cw=s_conv_w, s_cb=row(s_conv_b), s_alog=row(_pad_lanes(s_A_log)),
        s_drow=row(jnp.repeat(s_D, P_S, axis=-1)), s_nw=row(s_norm_w), s_wo=_b(s_w_out),
        r_mua=row(r_mu[:, :3 * D]), r_mub=row(r_mu[:, 3 * D:]),
        r_w0=row(r_w0), r_w2h=w2_hi, r_w2l=_b(w2_pad - w2_hi.astype(F32)),
        r_a0=row(r_a0), r_a2=_b(jnp.concatenate([zl, r_a2], axis=1)),
        r_kk=row(r_k_k), r_ka=row(r_k_a), r_rk=row(r_r_k.reshape(r_r_k.shape[0], -1)),
        r_lnw=row(r_ln_w), r_lnb=row(r_ln_b), r_wo=_b(r_w_out),
        wo=_b(w_out),
    )


def _row_tile(n, cap):
    t = min(n, cap)
    while n % t:
        t //= 2
    return t


def _run(x, states, prep, norm_w, final_norm_w):
    B, T, D = x.shape
    depth = norm_w.shape[0]
    mC, mn, mm, mconv, sS, sconv, rS, rsh = states
    L = CHUNK if T % CHUNK == 0 else T
    TB = L * MIXER_CHUNKS_PER_STEP if T % (L * MIXER_CHUNKS_PER_STEP) == 0 else L
    KW = mconv.shape[2] + 1
    DC = sconv.shape[-1]
    assert T >= KW - 1 and L % SUBLANE == 0
    assert DC % D == 0 and (DC + D) % (3 * D) == 0
    N = B * T
    tm = _row_tile(N, 2048)
    tmo = _row_tile(N, 512)
    x2 = x.reshape(N, D)
    h = _rmsnorm(x2, norm_w[0][None, :], BF16, tm)
    new = [[] for _ in range(5)]
    stacked = (None, None, None)
    for l in range(depth):
        p = {k: v[l] for k, v in prep.items()}
        def proj(w, dt, bias=None):
            nw_ = w.shape[2]
            bias = jnp.zeros((1, nw_), F32) if bias is None else bias
            return _matmul(h, w, bias, l, tm, 1024 if nw_ % 1024 == 0 else nw_, dt)

        uS = proj(prep["wS"], BF16, p["bS"])
        uL = proj(prep["wL"], BF16).reshape(B, T, -1)
        uN = proj(prep["wN"], BF16).reshape(B, T, -1)
        uE = proj(prep["wE"], F32, p["bE"]).reshape(B, T, -1)
        uS3 = uS.reshape(B, T, -1)
        (yr, rS_all), (hm, mC_all, mn1, mm1), (ys, sS_all) = _mixers(
            [_rwkv_spec((uN, (DC + D) // (3 * D)), (uL, 2), uE, rsh[l][..., :3 * D], rsh[l][..., 3 * D:], rS[l], p, L, TB),
             _mlstm_spec((uN, DC // D), (uL, 0), (uS3, 3), uE, mconv[l], mC[l], mn[l], mm[l], p, L, TB),
             _ssd_spec((uL, 1), (uN, 0), uE, sconv[l], sS[l], p, L, TB)],
            B, T, L, TB, l, depth, stacked)
        stacked = (rS_all, mC_all, sS_all)
        mm1 = mm1[..., 0]
        last = l == depth - 1
        nw = (final_norm_w if last else norm_w[l + 1])[None, :]
        x2, h = _out_proj(hm.reshape(N, D), ys.reshape(N, D), yr.reshape(N, D), uS, x2,
                          prep["m_wo"], prep["s_wo"], prep["r_wo"], prep["wo"], l, nw, F32 if last else BF16, tmo)
        tail = uN[:, T - (KW - 1):].astype(F32)
        outs = (mn1, mm1, tail[..., DC:DC + D], tail[..., 0:DC],
                jnp.concatenate([tail[:, KW - 2:, DC + D:], uE[:, T - 1:, E_LO:E_LO + LANE]], axis=-1))
        for acc, s in zip(new, outs):
            acc.append(s)
    mn_all, mm_all, mconv_all, sconv_all, rsh_all = (jnp.stack(s) for s in new)
    rS_all, mC_all, sS_all = stacked
    return h.reshape(B, T, D), [mC_all, mn_all, mm_all, mconv_all, sS_all, sconv_all, rS_all, rsh_all]


def kernel(x_prompt, x_sample, state_mlstm_C, state_mlstm_n, state_mlstm_m, state_mlstm_conv, state_ssd,
           state_ssd_conv, state_rwkv, state_rwkv_shift, norm_w, w_in, b_gate, m_conv_w, m_conv_b, m_wq, m_wk,
           m_wv, m_b_if, m_norm_w, m_skip, m_w_out, s_conv_w, s_conv_b, s_dt_bias, s_A_log, s_D, s_norm_w,
           s_w_out, r_mu, r_w0, r_w2, r_a0, r_a2, r_k_k, r_k_a, r_r_k, r_ln_w, r_ln_b, r_w_out, w_out,
           final_norm_w):
    D = x_prompt.shape[-1]
    prep = _prep_params(D, w_in, b_gate, m_conv_w, m_conv_b, m_wq, m_wk, m_wv, m_b_if, m_norm_w, m_skip, m_w_out,
                        s_conv_w, s_conv_b, s_dt_bias, s_A_log, s_D, s_norm_w, s_w_out,
                        r_mu, r_w0, r_w2, r_a0, r_a2, r_k_k, r_k_a, r_r_k, r_ln_w, r_ln_b, r_w_out, w_out)
    sample_states = (state_mlstm_C, state_mlstm_n, state_mlstm_m, state_mlstm_conv, state_ssd,
                     state_ssd_conv, state_rwkv, state_rwkv_shift)
    Bp = x_prompt.shape[0]
    zero_states = tuple(jnp.zeros(s.shape[:1] + (Bp,) + s.shape[2:], s.dtype) for s in sample_states)
    y_p, st_p = _run(x_prompt, zero_states, prep, norm_w, final_norm_w)
    y_s, st_s = _run(x_sample, sample_states, prep, norm_w, final_norm_w)
    return (y_p, y_s, *st_p, *st_s)
```

```python
import functools

import jax
import jax.numpy as jnp
from jax import lax
from jax.experimental import pallas as pl
from jax.experimental.pallas import tpu as pltpu

F32 = jnp.float32
BF16 = jnp.bfloat16

CHUNK = 64
NORM_EPS = 1e-6
H_M = 4
MLSTM_LN_EPS = 1e-5
P_S = 64
G_S = 4
N_S = 128
SSD_GN_EPS = 1e-5
HD_R = 64
LORA_W = 64
LORA_A = 64
RWKV_GN_EPS = 64e-5
RWKV_DECAY_SCALE = 0.6065306597126334
RWKV_WAVE = 4
RWKV_PACK = 4
SOLVE_BASE = 16
MIXER_CHUNKS_PER_STEP = 4
MIXER_ROWS_SHORT_SEQ = 4

LANE = 128
SUBLANE = 8
MXU_DIM = 256
VMEM_LIMIT_BYTES = 56 * 1024 * 1024

E_LO = 0 * LANE
E_IF = 1 * LANE
E_DT = 2 * LANE
E_W = 3 * LANE


def _nn(a, b, prec=None):
    return jnp.dot(a, b, precision=prec, preferred_element_type=F32)


def _nt(a, b, prec=None):
    return lax.dot_general(a, b, (((1,), (1,)), ((), ())), precision=prec, preferred_element_type=F32)


def _tn(a, b, prec=None):
    return lax.dot_general(a, b, (((0,), (0,)), ((), ())), precision=prec, preferred_element_type=F32)


def _b(x):
    return x.astype(BF16)


def _bf16_terms(x, n):
    terms, r = [], x
    for _ in range(n):
        t = _b(r)
        terms.append(t)
        r = r - t.astype(F32)
    return terms


def _split(x):
    return tuple(_bf16_terms(x, 2))


def _cat0(*xs):
    return jnp.concatenate(xs, axis=0)


def _cat1(*xs):
    return jnp.concatenate(xs, axis=1)


def _softplus(x):
    return jnp.maximum(x, 0.0) + jnp.log1p(jnp.exp(-jnp.abs(x)))


def _log_sigmoid(x):
    return -_softplus(-x)


def _silu(x):
    return x * jax.nn.sigmoid(x)


def _iota2(shape, dim):
    return lax.broadcasted_iota(jnp.int32, shape, dim)


def _memo(consts, key, build):
    if key not in consts:
        consts[key] = build()
    return consts[key]


def _tri_consts(consts, L):
    def build():
        r, c = _iota2((L, L), 0), _iota2((L, L), 1)
        causal = c <= r
        return causal, causal.astype(F32), (r <= c).astype(F32)
    return _memo(consts, ("tri", L), build)


def _eye(consts, n):
    return _memo(consts, ("eye", n), lambda: (_iota2((n, n), 0) == _iota2((n, n), 1)).astype(F32))


def _shifted_rows(xp_ref, x, shifts, L):
    xfull = _cat0(xp_ref[...], x)
    xp_ref[...] = x[L - SUBLANE:L, :]
    return [pltpu.roll(xfull, s, 0)[SUBLANE:SUBLANE + L, :] for s in shifts]


def _causal_conv(xp_ref, x, cw_ref, cb_ref, L, KW):
    taps = _shifted_rows(xp_ref, x, list(range(KW - 1, 0, -1)), L) + [x]
    y = cb_ref[...]
    for j in range(KW):
        y = y + taps[j] * cw_ref[j:j + 1, :]
    return y


def _norm_kernel(x_ref, w_ref, o_ref):
    x = x_ref[...]
    y = x * lax.rsqrt(jnp.mean(x * x, axis=-1, keepdims=True) + NORM_EPS)
    o_ref[...] = (y * w_ref[...]).astype(o_ref.dtype)


def _rmsnorm(x, w, out_dtype, tm):
    n, d = x.shape
    return pl.pallas_call(
        _norm_kernel,
        grid=(n // tm,),
        in_specs=[pl.BlockSpec((tm, d), lambda i: (i, 0)), pl.BlockSpec((1, d), lambda i: (0, 0))],
        out_specs=pl.BlockSpec((tm, d), lambda i: (i, 0)),
        out_shape=jax.ShapeDtypeStruct((n, d), out_dtype),
        compiler_params=pltpu.CompilerParams(dimension_semantics=("parallel",), vmem_limit_bytes=VMEM_LIMIT_BYTES),
        name="rmsnorm",
    )(x, w)


def _matmul_kernel(h_ref, w_ref, b_ref, o_ref):
    o_ref[...] = (_nn(h_ref[...], w_ref[...]) + b_ref[...]).astype(o_ref.dtype)


def _matmul(h, w, b, layer, tm, tn, out_dtype):
    n, d = h.shape
    nw = w.shape[2]
    return pl.pallas_call(
        _matmul_kernel,
        grid=(nw // tn, n // tm),
        in_specs=[pl.BlockSpec((tm, d), lambda j, i: (i, 0)),
                  pl.BlockSpec((None, d, tn), lambda j, i: (layer, 0, j)),
                  pl.BlockSpec((1, tn), lambda j, i: (0, j))],
        out_specs=pl.BlockSpec((tm, tn), lambda j, i: (i, j)),
        out_shape=jax.ShapeDtypeStruct((n, nw), out_dtype),
        compiler_params=pltpu.CompilerParams(dimension_semantics=("parallel", "parallel"),
                                             vmem_limit_bytes=VMEM_LIMIT_BYTES),
        name="proj_in",
    )(h, w, b)


def _out_kernel(hm_ref, ys_ref, yr_ref, g_ref, x_ref, wm_ref, ws_ref, wr_ref, wo_ref, nw_ref,
                xo_ref, ho_ref, *, D):
    g = jax.nn.sigmoid(g_ref[...].astype(F32))
    merged = (g[:, 0:D] * _nn(hm_ref[...], wm_ref[...])
              + g[:, D:2 * D] * _nn(ys_ref[...], ws_ref[...])
              + g[:, 2 * D:3 * D] * _nn(yr_ref[...], wr_ref[...]))
    out = x_ref[...] + _nn(_b(merged), wo_ref[...])
    xo_ref[...] = out
    y = out * lax.rsqrt(jnp.mean(out * out, axis=-1, keepdims=True) + NORM_EPS)
    ho_ref[...] = (y * nw_ref[...]).astype(ho_ref.dtype)


def _out_proj(hm, ys, yr, gate, x, wm, ws, wr, wo, layer, nw, h_dtype, tm):
    n, d = x.shape
    row = lambda w: pl.BlockSpec((tm, w), lambda i: (i, 0))
    full = lambda a: pl.BlockSpec(a.shape, lambda i: (0, 0))
    wspec = pl.BlockSpec((None, d, d), lambda i: (layer, 0, 0))
    return pl.pallas_call(
        functools.partial(_out_kernel, D=d),
        grid=(n // tm,),
        in_specs=[row(d), row(d), row(d), row(3 * d), row(d),
                  wspec, wspec, wspec, wspec, full(nw)],
        out_specs=[row(d), row(d)],
        out_shape=[jax.ShapeDtypeStruct((n, d), F32), jax.ShapeDtypeStruct((n, d), h_dtype)],
        compiler_params=pltpu.CompilerParams(dimension_semantics=("parallel",), vmem_limit_bytes=VMEM_LIMIT_BYTES),
        name="out_proj",
    )(hm, ys, yr, gate, x, wm, ws, wr, wo, nw)


def _blockdiag_apply(a, w_ref):
    ab = _b(a)
    return [_nn(ab[:, t * MXU_DIM:(t + 1) * MXU_DIM], w_ref[t]) for t in range(w_ref.shape[0])]


def _mlstm_part(phase, rows, consts, mx_ref, mz_ref, mo_ref, e_ref, conv0_ref, C0_ref, n0_ref, m0_ref,
                cw_ref, cb_ref, wq_ref, wk_ref, wv_ref, nw_ref, skip_ref,
                h_ref, C1_ref, n1_ref, m1_ref,
                xp_ref, C_ref, n_ref, m_ref, *, L, H, DH, KW):
    D = H * DH
    if phase == "init":
        xp_ref[...] = jnp.zeros(xp_ref.shape, F32)
        xp_ref[SUBLANE - (KW - 1):SUBLANE, :] = conv0_ref[0]
        C_ref[...] = C0_ref[0]
        n_ref[...] = n0_ref[0]
        m_ref[...] = m0_ref[0]
        return
    if phase == "final":
        C1_ref[0] = C_ref[...]
        n1_ref[0] = n_ref[...]
        m1_ref[0] = m_ref[...]
        return

    mx, mz, mo = (r[0, rows, :].astype(F32) for r in (mx_ref, mz_ref, mo_ref))
    mc = _silu(_causal_conv(xp_ref, mx, cw_ref, cb_ref, L, KW))
    yield
    q = _cat1(*_blockdiag_apply(mc, wq_ref))
    k = _cat1(*_blockdiag_apply(mc, wk_ref))
    v = _cat1(*_blockdiag_apply(mx, wv_ref))

    causal, tril, triu = _tri_consts(consts, L)
    ecol = e_ref[0, rows, E_IF:E_IF + LANE]
    eyeb, trilb, triub = _memo(consts, ("mlstm_b", L), lambda: (_b(_eye(consts, LANE)), _b(tril), _b(triu)))
    eT = sum(_nt(eyeb, t) for t in _bf16_terms(ecol, 3))[0:SUBLANE, :]
    b_col = sum(_nn(trilb, t) for t in _bf16_terms(_log_sigmoid(ecol), 3))
    b_row = sum(_nn(t, triub) for t in _bf16_terms(_log_sigmoid(eT), 3))

    Hs = range(H)
    sl = [slice(h * DH, (h + 1) * DH) for h in Hs]
    qf, kf = [q[:, s] for s in sl], [k[:, s] for s in sl]
    qh, kh, vh = [_b(x) for x in qf], [_b(x) for x in kf], [_b(v[:, s]) for s in sl]
    bc, br = [b_col[:, H + h:H + h + 1] for h in Hs], [b_row[H + h:H + h + 1, :] for h in Hs]
    ic, ir = [ecol[:, h:h + 1] for h in Hs], [eT[h:h + 1, :] for h in Hs]
    mprev = [m_ref[h:h + 1, 0:1] for h in Hs]
    Ch, nh = [C_ref[h] for h in Hs], [n_ref[h:h + 1, :] for h in Hs]

    yield
    qk = [_nt(qh[h], kh[h]) for h in Hs]
    qC = [_nn(qh[h], _b(Ch[h])) for h in Hs]
    logD = [jnp.where(causal, bc[h] - br[h] + ir[h], -jnp.inf) for h in Hs]
    inter = [bc[h] + mprev[h] for h in Hs]
    m_t = [jnp.maximum(inter[h], jnp.max(logD[h], axis=1, keepdims=True)) for h in Hs]
    sc = [jnp.exp(inter[h] - m_t[h]) for h in Hs]
    s = [qk[h] * jnp.exp(logD[h] - m_t[h]) for h in Hs]
    yield
    sv = [_nn(_b(s[h]), vh[h]) for h in Hs]

    bL = [bc[h][L - 1:L, :] for h in Hs]
    m_new = [jnp.maximum(bL[h] + mprev[h], jnp.max(bL[h] - br[h] + ir[h], axis=1, keepdims=True)) for h in Hs]
    dec = [jnp.exp(bL[h] + mprev[h] - m_new[h]) for h in Hs]
    kw = [kf[h] * jnp.exp(bL[h] - bc[h] + ic[h] - m_new[h]) for h in Hs]
    upd = [_tn(_b(kw[h]), vh[h]) for h in Hs]
    yield

    outs = []
    for h in Hs:
        den = jnp.sum(s[h], axis=1, keepdims=True) + sc[h] * jnp.sum(qf[h] * nh[h], axis=1, keepdims=True)
        hc = (sv[h] + qC[h] * sc[h]) * (1.0 / jnp.maximum(jnp.abs(den), jnp.exp(-m_t[h])))
        mu = jnp.mean(hc, axis=1, keepdims=True)
        hz = hc - mu
        var = jnp.mean(hz * hz, axis=1, keepdims=True)
        outs.append(hz * lax.rsqrt(var + MLSTM_LN_EPS))
        C_ref[h] = dec[h] * Ch[h] + upd[h]
        n_ref[h:h + 1, :] = dec[h] * nh[h] + jnp.sum(kw[h], axis=0, keepdims=True)
        m_ref[h:h + 1, :] = jnp.broadcast_to(m_new[h], (1, LANE))

    hm = jnp.concatenate(outs, axis=1) * nw_ref[...]
    out = (jax.nn.sigmoid(mo) * hm + skip_ref[...] * mc) * _silu(mz)
    h_ref[0, rows, :] = out.astype(h_ref.dtype)


def _per_b(a):
    return pl.BlockSpec((1,) + a.shape[1:], lambda b, c: (b,) + (0,) * (a.ndim - 1))


def _full(a):
    return pl.BlockSpec(a.shape, lambda b, c: (0,) * a.ndim)


def _time_block(tb, w, j=0):
    return pl.BlockSpec((1, tb, w), lambda b, c: (b, c, j))


def _mlstm_spec(mx, mz, mo, e, conv0, C0, n0, m0, p, L, TB):
    B, T = e.shape[:2]
    H, DH = C0.shape[1], C0.shape[2]
    D = H * DH
    KW = p["m_cw"].shape[0]
    m0b = jnp.broadcast_to(m0[..., None], (B, H, LANE))
    params = [p["m_cw"], p["m_cb"], p["m_wq"], p["m_wk"], p["m_wv"], p["m_nw"], p["m_skip"]]
    return dict(
        body=functools.partial(_mlstm_part, L=L, H=H, DH=DH, KW=KW), stride=1, stacked=1, n_batched=8,
        inputs=[mx[0], mz[0], mo[0], e, conv0, C0, n0, m0b] + params,
        in_specs=[_time_block(TB, D, mx[1]), _time_block(TB, D, mz[1]), _time_block(TB, D, mo[1]),
                  _time_block(TB, E_W), _per_b(conv0), _per_b(C0), _per_b(n0), _per_b(m0b)]
        + [_full(a) for a in params],
        out_specs=[_time_block(TB, D), _per_b(C0), _per_b(n0), _per_b(m0b)],
        out_shape=[jax.ShapeDtypeStruct((B, T, D), BF16), jax.ShapeDtypeStruct(C0.shape, F32),
                   jax.ShapeDtypeStruct(n0.shape, F32), jax.ShapeDtypeStruct(m0b.shape, F32)],
        scratch=[pltpu.VMEM((SUBLANE, D), F32), pltpu.VMEM((H, DH, DH), F32),
                 pltpu.VMEM((H, DH), F32), pltpu.VMEM((H, LANE), F32)],
    )


def _ssd_part(phase, rows, consts, sz_ref, xbc_ref, e_ref, conv0_ref, S0_ref, cw_ref, cb_ref, alog_ref, drow_ref, nw_ref,
              xl_ref, xp_exp_ref, y_ref, S1_ref, xp_ref, S_ref, *, L, H, P, G, N, KW):
    D = H * P
    E = H // G
    EL, EP = E * L, E * P
    if phase == "init":
        xp_ref[...] = jnp.zeros(xp_ref.shape, F32)
        xp_ref[SUBLANE - (KW - 1):SUBLANE, :] = conv0_ref[0]
        S_ref[...] = S0_ref[0].reshape(G, EP, N)
        return
    if phase == "final":
        S1_ref[0] = S_ref[...].reshape(H, P, N)
        return

    sz = sz_ref[0, rows, :].astype(F32)
    xbc = _silu(_causal_conv(xp_ref, xbc_ref[0, rows, :].astype(F32), cw_ref, cb_ref, L, KW))
    xs, Bm, Cm = xbc[:, 0:D], xbc[:, D:D + G * N], xbc[:, D + G * N:]
    yield

    dt = _softplus(e_ref[0, rows, E_DT:E_DT + LANE])
    tril = _tri_consts(consts, L)[1]
    trilb = _memo(consts, ("trib", L), lambda: _b(tril))
    cs = sum(_nn(trilb, t) for t in _bf16_terms(dt * (-jnp.exp(alog_ref[...])), 3))
    csL = cs[L - 1:L, :]

    def expand(items, e_ref_):
        rows = [t for x, n in items for t in _bf16_terms(x, n)]
        r = _nn(_cat0(*rows), e_ref_[...])
        outs, k = [], 0
        for _, n in items:
            outs.append(sum(r[j * L:(j + 1) * L] for j in range(k, k + n)))
            k += n
        return outs

    yield
    csx, dtx = expand([(cs, 3), (dt, 2)], xl_ref)
    f_out, f_in = expand([(jnp.exp(cs), 2), (jnp.exp(csL - cs) * dt, 2)], xp_exp_ref)
    def masks():
        row, col = _iota2((L, H * L), 0), _iota2((L, H * L), 1) % L
        return row == col, col <= row
    diag, lower = _memo(consts, ("ssd_masks", L, H), masks)
    cs_src = jnp.sum(jnp.where(diag, csx, 0.0), axis=0, keepdims=True)
    dt_src = jnp.sum(jnp.where(diag, dtx, 0.0), axis=0, keepdims=True)
    Mall = jnp.exp(jnp.where(lower, csx - cs_src, -jnp.inf)) * dt_src
    decay_end = jnp.exp(csL)
    xw = xs * f_in
    bd = _BlockDiag.get(consts, L, P, E)

    Gs = range(G)
    Bg = [_b(Bm[:, g * N:(g + 1) * N]) for g in Gs]
    Cg = [_b(Cm[:, g * N:(g + 1) * N]) for g in Gs]
    Sg = [S_ref[g] for g in Gs]
    yield
    CB = [_nt(Cg[g], _cat0(*[Bg[g]] * E)) for g in Gs]
    CS = [_nt(Cg[g], _b(Sg[g])) for g in Gs]
    Mg = [_b(CB[g] * Mall[:, g * EL:(g + 1) * EL]) for g in Gs]
    yield
    yi = [_nn(Mg[g], bd.heads(_b(xs[:, g * EP:(g + 1) * EP]))) for g in Gs]
    upd = [_tn(_b(xw[:, g * EP:(g + 1) * EP]), Bg[g]) for g in Gs]
    yield
    for g in Gs:
        scale = _cat0(*[jnp.broadcast_to(decay_end[:, h:h + 1], (P, 1)) for h in range(g * E, (g + 1) * E)])
        S_ref[g] = scale * Sg[g] + upd[g]

    yv = (_cat1(*yi) + _cat1(*CS) * f_out + drow_ref[...] * xs) * _silu(sz)
    DG = D // G
    parts = []
    for g in range(G):
        seg = yv[:, g * DG:(g + 1) * DG]
        parts.append(seg * lax.rsqrt(jnp.mean(seg * seg, axis=1, keepdims=True) + SSD_GN_EPS))
    y_ref[0, rows, :] = (jnp.concatenate(parts, axis=1) * nw_ref[...]).astype(y_ref.dtype)


def _lane_expander(n_heads, width):
    return (jnp.arange(LANE)[:, None] == jnp.arange(n_heads * width)[None, :] // width).astype(BF16)


def _ssd_spec(sz, xbc, e, conv0, S0, p, L, TB):
    B, T = e.shape[:2]
    H, P, N = S0.shape[1:]
    D = H * P
    DC = conv0.shape[-1]
    G = (DC - D) // (2 * N)
    KW = p["s_cw"].shape[0]
    params = [p["s_cw"], p["s_cb"], p["s_alog"], p["s_drow"], p["s_nw"],
              _lane_expander(H, L), _lane_expander(H, P)]
    return dict(
        body=functools.partial(_ssd_part, L=L, H=H, P=P, G=G, N=N, KW=KW), stride=1, stacked=1, n_batched=5,
        inputs=[sz[0], xbc[0], e, conv0, S0] + params,
        in_specs=[_time_block(TB, D, sz[1]), _time_block(TB, DC, xbc[1]), _time_block(TB, E_W),
                  _per_b(conv0), _per_b(S0)] + [_full(a) for a in params],
        out_specs=[_time_block(TB, D), _per_b(S0)],
        out_shape=[jax.ShapeDtypeStruct((B, T, D), BF16), jax.ShapeDtypeStruct(S0.shape, F32)],
        scratch=[pltpu.VMEM((SUBLANE, DC), F32), pltpu.VMEM((G, H // G * P, N), F32)],
    )


def _dot_split(dot, x, rhs_hi, rhs_lo):
    xh, xl = _split(x)
    m = x.shape[0]
    r = dot(_cat0(xh, xl), rhs_hi)
    return r[:m] + r[m:] + dot(xh, rhs_lo)


class _BlockDiag:
    @classmethod
    def get(cls, consts, L, HD, n):
        return _memo(consts, ("blockdiag", L, HD, n), lambda: cls(L, HD, n))

    def __init__(self, L, HD, n):
        self.hm = [((_iota2((L, n * HD), 1) // HD) == h).astype(BF16) for h in range(n)]
        self.cm = [((_iota2((L, n * L), 1) // L) == h).astype(BF16) for h in range(n)]

    def heads(self, xb):
        return _cat0(*[xb * m for m in self.hm])

    def cols(self, xb):
        return _cat0(*[xb * m for m in self.cm])


def _unit_lower_solve(consts, As, rhss, L, bd):
    base = min(SOLVE_BASE, L)
    W = As[0].shape[1]
    def masks():
        t, i = _iota2(As[0].shape, 0), _iota2(As[0].shape, 1) % L
        return (t // base) == (i // base), (t == i).astype(F32)
    same, eye = _memo(consts, ("solve_masks", As[0].shape, L, base), masks)
    mmc = lambda x, y: _nn(_b(x), bd.cols(_b(y)))
    mmv = lambda x, v: _nn(_b(x), bd.heads(_b(v)))
    Ads = [jnp.where(same, A, 0.0) for A in As]
    Tks = [eye + Ad for Ad in Ads]
    Pks = [mmc(Ad, Ad) for Ad in Ads]
    yield
    n = 2
    while n < base:
        if 2 * n < base:
            rs = [_nn(_b(Pk), _cat1(bd.cols(_b(Tk)), bd.cols(_b(Pk)))) for Tk, Pk in zip(Tks, Pks)]
            Tks = [Tk + r[:, :W] for Tk, r in zip(Tks, rs)]
            Pks = [r[:, W:] for r in rs]
        else:
            Tks = [Tk + mmc(Pk, Tk) for Tk, Pk in zip(Tks, Pks)]
        n *= 2
        yield
    nb = L // base
    if nb == 1:
        return [mmv(Tk, V) for Tk, V in zip(Tks, rhss)]
    HW = rhss[0].shape[1]
    rs = [_nn(_b(Tk), _cat1(bd.heads(_b(V)), bd.cols(_b(jnp.where(same, 0.0, A)))))
          for Tk, V, A in zip(Tks, rhss, As)]
    Xs, Ms = [r[:, :HW] for r in rs], [r[:, HW:] for r in rs]
    n = 1
    while n < nb:
        yield
        if 2 * n < nb:
            rs = [_nn(_b(M), _cat1(bd.heads(_b(X)), bd.cols(_b(M)))) for X, M in zip(Xs, Ms)]
            Xs = [X + r[:, :HW] for X, r in zip(Xs, rs)]
            Ms = [r[:, HW:] for r in rs]
        else:
            Xs = [X + mmv(M, X) for X, M in zip(Xs, Ms)]
        n *= 2
    return Xs


def _rwkv_part(phase, rows, consts, rsh_ref, rz_ref, e_ref, sha_ref, shb_ref, S0_ref,
               mua_ref, mub_ref, w0_ref, w2h_ref, w2l_ref, a0_ref, a2_ref, kk_ref, ka_ref, rk_ref, lnw_ref, lnb_ref,
               y_ref, S1_ref, xpa_ref, xpb_ref, S_ref, *, L, H, HD):
    D = H * HD
    GW = RWKV_PACK * HD
    NG = H // RWKV_PACK
    CW = RWKV_PACK * L
    if phase == "init":
        xpa_ref[...] = jnp.zeros(xpa_ref.shape, F32)
        xpb_ref[...] = jnp.zeros(xpb_ref.shape, F32)
        xpa_ref[SUBLANE - 1:SUBLANE, :] = sha_ref[0]
        xpb_ref[SUBLANE - 1:SUBLANE, :] = shb_ref[0]
        S_ref[...] = jnp.zeros(S_ref.shape, F32)
        for h in range(H):
            g, j = divmod(h, RWKV_PACK)
            S_ref[g, j * HD:(j + 1) * HD, j * HD:(j + 1) * HD] = S0_ref[0, h]
        return
    if phase == "final":
        for h in range(H):
            g, j = divmod(h, RWKV_PACK)
            S1_ref[0, h] = S_ref[g, j * HD:(j + 1) * HD, j * HD:(j + 1) * HD]
        return

    cur = rsh_ref[0, rows, :].astype(F32)
    rz = rz_ref[0, rows, :].astype(F32)
    rx = cur + (_shifted_rows(xpa_ref, cur, [1], L)[0] - cur) * mua_ref[...]
    lo_cur = e_ref[0, rows, E_LO:E_LO + LANE]
    lo = lo_cur + (_shifted_rows(xpb_ref, lo_cur, [1], L)[0] - lo_cur) * mub_ref[...]
    rr, kr, vr = rx[:, 0:D], rx[:, D:2 * D], rx[:, 2 * D:3 * D]
    yield

    def seg_masks():
        seg_ = (_iota2((GW, GW), 0) // HD) == (_iota2((GW, GW), 1) // HD)
        return seg_, seg_.astype(BF16), jnp.where(seg_, 1.0 / HD, 0.0).astype(BF16)
    seg, segb, seg_mean = _memo(consts, ("rwkv_seg", GW, HD), seg_masks)
    assert HD & (HD - 1) == 0

    def head_sum(x, w=segb):
        xb = _b(x)
        r = _nn(_cat0(*[xb[:, g * GW:(g + 1) * GW] for g in range(NG)]), w)
        return _cat1(*[r[g * L:(g + 1) * L] for g in range(NG)])

    z = w0_ref[...] + _dot_split(_nn, jnp.tanh(lo), w2h_ref[...], w2l_ref[...])
    lw = -RWKV_DECAY_SCALE * jax.nn.sigmoid(z)
    a_sig = jax.nn.sigmoid(a0_ref[...] + _nn(_b(lo), a2_ref[...]))
    yield
    kk = kr * kk_ref[...]
    kk = kk / jnp.maximum(jnp.sqrt(head_sum(kk * kk)), 1e-12)
    kmod = kr * (1.0 + (a_sig - 1.0) * ka_ref[...])
    kb = kk * a_sig
    yield

    trib = _memo(consts, ("trib", L), lambda: _tri_consts(consts, L)[1].astype(BF16))
    l1, l2 = _split(lw)
    cum = _nn(trib, l1) + _nn(trib, l2)
    cumL = cum[L - 1:L, :]
    yield
    p_in = jnp.exp(cum)
    p_inv = 1.0 / p_in
    At = -kk * jnp.exp(cum - lw)
    Bt, Kt, Rt = kb * p_inv, kmod * p_inv, rr * p_in
    PL = jnp.exp(cumL)

    def tri_masks():
        t, i = _iota2((L, 2 * CW), 0), _iota2((L, 2 * CW), 1) % L
        return i < t, i <= t
    strict, incl = _memo(consts, ("rwkv_tri", L, CW), tri_masks)
    bd = _BlockDiag.get(consts, L, HD, RWKV_PACK)

    def wave(groups):
        G = range(len(groups))
        gsl = [slice(g * GW, (g + 1) * GW) for g in groups]
        Sbd = [S_ref[g] for g in groups]
        AR = [_cat0(_b(At[:, s]), _b(Rt[:, s])) for s in gsl]
        Bb, Kb, Vb = ([_b(x[:, s]) for s in gsl] for x in (Bt, Kt, vr))
        Bst, Kst, Vst = ([bd.heads(x) for x in xs] for xs in (Bb, Kb, Vb))
        yield
        g1 = [_nt(AR[g], _cat0(Bst[g], Kst[g])) for g in G]
        s1 = [_nt(AR[g], _b(Sbd[g])) for g in G]
        GA = [jnp.where(strict, g1[g][:L], 0.0) for g in G]
        GR = [jnp.where(incl, g1[g][L:], 0.0) for g in G]
        yield
        rhs = [s1[g][:L] + _nn(_b(GA[g][:, CW:]), Vst[g]) for g in G]
        yield
        U = yield from _unit_lower_solve(consts, [GA[g][:, :CW] for g in G], rhs, L, bd)
        yield
        Ub = [_b(x) for x in U]
        ys_ = [s1[g][L:] + _nn(_b(GR[g]), _cat0(bd.heads(Ub[g]), Vst[g])) for g in G]
        yield
        for g in G:
            upd = _tn(_cat0(Ub[g], Vb[g]), _cat0(Bb[g], Kb[g]))
            PLg = PL[:, gsl[g]]
            for j in range(RWKV_PACK):
                rs = slice(j * HD, (j + 1) * HD)
                lo_ = (j * HD // LANE) * LANE
                ws = slice(lo_, max(lo_ + LANE, (j + 1) * HD))
                S_ref[groups[g], rs, ws] = ((Sbd[g][rs, ws] + jnp.where(seg[rs, ws], upd[rs, ws], 0.0))
                                            * PLg[:, ws])
        yield
        return ys_

    ys = []
    for first in range(0, NG, RWKV_WAVE):
        ys += yield from wave(list(range(first, min(first + RWKV_WAVE, NG))))

    y = jnp.concatenate(ys, axis=1)
    mu = head_sum(y, seg_mean)
    yz = y - mu
    yield
    var = head_sum(yz * yz, seg_mean)
    yn = yz * lax.rsqrt(var + RWKV_GN_EPS) * lnw_ref[...] + lnb_ref[...]
    bonus = head_sum(rr * kmod * rk_ref[...]) * vr
    y_ref[0, rows, :] = ((yn + bonus) * _silu(rz)).astype(y_ref.dtype)


def _rwkv_spec(rsh, rz, e, sha, shb, S0, p, L, TB):
    B, T = e.shape[:2]
    H, HD = S0.shape[1], S0.shape[2]
    D = H * HD
    GW = RWKV_PACK * HD
    params = [p["r_mua"], p["r_mub"], p["r_w0"], p["r_w2h"], p["r_w2l"], p["r_a0"], p["r_a2"], p["r_kk"], p["r_ka"],
              p["r_rk"], p["r_lnw"], p["r_lnb"]]
    return dict(
        body=functools.partial(_rwkv_part, L=L, H=H, HD=HD), stride=3, stacked=1, n_batched=6,
        inputs=[rsh[0], rz[0], e, sha, shb, S0] + params,
        in_specs=[_time_block(TB, 3 * D, rsh[1]), _time_block(TB, D, rz[1]), _time_block(TB, E_W),
                  _per_b(sha), _per_b(shb), _per_b(S0)]
        + [_full(a) for a in params],
        out_specs=[_time_block(TB, D), _per_b(S0)],
        out_shape=[jax.ShapeDtypeStruct((B, T, D), BF16), jax.ShapeDtypeStruct(S0.shape, F32)],
        scratch=[pltpu.VMEM((SUBLANE, 3 * D), F32), pltpu.VMEM((SUBLANE, LANE), F32),
                 pltpu.VMEM((H // RWKV_PACK, GW, GW), F32)],
    )


def _mixers_kernel(*refs, parts, n_alias, strides, chunks, L, R):
    n_in, n_out = sum(p[1] for p in parts), sum(p[2] for p in parts)
    groups, i, o, s = [], 0, n_in + n_alias, n_in + n_alias + n_out
    for (body, ni, no, ns, nb), stride in zip(parts, strides):
        for r in range(R):
            row = lambda ref: ref.at[r:r + 1]
            ins = tuple(row(x) for x in refs[i:i + nb]) + refs[i + nb:i + ni]
            outs = tuple(row(x) for x in refs[o:o + no])
            groups.append((body, ins + outs + refs[s + r * ns:s + (r + 1) * ns], stride))
        i, o, s = i + ni, o + no, s + R * ns
    c = pl.program_id(1)

    consts = {}

    def run(phase, rows):
        live = [(body(phase, rows, consts, *r), stride) for body, r, stride in groups]
        while live:
            nxt = []
            for gen, stride in live:
                try:
                    for _ in range(stride):
                        next(gen)
                    nxt.append((gen, stride))
                except StopIteration:
                    pass
            live = nxt

    @pl.when(c == 0)
    def _():
        run("init", None)

    for k in range(chunks):
        run("main", slice(k * L, (k + 1) * L))

    @pl.when(c == pl.num_programs(1) - 1)
    def _():
        run("final", None)


def _mixers(specs, B, T, L, TB, R, layer, depth, stacked):
    assert B % R == 0
    rows = lambda spec: pl.BlockSpec((R,) + tuple(spec.block_shape[1:]), spec.index_map)
    parts = tuple((s["body"], len(s["inputs"]), len(s["out_shape"]), len(s["scratch"]), s["n_batched"])
                  for s in specs)
    n_in = sum(p[1] for p in parts)
    in_specs = [rows(x) if k < s["n_batched"] else x for s in specs for k, x in enumerate(s["in_specs"])]
    out_specs, out_shape, alias_in, aliases = [], [], [], {}
    for s, acc in zip(specs, stacked):
        for j, (spec, shape) in enumerate(zip(s["out_specs"], s["out_shape"])):
            spec = rows(spec)
            if j == s["stacked"]:
                full = (depth,) + shape.shape
                aliases[n_in + len(alias_in)] = len(out_shape)
                alias_in.append(jnp.zeros(full, shape.dtype) if acc is None else acc)
                spec = pl.BlockSpec((None,) + tuple(spec.block_shape),
                                    lambda b, c, im=spec.index_map: (layer,) + tuple(im(b, c)))
                shape = jax.ShapeDtypeStruct(full, shape.dtype)
            out_specs.append(spec)
            out_shape.append(shape)
    outs = pl.pallas_call(
        functools.partial(_mixers_kernel, parts=parts, n_alias=len(alias_in),
                          strides=tuple(s["stride"] for s in specs), chunks=TB // L, L=L, R=R),
        grid=(B // R, T // TB),
        in_specs=in_specs + [pl.BlockSpec(memory_space=pl.ANY)] * len(alias_in),
        out_specs=out_specs,
        out_shape=out_shape,
        scratch_shapes=[x for s in specs for _ in range(R) for x in s["scratch"]],
        input_output_aliases=aliases,
        compiler_params=pltpu.CompilerParams(dimension_semantics=("parallel", "arbitrary"),
                                             vmem_limit_bytes=VMEM_LIMIT_BYTES),
        name="mixers",
    )(*[x for s in specs for x in s["inputs"]], *alias_in)
    res, k = [], 0
    for s in specs:
        res.append(outs[k:k + len(s["out_shape"])])
        k += len(s["out_shape"])
    return res


def _bd_tiles(w):
    dep, nb, qb, _ = w.shape
    rows = w.reshape(dep, nb * qb // MXU_DIM, MXU_DIM, qb)
    idx = jnp.arange(MXU_DIM)
    full = jnp.take(rows, idx % qb, axis=-1)
    return jnp.where((idx[:, None] // qb) == (idx[None, :] // qb), full, 0.0)


def _pad_lanes(a, width=LANE):
    return jnp.pad(a, [(0, 0)] * (a.ndim - 1) + [(0, width - a.shape[-1])])


def _prep_params(D, w_in, b_gate, m_conv_w, m_conv_b, m_wq, m_wk, m_wv, m_b_if, m_norm_w, m_skip, m_w_out,
                 s_conv_w, s_conv_b, s_dt_bias, s_A_log, s_D, s_norm_w, s_w_out,
                 r_mu, r_w0, r_w2, r_a0, r_a2, r_k_k, r_k_a, r_r_k, r_ln_w, r_ln_b, r_w_out, w_out):
    H_S = D // P_S
    DC = D + 2 * G_S * N_S
    n_lo = LORA_W + LORA_A
    assert n_lo == LANE and 2 * H_M <= SUBLANE and H_S <= LANE
    off_gate = 0
    off_m = off_gate + 3 * D
    off_if = off_m + 3 * D
    off_sz = off_if + 2 * H_M
    off_dt = off_sz + D + DC
    off_rsh = off_dt + H_S
    off_rz = off_rsh + 3 * D + n_lo
    assert off_rz + D == w_in.shape[-1]
    col = lambda o, n: w_in[:, :, o:o + n]
    row = lambda a: a[:, None, :]
    cat = lambda *a: jnp.concatenate(a, axis=-1)
    DH = D // H_M
    zl = jnp.zeros_like(r_w2)
    w2_pad = jnp.concatenate([r_w2, zl], axis=1)
    w2_hi = _b(w2_pad)
    return dict(
        wS=_b(cat(col(off_gate, 3 * D), col(off_m + 2 * D, D))),
        bS=row(cat(b_gate, jnp.zeros((w_in.shape[0], D), F32))),
        wL=_b(cat(col(off_m + D, D), col(off_sz, D), col(off_rz, D))),
        wN=_b(cat(col(off_sz + D, DC), col(off_m, D), col(off_rsh, 3 * D))),
        wE=_b(cat(col(off_rsh + 3 * D, n_lo), _pad_lanes(col(off_if, 2 * H_M)), _pad_lanes(col(off_dt, H_S)))),
        m_cw=m_conv_w, m_cb=row(m_conv_b),
        m_wq=_b(_bd_tiles(m_wq)), m_wk=_b(_bd_tiles(m_wk) * DH ** -0.5), m_wv=_b(_bd_tiles(m_wv)),
        bE=row(jnp.concatenate([jnp.zeros((w_in.shape[0], n_lo), F32), _pad_lanes(m_b_if),
                                _pad_lanes(s_dt_bias)], axis=-1)),
        m_nw=row(m_norm_w), m_skip=row(m_skip), m_wo=_b(m_w_out),
        s_cw=s_conv_w, s_cb=row(s_conv_b), s_alog=row(_pad_lanes(s_A_log)),
        s_drow=row(jnp.repeat(s_D, P_S, axis=-1)), s_nw=row(s_norm_w), s_wo=_b(s_w_out),
        r_mua=row(r_mu[:, :3 * D]), r_mub=row(r_mu[:, 3 * D:]),
        r_w0=row(r_w0), r_w2h=w2_hi, r_w2l=_b(w2_pad - w2_hi.astype(F32)),
        r_a0=row(r_a0), r_a2=_b(jnp.concatenate([zl, r_a2], axis=1)),
        r_kk=row(r_k_k), r_ka=row(r_k_a), r_rk=row(r_r_k.reshape(r_r_k.shape[0], -1)),
        r_lnw=row(r_ln_w), r_lnb=row(r_ln_b), r_wo=_b(r_w_out),
        wo=_b(w_out),
    )


def _row_tile(n, cap):
    t = min(n, cap)
    while n % t:
        t //= 2
    return t


def _run(x, states, prep, norm_w, final_norm_w):
    B, T, D = x.shape
    depth = norm_w.shape[0]
    mC, mn, mm, mconv, sS, sconv, rS, rsh = states
    L = CHUNK if T % CHUNK == 0 else T
    TB = L * MIXER_CHUNKS_PER_STEP if T % (L * MIXER_CHUNKS_PER_STEP) == 0 else L
    R = MIXER_ROWS_SHORT_SEQ if (T == TB and B % MIXER_ROWS_SHORT_SEQ == 0) else 1
    KW = mconv.shape[2] + 1
    DC = sconv.shape[-1]
    assert T >= KW - 1 and L % SUBLANE == 0
    assert DC % D == 0 and (DC + D) % (3 * D) == 0
    N = B * T
    tm = _row_tile(N, 4096)
    tmo = _row_tile(N, 512)
    x2 = x.reshape(N, D)
    h = _rmsnorm(x2, norm_w[0][None, :], BF16, tm)
    new = [[] for _ in range(5)]
    stacked = (None, None, None)
    for l in range(depth):
        p = {k: v[l] for k, v in prep.items()}
        def proj(w, dt, bias=None):
            nw_ = w.shape[2]
            bias = jnp.zeros((1, nw_), F32) if bias is None else bias
            return _matmul(h, w, bias, l, tm, 1024 if nw_ % 1024 == 0 else nw_, dt)

        uS = proj(prep["wS"], BF16, p["bS"])
        uL = proj(prep["wL"], BF16).reshape(B, T, -1)
        uN = proj(prep["wN"], BF16).reshape(B, T, -1)
        uE = proj(prep["wE"], F32, p["bE"]).reshape(B, T, -1)
        uS3 = uS.reshape(B, T, -1)
        (yr, rS_all), (hm, mC_all, mn1, mm1), (ys, sS_all) = _mixers(
            [_rwkv_spec((uN, (DC + D) // (3 * D)), (uL, 2), uE, rsh[l][..., :3 * D], rsh[l][..., 3 * D:], rS[l], p, L, TB),
             _mlstm_spec((uN, DC // D), (uL, 0), (uS3, 3), uE, mconv[l], mC[l], mn[l], mm[l], p, L, TB),
             _ssd_spec((uL, 1), (uN, 0), uE, sconv[l], sS[l], p, L, TB)],
            B, T, L, TB, R, l, depth, stacked)
        stacked = (rS_all, mC_all, sS_all)
        mm1 = mm1[..., 0]
        last = l == depth - 1
        nw = (final_norm_w if last else norm_w[l + 1])[None, :]
        x2, h = _out_proj(hm.reshape(N, D), ys.reshape(N, D), yr.reshape(N, D), uS, x2,
                          prep["m_wo"], prep["s_wo"], prep["r_wo"], prep["wo"], l, nw, F32 if last else BF16, tmo)
        tail = uN[:, T - (KW - 1):].astype(F32)
        outs = (mn1, mm1, tail[..., DC:DC + D], tail[..., 0:DC],
                jnp.concatenate([tail[:, KW - 2:, DC + D:], uE[:, T - 1:, E_LO:E_LO + LANE]], axis=-1))
        for acc, s in zip(new, outs):
            acc.append(s)
    mn_all, mm_all, mconv_all, sconv_all, rsh_all = (jnp.stack(s) for s in new)
    rS_all, mC_all, sS_all = stacked
    return h.reshape(B, T, D), [mC_all, mn_all, mm_all, mconv_all, sS_all, sconv_all, rS_all, rsh_all]


def kernel(x_prompt, x_sample, state_mlstm_C, state_mlstm_n, state_mlstm_m, state_mlstm_conv, state_ssd,
           state_ssd_conv, state_rwkv, state_rwkv_shift, norm_w, w_in, b_gate, m_conv_w, m_conv_b, m_wq, m_wk,
           m_wv, m_b_if, m_norm_w, m_skip, m_w_out, s_conv_w, s_conv_b, s_dt_bias, s_A_log, s_D, s_norm_w,
           s_w_out, r_mu, r_w0, r_w2, r_a0, r_a2, r_k_k, r_k_a, r_r_k, r_ln_w, r_ln_b, r_w_out, w_out,
           final_norm_w):
    D = x_prompt.shape[-1]
    prep = _prep_params(D, w_in, b_gate, m_conv_w, m_conv_b, m_wq, m_wk, m_wv, m_b_if, m_norm_w, m_skip, m_w_out,
                        s_conv_w, s_conv_b, s_dt_bias, s_A_log, s_D, s_norm_w, s_w_out,
                        r_mu, r_w0, r_w2, r_a0, r_a2, r_k_k, r_k_a, r_r_k, r_ln_w, r_ln_b, r_w_out, w_out)
    sample_states = (state_mlstm_C, state_mlstm_n, state_mlstm_m, state_mlstm_conv, state_ssd,
                     state_ssd_conv, state_rwkv, state_rwkv_shift)
    Bp = x_prompt.shape[0]
    zero_states = tuple(jnp.zeros(s.shape[:1] + (Bp,) + s.shape[2:], s.dtype) for s in sample_states)
    y_p, st_p = _run(x_prompt, zero_states, prep, norm_w, final_norm_w)
    y_s, st_s = _run(x_sample, sample_states, prep, norm_w, final_norm_w)
    return (y_p, y_s, *st_p, *st_s)
```

```python
import functools

import jax
import jax.numpy as jnp
from jax import lax
from jax.experimental import pallas as pl
from jax.experimental.pallas import tpu as pltpu

F32 = jnp.float32
BF16 = jnp.bfloat16

CHUNK = 64
NORM_EPS = 1e-6
H_M = 4
MLSTM_LN_EPS = 1e-5
P_S = 64
G_S = 4
N_S = 128
SSD_GN_EPS = 1e-5
HD_R = 64
LORA_W = 64
LORA_A = 64
RWKV_GN_EPS = 64e-5
RWKV_DECAY_SCALE = 0.6065306597126334
RWKV_WAVE = 4
RWKV_PACK = 4
SOLVE_BASE = 16
MIXER_CHUNKS_PER_STEP = 4
MIXER_ROWS_SHORT_SEQ = 4

LANE = 128
SUBLANE = 8
MXU_DIM = 256
VMEM_LIMIT_BYTES = 56 * 1024 * 1024

E_LO = 0 * LANE
E_IF = 1 * LANE
E_DT = 2 * LANE
E_W = 3 * LANE


def _nn(a, b, prec=None):
    return jnp.dot(a, b, precision=prec, preferred_element_type=F32)


def _nt(a, b, prec=None):
    return lax.dot_general(a, b, (((1,), (1,)), ((), ())), precision=prec, preferred_element_type=F32)


def _tn(a, b, prec=None):
    return lax.dot_general(a, b, (((0,), (0,)), ((), ())), precision=prec, preferred_element_type=F32)


def _b(x):
    return x.astype(BF16)


def _bf16_terms(x, n):
    terms, r = [], x
    for _ in range(n):
        t = _b(r)
        terms.append(t)
        r = r - t.astype(F32)
    return terms


def _split(x):
    return tuple(_bf16_terms(x, 2))


def _cat0(*xs):
    return jnp.concatenate(xs, axis=0)


def _cat1(*xs):
    return jnp.concatenate(xs, axis=1)


def _softplus(x):
    return jnp.maximum(x, 0.0) + jnp.log1p(jnp.exp(-jnp.abs(x)))


def _log_sigmoid(x):
    return -_softplus(-x)


def _silu(x):
    return x * jax.nn.sigmoid(x)


def _iota2(shape, dim):
    return lax.broadcasted_iota(jnp.int32, shape, dim)


def _memo(consts, key, build):
    if key not in consts:
        consts[key] = build()
    return consts[key]


def _tri_consts(consts, L):
    def build():
        r, c = _iota2((L, L), 0), _iota2((L, L), 1)
        causal = c <= r
        return causal, causal.astype(F32), (r <= c).astype(F32)
    return _memo(consts, ("tri", L), build)


def _eye(consts, n):
    return _memo(consts, ("eye", n), lambda: (_iota2((n, n), 0) == _iota2((n, n), 1)).astype(F32))


def _shifted_rows(xp_ref, x, shifts, L):
    xfull = _cat0(xp_ref[...], x)
    xp_ref[...] = x[L - SUBLANE:L, :]
    return [pltpu.roll(xfull, s, 0)[SUBLANE:SUBLANE + L, :] for s in shifts]


def _causal_conv(xp_ref, x, cw_ref, cb_ref, L, KW):
    taps = _shifted_rows(xp_ref, x, list(range(KW - 1, 0, -1)), L) + [x]
    y = cb_ref[...]
    for j in range(KW):
        y = y + taps[j] * cw_ref[j:j + 1, :]
    return y


def _norm_kernel(x_ref, w_ref, o_ref):
    x = x_ref[...]
    y = x * lax.rsqrt(jnp.mean(x * x, axis=-1, keepdims=True) + NORM_EPS)
    o_ref[...] = (y * w_ref[...]).astype(o_ref.dtype)


def _rmsnorm(x, w, out_dtype, tm):
    n, d = x.shape
    return pl.pallas_call(
        _norm_kernel,
        grid=(n // tm,),
        in_specs=[pl.BlockSpec((tm, d), lambda i: (i, 0)), pl.BlockSpec((1, d), lambda i: (0, 0))],
        out_specs=pl.BlockSpec((tm, d), lambda i: (i, 0)),
        out_shape=jax.ShapeDtypeStruct((n, d), out_dtype),
        compiler_params=pltpu.CompilerParams(dimension_semantics=("parallel",), vmem_limit_bytes=VMEM_LIMIT_BYTES),
        name="rmsnorm",
    )(x, w)


def _matmul_kernel(h_ref, w_ref, b_ref, o_ref):
    o_ref[...] = (_nn(h_ref[...], w_ref[...]) + b_ref[...]).astype(o_ref.dtype)


def _matmul(h, w, b, layer, tm, tn, out_dtype):
    n, d = h.shape
    nw = w.shape[2]
    return pl.pallas_call(
        _matmul_kernel,
        grid=(nw // tn, n // tm),
        in_specs=[pl.BlockSpec((tm, d), lambda j, i: (i, 0)),
                  pl.BlockSpec((None, d, tn), lambda j, i: (layer, 0, j)),
                  pl.BlockSpec((1, tn), lambda j, i: (0, j))],
        out_specs=pl.BlockSpec((tm, tn), lambda j, i: (i, j)),
        out_shape=jax.ShapeDtypeStruct((n, nw), out_dtype),
        compiler_params=pltpu.CompilerParams(dimension_semantics=("parallel", "parallel"),
                                             vmem_limit_bytes=VMEM_LIMIT_BYTES),
        name="proj_in",
    )(h, w, b)


def _out_kernel(hm_ref, ys_ref, yr_ref, g_ref, x_ref, wm_ref, ws_ref, wr_ref, wo_ref, nw_ref,
                xo_ref, ho_ref, *, D):
    g = jax.nn.sigmoid(g_ref[...].astype(F32))
    merged = (g[:, 0:D] * _nn(hm_ref[...], wm_ref[...])
              + g[:, D:2 * D] * _nn(ys_ref[...], ws_ref[...])
              + g[:, 2 * D:3 * D] * _nn(yr_ref[...], wr_ref[...]))
    out = x_ref[...] + _nn(_b(merged), wo_ref[...])
    xo_ref[...] = out
    y = out * lax.rsqrt(jnp.mean(out * out, axis=-1, keepdims=True) + NORM_EPS)
    ho_ref[...] = (y * nw_ref[...]).astype(ho_ref.dtype)


def _out_proj(hm, ys, yr, gate, x, wm, ws, wr, wo, layer, nw, h_dtype, tm):
    n, d = x.shape
    row = lambda w: pl.BlockSpec((tm, w), lambda i: (i, 0))
    full = lambda a: pl.BlockSpec(a.shape, lambda i: (0, 0))
    wspec = pl.BlockSpec((None, d, d), lambda i: (layer, 0, 0))
    return pl.pallas_call(
        functools.partial(_out_kernel, D=d),
        grid=(n // tm,),
        in_specs=[row(d), row(d), row(d), row(3 * d), row(d),
                  wspec, wspec, wspec, wspec, full(nw)],
        out_specs=[row(d), row(d)],
        out_shape=[jax.ShapeDtypeStruct((n, d), F32), jax.ShapeDtypeStruct((n, d), h_dtype)],
        compiler_params=pltpu.CompilerParams(dimension_semantics=("parallel",), vmem_limit_bytes=VMEM_LIMIT_BYTES),
        name="out_proj",
    )(hm, ys, yr, gate, x, wm, ws, wr, wo, nw)


def _blockdiag_apply(a, w_ref):
    ab = _b(a)
    return [_nn(ab[:, t * MXU_DIM:(t + 1) * MXU_DIM], w_ref[t]) for t in range(w_ref.shape[0])]


def _mlstm_part(phase, rows, consts, mx_ref, mz_ref, mo_ref, e_ref, conv0_ref, C0_ref, n0_ref, m0_ref,
                cw_ref, cb_ref, wq_ref, wk_ref, wv_ref, nw_ref, skip_ref,
                h_ref, C1_ref, n1_ref, m1_ref,
                xp_ref, C_ref, n_ref, m_ref, *, L, H, DH, KW):
    D = H * DH
    if phase == "init":
        xp_ref[...] = jnp.zeros(xp_ref.shape, F32)
        xp_ref[SUBLANE - (KW - 1):SUBLANE, :] = conv0_ref[0]
        C_ref[...] = C0_ref[0]
        n_ref[...] = n0_ref[0]
        m_ref[...] = m0_ref[0]
        return
    if phase == "final":
        C1_ref[0] = C_ref[...]
        n1_ref[0] = n_ref[...]
        m1_ref[0] = m_ref[...]
        return

    mx, mz, mo = (r[0, rows, :].astype(F32) for r in (mx_ref, mz_ref, mo_ref))
    mc = _silu(_causal_conv(xp_ref, mx, cw_ref, cb_ref, L, KW))
    yield
    q = _cat1(*_blockdiag_apply(mc, wq_ref))
    k = _cat1(*_blockdiag_apply(mc, wk_ref))
    v = _cat1(*_blockdiag_apply(mx, wv_ref))

    causal, tril, triu = _tri_consts(consts, L)
    ecol = e_ref[0, rows, E_IF:E_IF + LANE]
    eyeb, trilb, triub = _memo(consts, ("mlstm_b", L), lambda: (_b(_eye(consts, LANE)), _b(tril), _b(triu)))
    eT = sum(_nt(eyeb, t) for t in _bf16_terms(ecol, 3))[0:SUBLANE, :]
    b_col = sum(_nn(trilb, t) for t in _bf16_terms(_log_sigmoid(ecol), 3))
    b_row = sum(_nn(t, triub) for t in _bf16_terms(_log_sigmoid(eT), 3))

    Hs = range(H)
    sl = [slice(h * DH, (h + 1) * DH) for h in Hs]
    qf, kf = [q[:, s] for s in sl], [k[:, s] for s in sl]
    qh, kh, vh = [_b(x) for x in qf], [_b(x) for x in kf], [_b(v[:, s]) for s in sl]
    bc, br = [b_col[:, H + h:H + h + 1] for h in Hs], [b_row[H + h:H + h + 1, :] for h in Hs]
    ic, ir = [ecol[:, h:h + 1] for h in Hs], [eT[h:h + 1, :] for h in Hs]
    mprev = [m_ref[h:h + 1, 0:1] for h in Hs]
    Ch, nh = [C_ref[h] for h in Hs], [n_ref[h:h + 1, :] for h in Hs]

    yield
    qk = [_nt(qh[h], kh[h]) for h in Hs]
    qC = [_nn(qh[h], _b(Ch[h])) for h in Hs]
    logD = [jnp.where(causal, bc[h] - br[h] + ir[h], -jnp.inf) for h in Hs]
    inter = [bc[h] + mprev[h] for h in Hs]
    m_t = [jnp.maximum(inter[h], jnp.max(logD[h], axis=1, keepdims=True)) for h in Hs]
    sc = [jnp.exp(inter[h] - m_t[h]) for h in Hs]
    s = [qk[h] * jnp.exp(logD[h] - m_t[h]) for h in Hs]
    yield
    sv = [_nn(_b(s[h]), vh[h]) for h in Hs]

    bL = [bc[h][L - 1:L, :] for h in Hs]
    m_new = [jnp.maximum(bL[h] + mprev[h], jnp.max(bL[h] - br[h] + ir[h], axis=1, keepdims=True)) for h in Hs]
    dec = [jnp.exp(bL[h] + mprev[h] - m_new[h]) for h in Hs]
    kw = [kf[h] * jnp.exp(bL[h] - bc[h] + ic[h] - m_new[h]) for h in Hs]
    upd = [_tn(_b(kw[h]), vh[h]) for h in Hs]
    yield

    outs = []
    for h in Hs:
        den = jnp.sum(s[h], axis=1, keepdims=True) + sc[h] * jnp.sum(qf[h] * nh[h], axis=1, keepdims=True)
        hc = (sv[h] + qC[h] * sc[h]) * (1.0 / jnp.maximum(jnp.abs(den), jnp.exp(-m_t[h])))
        mu = jnp.mean(hc, axis=1, keepdims=True)
        hz = hc - mu
        var = jnp.mean(hz * hz, axis=1, keepdims=True)
        outs.append(hz * lax.rsqrt(var + MLSTM_LN_EPS))
        C_ref[h] = dec[h] * Ch[h] + upd[h]
        n_ref[h:h + 1, :] = dec[h] * nh[h] + jnp.sum(kw[h], axis=0, keepdims=True)
        m_ref[h:h + 1, :] = jnp.broadcast_to(m_new[h], (1, LANE))

    hm = jnp.concatenate(outs, axis=1) * nw_ref[...]
    out = (jax.nn.sigmoid(mo) * hm + skip_ref[...] * mc) * _silu(mz)
    h_ref[0, rows, :] = out.astype(h_ref.dtype)


def _per_b(a):
    return pl.BlockSpec((1,) + a.shape[1:], lambda b, c: (b,) + (0,) * (a.ndim - 1))


def _full(a):
    return pl.BlockSpec(a.shape, lambda b, c: (0,) * a.ndim)


def _time_block(tb, w, j=0):
    return pl.BlockSpec((1, tb, w), lambda b, c: (b, c, j))


def _mlstm_spec(mx, mz, mo, e, conv0, C0, n0, m0, p, L, TB):
    B, T = e.shape[:2]
    H, DH = C0.shape[1], C0.shape[2]
    D = H * DH
    KW = p["m_cw"].shape[0]
    m0b = jnp.broadcast_to(m0[..., None], (B, H, LANE))
    params = [p["m_cw"], p["m_cb"], p["m_wq"], p["m_wk"], p["m_wv"], p["m_nw"], p["m_skip"]]
    return dict(
        body=functools.partial(_mlstm_part, L=L, H=H, DH=DH, KW=KW), stride=1, stacked=1, n_batched=8,
        inputs=[mx[0], mz[0], mo[0], e, conv0, C0, n0, m0b] + params,
        in_specs=[_time_block(TB, D, mx[1]), _time_block(TB, D, mz[1]), _time_block(TB, D, mo[1]),
                  _time_block(TB, E_W), _per_b(conv0), _per_b(C0), _per_b(n0), _per_b(m0b)]
        + [_full(a) for a in params],
        out_specs=[_time_block(TB, D), _per_b(C0), _per_b(n0), _per_b(m0b)],
        out_shape=[jax.ShapeDtypeStruct((B, T, D), BF16), jax.ShapeDtypeStruct(C0.shape, F32),
                   jax.ShapeDtypeStruct(n0.shape, F32), jax.ShapeDtypeStruct(m0b.shape, F32)],
        scratch=[pltpu.VMEM((SUBLANE, D), F32), pltpu.VMEM((H, DH, DH), F32),
                 pltpu.VMEM((H, DH), F32), pltpu.VMEM((H, LANE), F32)],
    )


def _ssd_part(phase, rows, consts, sz_ref, xbc_ref, e_ref, conv0_ref, S0_ref, cw_ref, cb_ref, alog_ref, drow_ref, nw_ref,
              xl_ref, xp_exp_ref, y_ref, S1_ref, xp_ref, S_ref, *, L, H, P, G, N, KW):
    D = H * P
    E = H // G
    EL, EP = E * L, E * P
    if phase == "init":
        xp_ref[...] = jnp.zeros(xp_ref.shape, F32)
        xp_ref[SUBLANE - (KW - 1):SUBLANE, :] = conv0_ref[0]
        S_ref[...] = S0_ref[0].reshape(G, EP, N)
        return
    if phase == "final":
        S1_ref[0] = S_ref[...].reshape(H, P, N)
        return

    sz = sz_ref[0, rows, :].astype(F32)
    xbc = _silu(_causal_conv(xp_ref, xbc_ref[0, rows, :].astype(F32), cw_ref, cb_ref, L, KW))
    xs, Bm, Cm = xbc[:, 0:D], xbc[:, D:D + G * N], xbc[:, D + G * N:]
    yield

    dt = _softplus(e_ref[0, rows, E_DT:E_DT + LANE])
    tril = _tri_consts(consts, L)[1]
    trilb = _memo(consts, ("trib", L), lambda: _b(tril))
    cs = sum(_nn(trilb, t) for t in _bf16_terms(dt * (-jnp.exp(alog_ref[...])), 3))
    csL = cs[L - 1:L, :]

    def expand(items, e_ref_):
        rows = [t for x, n in items for t in _bf16_terms(x, n)]
        r = _nn(_cat0(*rows), e_ref_[...])
        outs, k = [], 0
        for _, n in items:
            outs.append(sum(r[j * L:(j + 1) * L] for j in range(k, k + n)))
            k += n
        return outs

    yield
    csx, dtx = expand([(cs, 3), (dt, 2)], xl_ref)
    f_out, f_in = expand([(jnp.exp(cs), 2), (jnp.exp(csL - cs) * dt, 2)], xp_exp_ref)
    def masks():
        row, col = _iota2((L, H * L), 0), _iota2((L, H * L), 1) % L
        return row == col, col <= row
    diag, lower = _memo(consts, ("ssd_masks", L, H), masks)
    cs_src = jnp.sum(jnp.where(diag, csx, 0.0), axis=0, keepdims=True)
    dt_src = jnp.sum(jnp.where(diag, dtx, 0.0), axis=0, keepdims=True)
    Mall = jnp.exp(jnp.where(lower, csx - cs_src, -jnp.inf)) * dt_src
    decay_end = jnp.exp(csL)
    xw = xs * f_in
    bd = _BlockDiag.get(consts, L, P, E)

    Gs = range(G)
    Bg = [_b(Bm[:, g * N:(g + 1) * N]) for g in Gs]
    Cg = [_b(Cm[:, g * N:(g + 1) * N]) for g in Gs]
    Sg = [S_ref[g] for g in Gs]
    yield
    CB = [_nt(Cg[g], _cat0(*[Bg[g]] * E)) for g in Gs]
    CS = [_nt(Cg[g], _b(Sg[g])) for g in Gs]
    Mg = [_b(CB[g] * Mall[:, g * EL:(g + 1) * EL]) for g in Gs]
    yield
    yi = [_nn(Mg[g], bd.heads(_b(xs[:, g * EP:(g + 1) * EP]))) for g in Gs]
    upd = [_tn(_b(xw[:, g * EP:(g + 1) * EP]), Bg[g]) for g in Gs]
    yield
    for g in Gs:
        scale = _cat0(*[jnp.broadcast_to(decay_end[:, h:h + 1], (P, 1)) for h in range(g * E, (g + 1) * E)])
        S_ref[g] = scale * Sg[g] + upd[g]

    yv = (_cat1(*yi) + _cat1(*CS) * f_out + drow_ref[...] * xs) * _silu(sz)
    DG = D // G
    parts = []
    for g in range(G):
        seg = yv[:, g * DG:(g + 1) * DG]
        parts.append(seg * lax.rsqrt(jnp.mean(seg * seg, axis=1, keepdims=True) + SSD_GN_EPS))
    y_ref[0, rows, :] = (jnp.concatenate(parts, axis=1) * nw_ref[...]).astype(y_ref.dtype)


def _lane_expander(n_heads, width):
    return (jnp.arange(LANE)[:, None] == jnp.arange(n_heads * width)[None, :] // width).astype(BF16)


def _ssd_spec(sz, xbc, e, conv0, S0, p, L, TB):
    B, T = e.shape[:2]
    H, P, N = S0.shape[1:]
    D = H * P
    DC = conv0.shape[-1]
    G = (DC - D) // (2 * N)
    KW = p["s_cw"].shape[0]
    params = [p["s_cw"], p["s_cb"], p["s_alog"], p["s_drow"], p["s_nw"],
              _lane_expander(H, L), _lane_expander(H, P)]
    return dict(
        body=functools.partial(_ssd_part, L=L, H=H, P=P, G=G, N=N, KW=KW), stride=1, stacked=1, n_batched=5,
        inputs=[sz[0], xbc[0], e, conv0, S0] + params,
        in_specs=[_time_block(TB, D, sz[1]), _time_block(TB, DC, xbc[1]), _time_block(TB, E_W),
                  _per_b(conv0), _per_b(S0)] + [_full(a) for a in params],
        out_specs=[_time_block(TB, D), _per_b(S0)],
        out_shape=[jax.ShapeDtypeStruct((B, T, D), BF16), jax.ShapeDtypeStruct(S0.shape, F32)],
        scratch=[pltpu.VMEM((SUBLANE, DC), F32), pltpu.VMEM((G, H // G * P, N), F32)],
    )


def _dot_split(dot, x, rhs_hi, rhs_lo):
    xh, xl = _split(x)
    m = x.shape[0]
    r = dot(_cat0(xh, xl), rhs_hi)
    return r[:m] + r[m:] + dot(xh, rhs_lo)


class _BlockDiag:
    @classmethod
    def get(cls, consts, L, HD, n):
        return _memo(consts, ("blockdiag", L, HD, n), lambda: cls(L, HD, n))

    def __init__(self, L, HD, n):
        self.hm = [((_iota2((L, n * HD), 1) // HD) == h).astype(BF16) for h in range(n)]
        self.cm = [((_iota2((L, n * L), 1) // L) == h).astype(BF16) for h in range(n)]

    def heads(self, xb):
        return _cat0(*[xb * m for m in self.hm])

    def cols(self, xb):
        return _cat0(*[xb * m for m in self.cm])


def _unit_lower_solve(consts, As, rhss, L, bd):
    base = min(SOLVE_BASE, L)
    W = As[0].shape[1]
    def masks():
        t, i = _iota2(As[0].shape, 0), _iota2(As[0].shape, 1) % L
        return (t // base) == (i // base), (t == i).astype(F32)
    same, eye = _memo(consts, ("solve_masks", As[0].shape, L, base), masks)
    mmc = lambda x, y: _nn(_b(x), bd.cols(_b(y)))
    mmv = lambda x, v: _nn(_b(x), bd.heads(_b(v)))
    Ads = [jnp.where(same, A, 0.0) for A in As]
    Tks = [eye + Ad for Ad in Ads]
    Pks = [mmc(Ad, Ad) for Ad in Ads]
    yield
    n = 2
    while n < base:
        if 2 * n < base:
            rs = [_nn(_b(Pk), _cat1(bd.cols(_b(Tk)), bd.cols(_b(Pk)))) for Tk, Pk in zip(Tks, Pks)]
            Tks = [Tk + r[:, :W] for Tk, r in zip(Tks, rs)]
            Pks = [r[:, W:] for r in rs]
        else:
            Tks = [Tk + mmc(Pk, Tk) for Tk, Pk in zip(Tks, Pks)]
        n *= 2
        yield
    nb = L // base
    if nb == 1:
        return [mmv(Tk, V) for Tk, V in zip(Tks, rhss)]
    HW = rhss[0].shape[1]
    rs = [_nn(_b(Tk), _cat1(bd.heads(_b(V)), bd.cols(_b(jnp.where(same, 0.0, A)))))
          for Tk, V, A in zip(Tks, rhss, As)]
    Xs, Ms = [r[:, :HW] for r in rs], [r[:, HW:] for r in rs]
    n = 1
    while n < nb:
        yield
        if 2 * n < nb:
            rs = [_nn(_b(M), _cat1(bd.heads(_b(X)), bd.cols(_b(M)))) for X, M in zip(Xs, Ms)]
            Xs = [X + r[:, :HW] for X, r in zip(Xs, rs)]
            Ms = [r[:, HW:] for r in rs]
        else:
            Xs = [X + mmv(M, X) for X, M in zip(Xs, Ms)]
        n *= 2
    return Xs


def _rwkv_part(phase, rows, consts, rsh_ref, rz_ref, e_ref, sha_ref, shb_ref, S0_ref,
               mua_ref, mub_ref, w0_ref, w2h_ref, w2l_ref, a0_ref, a2_ref, kk_ref, ka_ref, rk_ref, lnw_ref, lnb_ref,
               y_ref, S1_ref, xpa_ref, xpb_ref, S_ref, *, L, H, HD):
    D = H * HD
    GW = RWKV_PACK * HD
    NG = H // RWKV_PACK
    CW = RWKV_PACK * L
    if phase == "init":
        xpa_ref[...] = jnp.zeros(xpa_ref.shape, F32)
        xpb_ref[...] = jnp.zeros(xpb_ref.shape, F32)
        xpa_ref[SUBLANE - 1:SUBLANE, :] = sha_ref[0]
        xpb_ref[SUBLANE - 1:SUBLANE, :] = shb_ref[0]
        S_ref[...] = jnp.zeros(S_ref.shape, F32)
        for h in range(H):
            g, j = divmod(h, RWKV_PACK)
            S_ref[g, j * HD:(j + 1) * HD, j * HD:(j + 1) * HD] = S0_ref[0, h]
        return
    if phase == "final":
        for h in range(H):
            g, j = divmod(h, RWKV_PACK)
            S1_ref[0, h] = S_ref[g, j * HD:(j + 1) * HD, j * HD:(j + 1) * HD]
        return

    cur = rsh_ref[0, rows, :].astype(F32)
    rz = rz_ref[0, rows, :].astype(F32)
    rx = cur + (_shifted_rows(xpa_ref, cur, [1], L)[0] - cur) * mua_ref[...]
    lo_cur = e_ref[0, rows, E_LO:E_LO + LANE]
    lo = lo_cur + (_shifted_rows(xpb_ref, lo_cur, [1], L)[0] - lo_cur) * mub_ref[...]
    rr, kr, vr = rx[:, 0:D], rx[:, D:2 * D], rx[:, 2 * D:3 * D]
    yield

    def seg_masks():
        seg_ = (_iota2((GW, GW), 0) // HD) == (_iota2((GW, GW), 1) // HD)
        return seg_, seg_.astype(BF16), jnp.where(seg_, 1.0 / HD, 0.0).astype(BF16)
    seg, segb, seg_mean = _memo(consts, ("rwkv_seg", GW, HD), seg_masks)
    assert HD & (HD - 1) == 0

    def head_sum(x, w=segb):
        xb = _b(x)
        r = _nn(_cat0(*[xb[:, g * GW:(g + 1) * GW] for g in range(NG)]), w)
        return _cat1(*[r[g * L:(g + 1) * L] for g in range(NG)])

    z = w0_ref[...] + _dot_split(_nn, jnp.tanh(lo), w2h_ref[...], w2l_ref[...])
    lw = -RWKV_DECAY_SCALE * jax.nn.sigmoid(z)
    a_sig = jax.nn.sigmoid(a0_ref[...] + _nn(_b(lo), a2_ref[...]))
    yield
    kk = kr * kk_ref[...]
    kk = kk / jnp.maximum(jnp.sqrt(head_sum(kk * kk)), 1e-12)
    kmod = kr * (1.0 + (a_sig - 1.0) * ka_ref[...])
    kb = kk * a_sig
    yield

    trib = _memo(consts, ("trib", L), lambda: _tri_consts(consts, L)[1].astype(BF16))
    l1, l2 = _split(lw)
    cum = _nn(trib, l1) + _nn(trib, l2)
    cumL = cum[L - 1:L, :]
    yield
    p_in = jnp.exp(cum)
    p_inv = 1.0 / p_in
    At = -kk * jnp.exp(cum - lw)
    Bt, Kt, Rt = kb * p_inv, kmod * p_inv, rr * p_in
    PL = jnp.exp(cumL)

    def tri_masks():
        t, i = _iota2((L, 2 * CW), 0), _iota2((L, 2 * CW), 1) % L
        return i < t, i <= t
    strict, incl = _memo(consts, ("rwkv_tri", L, CW), tri_masks)
    bd = _BlockDiag.get(consts, L, HD, RWKV_PACK)

    def wave(groups):
        G = range(len(groups))
        gsl = [slice(g * GW, (g + 1) * GW) for g in groups]
        Sbd = [S_ref[g] for g in groups]
        AR = [_cat0(_b(At[:, s]), _b(Rt[:, s])) for s in gsl]
        Bb, Kb, Vb = ([_b(x[:, s]) for s in gsl] for x in (Bt, Kt, vr))
        Bst, Kst, Vst = ([bd.heads(x) for x in xs] for xs in (Bb, Kb, Vb))
        yield
        g1 = [_nt(AR[g], _cat0(Bst[g], Kst[g])) for g in G]
        s1 = [_nt(AR[g], _b(Sbd[g])) for g in G]
        GA = [jnp.where(strict, g1[g][:L], 0.0) for g in G]
        GR = [jnp.where(incl, g1[g][L:], 0.0) for g in G]
        yield
        rhs = [s1[g][:L] + _nn(_b(GA[g][:, CW:]), Vst[g]) for g in G]
        yield
        U = yield from _unit_lower_solve(consts, [GA[g][:, :CW] for g in G], rhs, L, bd)
        yield
        Ub = [_b(x) for x in U]
        ys_ = [s1[g][L:] + _nn(_b(GR[g]), _cat0(bd.heads(Ub[g]), Vst[g])) for g in G]
        yield
        for g in G:
            upd = _tn(_cat0(Ub[g], Vb[g]), _cat0(Bb[g], Kb[g]))
            PLg = PL[:, gsl[g]]
            for j in range(RWKV_PACK):
                rs = slice(j * HD, (j + 1) * HD)
                lo_ = (j * HD // LANE) * LANE
                ws = slice(lo_, max(lo_ + LANE, (j + 1) * HD))
                S_ref[groups[g], rs, ws] = ((Sbd[g][rs, ws] + jnp.where(seg[rs, ws], upd[rs, ws], 0.0))
                                            * PLg[:, ws])
        yield
        return ys_

    ys = []
    for first in range(0, NG, RWKV_WAVE):
        ys += yield from wave(list(range(first, min(first + RWKV_WAVE, NG))))

    y = jnp.concatenate(ys, axis=1)
    mu = head_sum(y, seg_mean)
    yz = y - mu
    yield
    var = head_sum(yz * yz, seg_mean)
    yn = yz * lax.rsqrt(var + RWKV_GN_EPS) * lnw_ref[...] + lnb_ref[...]
    bonus = head_sum(rr * kmod * rk_ref[...]) * vr
    y_ref[0, rows, :] = ((yn + bonus) * _silu(rz)).astype(y_ref.dtype)


def _rwkv_spec(rsh, rz, e, sha, shb, S0, p, L, TB):
    B, T = e.shape[:2]
    H, HD = S0.shape[1], S0.shape[2]
    D = H * HD
    GW = RWKV_PACK * HD
    params = [p["r_mua"], p["r_mub"], p["r_w0"], p["r_w2h"], p["r_w2l"], p["r_a0"], p["r_a2"], p["r_kk"], p["r_ka"],
              p["r_rk"], p["r_lnw"], p["r_lnb"]]
    return dict(
        body=functools.partial(_rwkv_part, L=L, H=H, HD=HD), stride=2, stacked=1, n_batched=6,
        inputs=[rsh[0], rz[0], e, sha, shb, S0] + params,
        in_specs=[_time_block(TB, 3 * D, rsh[1]), _time_block(TB, D, rz[1]), _time_block(TB, E_W),
                  _per_b(sha), _per_b(shb), _per_b(S0)]
        + [_full(a) for a in params],
        out_specs=[_time_block(TB, D), _per_b(S0)],
        out_shape=[jax.ShapeDtypeStruct((B, T, D), BF16), jax.ShapeDtypeStruct(S0.shape, F32)],
        scratch=[pltpu.VMEM((SUBLANE, 3 * D), F32), pltpu.VMEM((SUBLANE, LANE), F32),
                 pltpu.VMEM((H // RWKV_PACK, GW, GW), F32)],
    )


def _mixers_kernel(*refs, parts, n_alias, strides, chunks, L, R):
    n_in, n_out = sum(p[1] for p in parts), sum(p[2] for p in parts)
    groups, i, o, s = [], 0, n_in + n_alias, n_in + n_alias + n_out
    for (body, ni, no, ns, nb), stride in zip(parts, strides):
        for r in range(R):
            row = lambda ref: ref.at[r:r + 1]
            ins = tuple(row(x) for x in refs[i:i + nb]) + refs[i + nb:i + ni]
            outs = tuple(row(x) for x in refs[o:o + no])
            groups.append((body, ins + outs + refs[s + r * ns:s + (r + 1) * ns], stride))
        i, o, s = i + ni, o + no, s + R * ns
    c = pl.program_id(1)

    consts = {}

    def run(phase, rows):
        live = [(body(phase, rows, consts, *r), stride) for body, r, stride in groups]
        while live:
            nxt = []
            for gen, stride in live:
                try:
                    for _ in range(stride):
                        next(gen)
                    nxt.append((gen, stride))
                except StopIteration:
                    pass
            live = nxt

    @pl.when(c == 0)
    def _():
        run("init", None)

    for k in range(chunks):
        run("main", slice(k * L, (k + 1) * L))

    @pl.when(c == pl.num_programs(1) - 1)
    def _():
        run("final", None)


def _mixers(specs, B, T, L, TB, R, layer, depth, stacked):
    assert B % R == 0
    rows = lambda spec: pl.BlockSpec((R,) + tuple(spec.block_shape[1:]), spec.index_map)
    parts = tuple((s["body"], len(s["inputs"]), len(s["out_shape"]), len(s["scratch"]), s["n_batched"])
                  for s in specs)
    n_in = sum(p[1] for p in parts)
    in_specs = [rows(x) if k < s["n_batched"] else x for s in specs for k, x in enumerate(s["in_specs"])]
    out_specs, out_shape, alias_in, aliases = [], [], [], {}
    for s, acc in zip(specs, stacked):
        for j, (spec, shape) in enumerate(zip(s["out_specs"], s["out_shape"])):
            spec = rows(spec)
            if j == s["stacked"]:
                full = (depth,) + shape.shape
                aliases[n_in + len(alias_in)] = len(out_shape)
                alias_in.append(jnp.zeros(full, shape.dtype) if acc is None else acc)
                spec = pl.BlockSpec((None,) + tuple(spec.block_shape),
                                    lambda b, c, im=spec.index_map: (layer,) + tuple(im(b, c)))
                shape = jax.ShapeDtypeStruct(full, shape.dtype)
            out_specs.append(spec)
            out_shape.append(shape)
    outs = pl.pallas_call(
        functools.partial(_mixers_kernel, parts=parts, n_alias=len(alias_in),
                          strides=tuple(s["stride"] for s in specs), chunks=TB // L, L=L, R=R),
        grid=(B // R, T // TB),
        in_specs=in_specs + [pl.BlockSpec(memory_space=pl.ANY)] * len(alias_in),
        out_specs=out_specs,
        out_shape=out_shape,
        scratch_shapes=[x for s in specs for _ in range(R) for x in s["scratch"]],
        input_output_aliases=aliases,
        compiler_params=pltpu.CompilerParams(dimension_semantics=("parallel", "arbitrary"),
                                             vmem_limit_bytes=VMEM_LIMIT_BYTES),
        name="mixers",
    )(*[x for s in specs for x in s["inputs"]], *alias_in)
    res, k = [], 0
    for s in specs:
        res.append(outs[k:k + len(s["out_shape"])])
        k += len(s["out_shape"])
    return res


def _bd_tiles(w):
    dep, nb, qb, _ = w.shape
    rows = w.reshape(dep, nb * qb // MXU_DIM, MXU_DIM, qb)
    idx = jnp.arange(MXU_DIM)
    full = jnp.take(rows, idx % qb, axis=-1)
    return jnp.where((idx[:, None] // qb) == (idx[None, :] // qb), full, 0.0)


def _pad_lanes(a, width=LANE):
    return jnp.pad(a, [(0, 0)] * (a.ndim - 1) + [(0, width - a.shape[-1])])


def _prep_params(D, w_in, b_gate, m_conv_w, m_conv_b, m_wq, m_wk, m_wv, m_b_if, m_norm_w, m_skip, m_w_out,
                 s_conv_w, s_conv_b, s_dt_bias, s_A_log, s_D, s_norm_w, s_w_out,
                 r_mu, r_w0, r_w2, r_a0, r_a2, r_k_k, r_k_a, r_r_k, r_ln_w, r_ln_b, r_w_out, w_out):
    H_S = D // P_S
    DC = D + 2 * G_S * N_S
    n_lo = LORA_W + LORA_A
    assert n_lo == LANE and 2 * H_M <= SUBLANE and H_S <= LANE
    off_gate = 0
    off_m = off_gate + 3 * D
    off_if = off_m + 3 * D
    off_sz = off_if + 2 * H_M
    off_dt = off_sz + D + DC
    off_rsh = off_dt + H_S
    off_rz = off_rsh + 3 * D + n_lo
    assert off_rz + D == w_in.shape[-1]
    col = lambda o, n: w_in[:, :, o:o + n]
    row = lambda a: a[:, None, :]
    cat = lambda *a: jnp.concatenate(a, axis=-1)
    DH = D // H_M
    zl = jnp.zeros_like(r_w2)
    w2_pad = jnp.concatenate([r_w2, zl], axis=1)
    w2_hi = _b(w2_pad)
    return dict(
        wS=_b(cat(col(off_gate, 3 * D), col(off_m + 2 * D, D))),
        bS=row(cat(b_gate, jnp.zeros((w_in.shape[0], D), F32))),
        wL=_b(cat(col(off_m + D, D), col(off_sz, D), col(off_rz, D))),
        wN=_b(cat(col(off_sz + D, DC), col(off_m, D), col(off_rsh, 3 * D))),
        wE=_b(cat(col(off_rsh + 3 * D, n_lo), _pad_lanes(col(off_if, 2 * H_M)), _pad_lanes(col(off_dt, H_S)))),
        m_cw=m_conv_w, m_cb=row(m_conv_b),
        m_wq=_b(_bd_tiles(m_wq)), m_wk=_b(_bd_tiles(m_wk) * DH ** -0.5), m_wv=_b(_bd_tiles(m_wv)),
        bE=row(jnp.concatenate([jnp.zeros((w_in.shape[0], n_lo), F32), _pad_lanes(m_b_if),
                                _pad_lanes(s_dt_bias)], axis=-1)),
        m_nw=row(m_norm_w), m_skip=row(m_skip), m_wo=_b(m_w_out),
        s_cw=s_conv_w, s_cb=row(s_conv_b), s_alog=row(_pad_lanes(s_A_log)),
        s_drow=row(jnp.repeat(s_D, P_S, axis=-1)), s_nw=row(s_norm_w), s_wo=_b(s_w_out),
        r_mua=row(r_mu[:, :3 * D]), r_mub=row(r_mu[:, 3 * D:]),
        r_w0=row(r_w0), r_w2h=w2_hi, r_w2l=_b(w2_pad - w2_hi.astype(F32)),
        r_a0=row(r_a0), r_a2=_b(jnp.concatenate([zl, r_a2], axis=1)),
        r_kk=row(r_k_k), r_ka=row(r_k_a), r_rk=row(r_r_k.reshape(r_r_k.shape[0], -1)),
        r_lnw=row(r_ln_w), r_lnb=row(r_ln_b), r_wo=_b(r_w_out),
        wo=_b(w_out),
    )


def _row_tile(n, cap):
    t = min(n, cap)
    while n % t:
        t //= 2
    return t


def _run(x, states, prep, norm_w, final_norm_w):
    B, T, D = x.shape
    depth = norm_w.shape[0]
    mC, mn, mm, mconv, sS, sconv, rS, rsh = states
    L = CHUNK if T % CHUNK == 0 else T
    TB = L * MIXER_CHUNKS_PER_STEP if T % (L * MIXER_CHUNKS_PER_STEP) == 0 else L
    R = MIXER_ROWS_SHORT_SEQ if (T == TB and B % MIXER_ROWS_SHORT_SEQ == 0) else 1
    KW = mconv.shape[2] + 1
    DC = sconv.shape[-1]
    assert T >= KW - 1 and L % SUBLANE == 0
    assert DC % D == 0 and (DC + D) % (3 * D) == 0
    N = B * T
    tm = _row_tile(N, 4096)
    tmo = _row_tile(N, 512)
    x2 = x.reshape(N, D)
    h = _rmsnorm(x2, norm_w[0][None, :], BF16, tm)
    new = [[] for _ in range(5)]
    stacked = (None, None, None)
    for l in range(depth):
        p = {k: v[l] for k, v in prep.items()}
        def proj(w, dt, bias=None):
            nw_ = w.shape[2]
            bias = jnp.zeros((1, nw_), F32) if bias is None else bias
            return _matmul(h, w, bias, l, tm, 1024 if nw_ % 1024 == 0 else nw_, dt)

        uS = proj(prep["wS"], BF16, p["bS"])
        uL = proj(prep["wL"], BF16).reshape(B, T, -1)
        uN = proj(prep["wN"], BF16).reshape(B, T, -1)
        uE = proj(prep["wE"], F32, p["bE"]).reshape(B, T, -1)
        uS3 = uS.reshape(B, T, -1)
        (yr, rS_all), (hm, mC_all, mn1, mm1), (ys, sS_all) = _mixers(
            [_rwkv_spec((uN, (DC + D) // (3 * D)), (uL, 2), uE, rsh[l][..., :3 * D], rsh[l][..., 3 * D:], rS[l], p, L, TB),
             _mlstm_spec((uN, DC // D), (uL, 0), (uS3, 3), uE, mconv[l], mC[l], mn[l], mm[l], p, L, TB),
             _ssd_spec((uL, 1), (uN, 0), uE, sconv[l], sS[l], p, L, TB)],
            B, T, L, TB, R, l, depth, stacked)
        stacked = (rS_all, mC_all, sS_all)
        mm1 = mm1[..., 0]
        last = l == depth - 1
        nw = (final_norm_w if last else norm_w[l + 1])[None, :]
        x2, h = _out_proj(hm.reshape(N, D), ys.reshape(N, D), yr.reshape(N, D), uS, x2,
                          prep["m_wo"], prep["s_wo"], prep["r_wo"], prep["wo"], l, nw, F32 if last else BF16, tmo)
        tail = uN[:, T - (KW - 1):].astype(F32)
        outs = (mn1, mm1, tail[..., DC:DC + D], tail[..., 0:DC],
                jnp.concatenate([tail[:, KW - 2:, DC + D:], uE[:, T - 1:, E_LO:E_LO + LANE]], axis=-1))
        for acc, s in zip(new, outs):
            acc.append(s)
    mn_all, mm_all, mconv_all, sconv_all, rsh_all = (jnp.stack(s) for s in new)
    rS_all, mC_all, sS_all = stacked
    return h.reshape(B, T, D), [mC_all, mn_all, mm_all, mconv_all, sS_all, sconv_all, rS_all, rsh_all]


def kernel(x_prompt, x_sample, state_mlstm_C, state_mlstm_n, state_mlstm_m, state_mlstm_conv, state_ssd,
           state_ssd_conv, state_rwkv, state_rwkv_shift, norm_w, w_in, b_gate, m_conv_w, m_conv_b, m_wq, m_wk,
           m_wv, m_b_if, m_norm_w, m_skip, m_w_out, s_conv_w, s_conv_b, s_dt_bias, s_A_log, s_D, s_norm_w,
           s_w_out, r_mu, r_w0, r_w2, r_a0, r_a2, r_k_k, r_k_a, r_r_k, r_ln_w, r_ln_b, r_w_out, w_out,
           final_norm_w):
    D = x_prompt.shape[-1]
    prep = _prep_params(D, w_in, b_gate, m_conv_w, m_conv_b, m_wq, m_wk, m_wv, m_b_if, m_norm_w, m_skip, m_w_out,
                        s_conv_w, s_conv_b, s_dt_bias, s_A_log, s_D, s_norm_w, s_w_out,
                        r_mu, r_w0, r_w2, r_a0, r_a2, r_k_k, r_k_a, r_r_k, r_ln_w, r_ln_b, r_w_out, w_out)
    sample_states = (state_mlstm_C, state_mlstm_n, state_mlstm_m, state_mlstm_conv, state_ssd,
                     state_ssd_conv, state_rwkv, state_rwkv_shift)
    Bp = x_prompt.shape[0]
    zero_states = tuple(jnp.zeros(s.shape[:1] + (Bp,) + s.shape[2:], s.dtype) for s in sample_states)
    y_p, st_p = _run(x_prompt, zero_states, prep, norm_w, final_norm_w)
    y_s, st_s = _run(x_sample, sample_states, prep, norm_w, final_norm_w)
    return (y_p, y_s, *st_p, *st_s)
```
